```python
import jax, jax.numpy as jnp
from jax import lax
import numpy as np

D_MODEL = 1024
BATCH = 8
SEQ = 2048
DEPTH = 2

GRID_W = 64
CTX_LEN = 256
EPS = 1e-6

ATTN_HEADS = 8
ATTN_KV_HEADS = 2
HEAD_DIM = 64
ATTN_REP = ATTN_HEADS // ATTN_KV_HEADS
ATTN_W = ATTN_HEADS * HEAD_DIM
KV_W = ATTN_KV_HEADS * HEAD_DIM
ROPE_THETA = 10000.0
Q_BLOCK = 128

CONV_CH = D_MODEL // 2
CONV_WIDTH = 31

IN_EVEN = ATTN_W + 2 * KV_W + 2 * CONV_CH
MIX_EVEN = ATTN_W + CONV_CH

D_INNER = 2 * D_MODEL
SSM_HEADDIM = 64
SSM_HEADS = D_INNER // SSM_HEADDIM
SSM_GROUPS = 4
HEADS_PER_GROUP = SSM_HEADS // SSM_GROUPS
D_STATE = 128
SSM_CONV = 7
CHUNK = 128
GN = SSM_GROUPS * D_STATE
CONV_DIM = D_INNER + 2 * GN
IN_ODD = D_INNER + CONV_DIM + 2 * SSM_HEADS

D_FF = 2816
N_EXPERTS = 8
TOP_K = 2
D_FF_EXPERT = 3584

N_EVEN = (DEPTH + 1) // 2
N_ODD = DEPTH // 2

kernel_name = 'hybrid_dit_attn_conv_ssd_moe'


def rms_norm(x, g):
    xf = x.astype(jnp.float32)
    y = xf * lax.rsqrt(jnp.mean(xf * xf, axis=-1, keepdims=True) + EPS)
    return (y * g.astype(jnp.float32)).astype(x.dtype)


def layer_norm(x, g, b):
    xf = x.astype(jnp.float32)
    mu = jnp.mean(xf, axis=-1, keepdims=True)
    var = jnp.mean(jnp.square(xf - mu), axis=-1, keepdims=True)
    y = (xf - mu) * lax.rsqrt(var + EPS)
    return (y * g.astype(jnp.float32) + b.astype(jnp.float32)).astype(x.dtype)


def adaln(cond, w, b):
    m = jax.nn.silu(cond) @ w + b
    return jnp.split(m[..., None, :], 6, axis=-1)


def modulate(h, shift, scale):
    return h * (1 + scale) + shift


def depthwise_conv(x, w, b):
    width = w.shape[0]
    y = lax.conv_general_dilated(x, w[:, None, :].astype(x.dtype), window_strides=(1,),
                                 padding=[(width // 2, width // 2)],
                                 dimension_numbers=('NWC', 'WIO', 'NWC'),
                                 feature_group_count=x.shape[-1])
    return y + b


def axial_rope(n_tokens):
    rows = n_tokens // GRID_W
    t_row = jnp.repeat(jnp.arange(rows, dtype=jnp.float32), GRID_W)
    t_col = jnp.tile(jnp.arange(GRID_W, dtype=jnp.float32), rows)
    axis_dim = HEAD_DIM // 2
    inv_freq = ROPE_THETA ** (-jnp.arange(0, axis_dim, 2, dtype=jnp.float32) / axis_dim)
    ang = jnp.concatenate([t_row[:, None] * inv_freq, t_col[:, None] * inv_freq], axis=-1)
    return jnp.cos(ang)[:, None, :], jnp.sin(ang)[:, None, :]


def apply_rope(x, cos, sin):
    xf = x.astype(jnp.float32).reshape(x.shape[:-1] + (HEAD_DIM // 2, 2))
    x1, x2 = xf[..., 0], xf[..., 1]
    out = jnp.stack([x1 * cos - x2 * sin, x1 * sin + x2 * cos], axis=-1)
    return out.reshape(x.shape).astype(x.dtype)


def attend(q, k, v):
    s = jnp.einsum('bqkrd,bskd->bkrqs', q, k).astype(jnp.float32) * (HEAD_DIM ** -0.5)
    p = jax.nn.softmax(s, axis=-1).astype(v.dtype)
    return jnp.einsum('bkrqs,bskd->bqkrd', p, v)


def latent_attention(q, k, v, k_ctx, v_ctx):
    b, t = q.shape[:2]
    nb = t // Q_BLOCK
    k_all = jnp.concatenate([k_ctx, k], axis=1)
    v_all = jnp.concatenate([v_ctx, v], axis=1)
    qb = q.reshape(b, nb, Q_BLOCK, ATTN_KV_HEADS, ATTN_REP, HEAD_DIM).swapaxes(0, 1)
    o = lax.map(lambda qblk: attend(qblk, k_all, v_all), qb)
    return o.swapaxes(0, 1).reshape(b, t, ATTN_W)


def context_attention(q, k, v):
    b, t = q.shape[:2]
    return attend(q.reshape(b, t, ATTN_KV_HEADS, ATTN_REP, HEAD_DIM), k, v).reshape(b, t, ATTN_W)


def even_proj(h, w_in, q_gain, k_gain):
    b, t = h.shape[:2]
    q, k, v, u = jnp.split(h @ w_in, [ATTN_W, ATTN_W + KV_W, ATTN_W + 2 * KV_W], axis=-1)
    q = rms_norm(q.reshape(b, t, ATTN_HEADS, HEAD_DIM), q_gain)
    k = rms_norm(k.reshape(b, t, ATTN_KV_HEADS, HEAD_DIM), k_gain)
    return q, k, v.reshape(b, t, ATTN_KV_HEADS, HEAD_DIM), u


def conformer_conv(u, dw_w, dw_b, ln_g, ln_b):
    a, g = jnp.split(u, 2, axis=-1)
    h = depthwise_conv(a * jax.nn.sigmoid(g), dw_w, dw_b)
    return jax.nn.silu(layer_norm(h, ln_g, ln_b))


def even_mixer(h, hc, w_in, q_gain, k_gain, dw_w, dw_b, ln_g, ln_b, w_o, need_ctx):
    q, k, v, u = even_proj(h, w_in, q_gain, k_gain)
    qc, kc, vc, uc = even_proj(hc, w_in, q_gain, k_gain)
    cos, sin = axial_rope(h.shape[1])
    q = apply_rope(q, cos, sin)
    k = apply_rope(k, cos, sin)
    a_lat = latent_attention(q, k, v, kc, vc)
    b_lat = conformer_conv(u, dw_w, dw_b, ln_g, ln_b)
    y = jnp.concatenate([a_lat, b_lat], axis=-1) @ w_o
    if not need_ctx:
        return y, None
    a_ctx = context_attention(qc, kc, vc)
    b_ctx = conformer_conv(uc, dw_w, dw_b, ln_g, ln_b)
    yc = jnp.concatenate([a_ctx, b_ctx], axis=-1) @ w_o
    return y, yc


def ssm_proj(h, w_in, conv_w, conv_b):
    b, t = h.shape[:2]
    z, xbc, dt = jnp.split(h @ w_in, [D_INNER, D_INNER + CONV_DIM], axis=-1)
    xbc = jax.nn.silu(depthwise_conv(xbc, conv_w, conv_b))
    xs, bm, cm = jnp.split(xbc, [D_INNER, D_INNER + GN], axis=-1)
    dt_f, dt_b = jnp.split(dt, 2, axis=-1)
    hs = (b, t, SSM_GROUPS, HEADS_PER_GROUP)
    return (z, xs.reshape(hs + (SSM_HEADDIM,)), bm.reshape(b, t, SSM_GROUPS, D_STATE),
            cm.reshape(b, t, SSM_GROUPS, D_STATE), dt_f.reshape(hs), dt_b.reshape(hs))


def ssd_scan(x, dt, a_coef, bm, cm, init_state, return_y):
    b, t = x.shape[:2]
    nc = t // CHUNK
    xc = (x.astype(jnp.float32) * dt[..., None]).reshape(b, nc, CHUNK, SSM_GROUPS, HEADS_PER_GROUP, SSM_HEADDIM)
    bc = bm.astype(jnp.float32).reshape(b, nc, CHUNK, SSM_GROUPS, D_STATE)
    cc = cm.astype(jnp.float32).reshape(b, nc, CHUNK, SSM_GROUPS, D_STATE)
    a = (dt * a_coef).reshape(b, nc, CHUNK, SSM_GROUPS, HEADS_PER_GROUP)
    a_cs = jnp.cumsum(jnp.moveaxis(a, 2, -1), axis=-1)
    decay_to_end = jnp.exp(a_cs[..., -1:] - a_cs)
    chunk_states = jnp.einsum('bclgn,bcgel,bclgep->bcgepn', bc, decay_to_end, xc)
    chunk_decay = jnp.exp(a_cs[..., -1])

    def carry(state, inp):
        s_c, d_c = inp
        return state * d_c[..., None, None] + s_c, state

    final, start = lax.scan(carry, init_state.astype(jnp.float32),
                            (jnp.moveaxis(chunk_states, 1, 0), jnp.moveaxis(chunk_decay, 1, 0)))
    if not return_y:
        return None, final
    start = jnp.moveaxis(start, 0, 1)
    lower = jnp.tril(jnp.ones((CHUNK, CHUNK), dtype=bool))
    seg = jnp.exp(jnp.where(lower, a_cs[..., :, None] - a_cs[..., None, :], -jnp.inf))
    cb = jnp.einsum('bclgn,bcsgn->bcgls', cc, bc)
    y_diag = jnp.einsum('bcgls,bcgels,bcsgep->bclgep', cb, seg, xc)
    y_off = jnp.einsum('bclgn,bcgepn,bcgel->bclgep', cc, start, jnp.exp(a_cs))
    return (y_diag + y_off).reshape(x.shape), final


def bidir_ssd(xs, bm, cm, dt_f, dt_b, a_log_f, a_log_b, dt_bias_f, dt_bias_b, d_skip, init_f, init_b, return_y):
    shp = (SSM_GROUPS, HEADS_PER_GROUP)
    a_f = -jnp.exp(a_log_f.astype(jnp.float32)).reshape(shp)
    a_b = -jnp.exp(a_log_b.astype(jnp.float32)).reshape(shp)
    dtf = jax.nn.softplus(dt_f.astype(jnp.float32) + dt_bias_f.astype(jnp.float32).reshape(shp))
    dtb = jax.nn.softplus(dt_b.astype(jnp.float32) + dt_bias_b.astype(jnp.float32).reshape(shp))
    flip = lambda t: jnp.flip(t, axis=1)
    y_f, fin_f = ssd_scan(xs, dtf, a_f, bm, cm, init_f, return_y)
    y_b, fin_b = ssd_scan(flip(xs), flip(dtb), a_b, flip(bm), flip(cm), init_b, return_y)
    if not return_y:
        return None, fin_f, fin_b
    y = y_f + flip(y_b) + d_skip.astype(jnp.float32).reshape(shp + (1,)) * xs.astype(jnp.float32)
    return y.astype(xs.dtype), fin_f, fin_b


def odd_mixer(h, hc, w_in, conv_w, conv_b, a_log_f, a_log_b, dt_bias_f, dt_bias_b, d_skip, gnorm, w_out, need_ctx):
    ssm_p = (a_log_f, a_log_b, dt_bias_f, dt_bias_b, d_skip)
    z, xs, bm, cm, dtf, dtb = ssm_proj(h, w_in, conv_w, conv_b)
    zc, xsc, bmc, cmc, dtfc, dtbc = ssm_proj(hc, w_in, conv_w, conv_b)
    zero = jnp.zeros((h.shape[0], SSM_GROUPS, HEADS_PER_GROUP, SSM_HEADDIM, D_STATE), jnp.float32)
    yc, fin_f, fin_b = bidir_ssd(xsc, bmc, cmc, dtfc, dtbc, *ssm_p, zero, zero, need_ctx)
    y, _, _ = bidir_ssd(xs, bm, cm, dtf, dtb, *ssm_p, fin_f, fin_b, True)

    def gated_out(yy, zz):
        return rms_norm(yy.reshape(zz.shape) * jax.nn.silu(zz), gnorm) @ w_out

    if not need_ctx:
        return gated_out(y, z), None
    return gated_out(y, z), gated_out(yc, zc)


def swiglu(h, w1, w3, w2):
    return (jax.nn.silu(h @ w1) * (h @ w3)) @ w2


def moe_swiglu(h, w_router, w1, w3, w2):
    b, t, d = h.shape
    hf = h.reshape(b * t, d)
    logits = (hf @ w_router).astype(jnp.float32)
    top_v, top_i = lax.top_k(logits, TOP_K)
    top_w = jax.nn.softmax(top_v, axis=-1)
    gates = jnp.sum(jax.nn.one_hot(top_i, N_EXPERTS, dtype=jnp.float32) * top_w[..., None], axis=1)
    out = jnp.zeros_like(hf)
    for e in range(N_EXPERTS):
        out = out + gates[:, e:e + 1].astype(hf.dtype) * swiglu(hf, w1[e], w3[e], w2[e])
    return out.reshape(b, t, d)


def even_layer(x, xc, c, c_ctx, ada_w, ada_b, norm1, norm2, w_in, q_gain, k_gain, dw_w, dw_b, ln_g, ln_b,
               w_o, ff_w1, ff_w3, ff_w2, need_ctx):
    sh1, sc1, g1, sh2, sc2, g2 = adaln(c, ada_w, ada_b)
    csh1, csc1, cg1, csh2, csc2, cg2 = adaln(c_ctx, ada_w, ada_b)
    h = modulate(rms_norm(x, norm1), sh1, sc1)
    hc = modulate(rms_norm(xc, norm1), csh1, csc1)
    mix, mix_c = even_mixer(h, hc, w_in, q_gain, k_gain, dw_w, dw_b, ln_g, ln_b, w_o, need_ctx)
    x = x + g1 * mix
    x = x + g2 * swiglu(modulate(rms_norm(x, norm2), sh2, sc2), ff_w1, ff_w3, ff_w2)
    if not need_ctx:
        return x, None
    xc = xc + cg1 * mix_c
    xc = xc + cg2 * swiglu(modulate(rms_norm(xc, norm2), csh2, csc2), ff_w1, ff_w3, ff_w2)
    return x, xc


def odd_layer(x, xc, c, c_ctx, ada_w, ada_b, norm1, norm2, w_in, conv_w, conv_b, a_log_f, a_log_b,
              dt_bias_f, dt_bias_b, d_skip, gnorm, w_out, router, ex_w1, ex_w3, ex_w2, need_ctx):
    sh1, sc1, g1, sh2, sc2, g2 = adaln(c, ada_w, ada_b)
    csh1, csc1, cg1, csh2, csc2, cg2 = adaln(c_ctx, ada_w, ada_b)
    h = modulate(rms_norm(x, norm1), sh1, sc1)
    hc = modulate(rms_norm(xc, norm1), csh1, csc1)
    mix, mix_c = odd_mixer(h, hc, w_in, conv_w, conv_b, a_log_f, a_log_b, dt_bias_f, dt_bias_b, d_skip,
                           gnorm, w_out, need_ctx)
    x = x + g1 * mix
    x = x + g2 * moe_swiglu(modulate(rms_norm(x, norm2), sh2, sc2), router, ex_w1, ex_w3, ex_w2)
    if not need_ctx:
        return x, None
    xc = xc + cg1 * mix_c
    xc = xc + cg2 * moe_swiglu(modulate(rms_norm(xc, norm2), csh2, csc2), router, ex_w1, ex_w3, ex_w2)
    return x, xc


def setup_inputs(seed: int = 0) -> dict:
    key = jax.random.key(seed)
    ks = iter(jax.random.split(key, 64))

    def nrm(shape, scale):
        return jax.random.normal(next(ks), shape, jnp.float32) * scale

    def gain(shape):
        return 1.0 + nrm(shape, 0.02)

    def dt_bias(shape):
        dt = jnp.exp(jax.random.uniform(next(ks), shape, jnp.float32,
                                        minval=float(np.log(1e-3)), maxval=float(np.log(1e-1))))
        return dt + jnp.log(-jnp.expm1(-dt))

    def a_log(shape):
        return jnp.log(jax.random.uniform(next(ks), shape, jnp.float32, minval=1.0, maxval=16.0))

    ne, no = N_EVEN, N_ODD
    d = D_MODEL
    return {
        'x': nrm((BATCH, SEQ, d), 1.0),
        'c': nrm((BATCH, d), 1.0),
        'ctx': nrm((BATCH, CTX_LEN, d), 1.0),
        'c_ctx': nrm((d,), 1.0),
        'ev_ada_w': nrm((ne, d, 6 * d), 0.02),
        'ev_ada_b': nrm((ne, 6 * d), 0.02),
        'ev_norm1': gain((ne, d)),
        'ev_norm2': gain((ne, d)),
        'ev_w_in': nrm((ne, d, IN_EVEN), d ** -0.5),
        'ev_q_gain': gain((ne, HEAD_DIM)),
        'ev_k_gain': gain((ne, HEAD_DIM)),
        'ev_dw_w': nrm((ne, CONV_WIDTH, CONV_CH), CONV_WIDTH ** -0.5),
        'ev_dw_b': nrm((ne, CONV_CH), 0.02),
        'ev_ln_g': gain((ne, CONV_CH)),
        'ev_ln_b': nrm((ne, CONV_CH), 0.02),
        'ev_w_o': nrm((ne, MIX_EVEN, d), MIX_EVEN ** -0.5),
        'ev_ff_w1': nrm((ne, d, D_FF), d ** -0.5),
        'ev_ff_w3': nrm((ne, d, D_FF), d ** -0.5),
        'ev_ff_w2': nrm((ne, D_FF, d), D_FF ** -0.5),
        'od_ada_w': nrm((no, d, 6 * d), 0.02),
        'od_ada_b': nrm((no, 6 * d), 0.02),
        'od_norm1': gain((no, d)),
        'od_norm2': gain((no, d)),
        'od_w_in': nrm((no, d, IN_ODD), d ** -0.5),
        'od_conv_w': nrm((no, SSM_CONV, CONV_DIM), SSM_CONV ** -0.5),
        'od_conv_b': nrm((no, CONV_DIM), 0.02),
        'od_a_log_f': a_log((no, SSM_HEADS)),
        'od_a_log_b': a_log((no, SSM_HEADS)),
        'od_dt_bias_f': dt_bias((no, SSM_HEADS)),
        'od_dt_bias_b': dt_bias((no, SSM_HEADS)),
        'od_d_skip': gain((no, SSM_HEADS)),
        'od_gnorm': gain((no, D_INNER)),
        'od_w_out': nrm((no, D_INNER, d), D_INNER ** -0.5),
        'od_router': nrm((no, d, N_EXPERTS), d ** -0.5),
        'od_ex_w1': nrm((no, N_EXPERTS, d, D_FF_EXPERT), d ** -0.5),
        'od_ex_w3': nrm((no, N_EXPERTS, d, D_FF_EXPERT), d ** -0.5),
        'od_ex_w2': nrm((no, N_EXPERTS, D_FF_EXPERT, d), D_FF_EXPERT ** -0.5),
        'final_norm': gain((d,)),
    }


def reference(x, c, ctx, c_ctx,
              ev_ada_w, ev_ada_b, ev_norm1, ev_norm2, ev_w_in, ev_q_gain, ev_k_gain, ev_dw_w, ev_dw_b,
              ev_ln_g, ev_ln_b, ev_w_o, ev_ff_w1, ev_ff_w3, ev_ff_w2,
              od_ada_w, od_ada_b, od_norm1, od_norm2, od_w_in, od_conv_w, od_conv_b, od_a_log_f, od_a_log_b,
              od_dt_bias_f, od_dt_bias_b, od_d_skip, od_gnorm, od_w_out, od_router, od_ex_w1, od_ex_w3, od_ex_w2,
              final_norm):
    xc = ctx
    for i in range(DEPTH):
        need_ctx = i < DEPTH - 1
        j = i // 2
        if i % 2 == 0:
            x, xc = even_layer(x, xc, c, c_ctx, ev_ada_w[j], ev_ada_b[j], ev_norm1[j], ev_norm2[j], ev_w_in[j],
                               ev_q_gain[j], ev_k_gain[j], ev_dw_w[j], ev_dw_b[j], ev_ln_g[j], ev_ln_b[j],
                               ev_w_o[j], ev_ff_w1[j], ev_ff_w3[j], ev_ff_w2[j], need_ctx)
        else:
            x, xc = odd_layer(x, xc, c, c_ctx, od_ada_w[j], od_ada_b[j], od_norm1[j], od_norm2[j], od_w_in[j],
                              od_conv_w[j], od_conv_b[j], od_a_log_f[j], od_a_log_b[j], od_dt_bias_f[j],
                              od_dt_bias_b[j], od_d_skip[j], od_gnorm[j], od_w_out[j], od_router[j],
                              od_ex_w1[j], od_ex_w3[j], od_ex_w2[j], need_ctx)
    return rms_norm(x, final_norm)
```

```python
import functools

import jax
import jax.numpy as jnp
from jax import lax
from jax.experimental import pallas as pl
from jax.experimental.pallas import tpu as pltpu

F32 = jnp.float32
BF16 = jnp.bfloat16
EPS = 1e-6

GRID_W = 64
HEAD_DIM = 64
ATTN_HEADS = 8
KV_HEADS = 2
ATTN_W = ATTN_HEADS * HEAD_DIM
KV_W = KV_HEADS * HEAD_DIM
ROPE_THETA = 10000.0
SSM_HEADDIM = 64
SSM_GROUPS = 4
D_STATE = 128
CHUNK = 128
N_EXPERTS = 8
TOP_K = 2

LANES = 128
SUBLANES = 8
VMEM_LIMIT = 56 * 1024 * 1024

TM = 256
TM_E = 512
TF_E = 896
TM_C = 256


def _params(*sem):
    return pltpu.CompilerParams(dimension_semantics=sem, vmem_limit_bytes=VMEM_LIMIT)


def _resident(shape):
    nd = len(shape)
    return pl.BlockSpec(shape, lambda *_: (0,) * nd, pipeline_mode=pl.Buffered(1))


def _silu(x):
    return x * jax.nn.sigmoid(x)


def _rms(x, g):
    return x * lax.rsqrt(jnp.mean(x * x, axis=-1, keepdims=True) + EPS) * g


def _dot(a, b):
    return jnp.dot(a, b, preferred_element_type=F32)


def _split3(a):
    a1 = a.astype(BF16)
    r1 = a - a1.astype(F32)
    a2 = r1.astype(BF16)
    a3 = (r1 - a2.astype(F32)).astype(BF16)
    return a1, a2, a3


def _ada_body(c_ref, w_ref, b_ref, o_ref):
    s = _silu(c_ref[...]).astype(BF16)
    o_ref[...] = _dot(s, w_ref[...].astype(BF16)) + b_ref[...]


def _adaln(cond, w, b):
    r, d = cond.shape
    n = w.shape[1]
    tn = n // 4
    return pl.pallas_call(
        _ada_body,
        grid=(n // tn,),
        in_specs=[pl.BlockSpec((r, d), lambda j: (0, 0)),
                  pl.BlockSpec((d, tn), lambda j: (0, j)),
                  pl.BlockSpec((1, tn), lambda j: (0, j))],
        out_specs=pl.BlockSpec((r, tn), lambda j: (0, j)),
        out_shape=jax.ShapeDtypeStruct((r, n), F32),
        compiler_params=_params("arbitrary"),
        name="adaln",
    )(cond, w, b.reshape(1, n))


def _mod_table(c, c_ctx, w, b):
    bsz, d = c.shape
    rows = -(-(bsz + 1) // SUBLANES) * SUBLANES
    cond = jnp.zeros((rows, d), F32).at[:bsz].set(c).at[bsz].set(c_ctx)
    m = _adaln(cond, w, b)
    lat = m[:bsz]
    ctx = jnp.broadcast_to(m[bsz][None], lat.shape)
    return jnp.stack([ctx, lat], axis=1).reshape(bsz, 2, 6, d)


def _inproj_body(x_ref, mod_ref, n_ref, w_ref, *out_refs, splits):
    h = _rms(x_ref[0], n_ref[...]) * (1.0 + mod_ref[0, 0, 1:2, :]) + mod_ref[0, 0, 0:1, :]
    r = _dot(h.astype(BF16), w_ref[...])
    for o_ref, (lo, hi) in zip(out_refs, splits):
        o_ref[0] = r[:, lo:hi].astype(o_ref.dtype)


def _inproj(x, mod, norm, w_bf16, splits, n_ctx_tiles):
    bsz, t, d = x.shape
    n = w_bf16.shape[1]
    widths = [hi - lo for lo, hi in splits]
    return pl.pallas_call(
        functools.partial(_inproj_body, splits=tuple(splits)),
        grid=(bsz, t // TM),
        in_specs=[pl.BlockSpec((1, TM, d), lambda b, i: (b, i, 0)),
                  pl.BlockSpec((1, 1, 6, d), lambda b, i: (b, (i >= n_ctx_tiles).astype(jnp.int32), 0, 0)),
                  _resident((1, d)),
                  _resident((d, n))],
        out_specs=[pl.BlockSpec((1, TM, wd), lambda b, i: (b, i, 0)) for wd in widths],
        out_shape=[jax.ShapeDtypeStruct((bsz, t, wd), F32) for wd in widths],
        compiler_params=_params("arbitrary", "arbitrary"),
        name="inproj",
    )(x, mod, norm.reshape(1, d), w_bf16)


def _attn_body(q_ref, k_ref, v_ref, cq_ref, sq_ref, ck_ref, sk_ref, qg_ref, kg_ref, o_ref, kt_s, v2_s,
               *, n_ctx_tiles, ctx_len):
    i = pl.program_id(1)
    t = k_ref.shape[1]
    lane = lax.broadcasted_iota(jnp.int32, (1, LANES), 1)
    low = lane < HEAD_DIM
    even = (lane % 2) == 0

    def norm_rope(x, gain, cos, sin):
        x2 = x * x
        s_lo = jnp.sum(jnp.where(low, x2, 0.0), axis=-1, keepdims=True)
        s_hi = jnp.sum(jnp.where(low, 0.0, x2), axis=-1, keepdims=True)
        ms = jnp.where(low, s_lo, s_hi) * (1.0 / HEAD_DIM)
        xn = x * lax.rsqrt(ms + EPS) * gain
        swapped = jnp.where(even, pltpu.roll(xn, LANES - 1, 1), pltpu.roll(xn, 1, 1))
        return xn * cos + swapped * sin

    @pl.when(i == 0)
    def _():
        k = norm_rope(k_ref[0], kg_ref[...], ck_ref[...], sk_ref[...])
        kr = pltpu.roll(k, HEAD_DIM, 1)
        kt_s[0] = jnp.where(low, k, kr).T.astype(BF16)
        kt_s[1] = jnp.where(low, kr, k).T.astype(BF16)
        v = v_ref[0]
        vr = pltpu.roll(v, HEAD_DIM, 1)
        v2_s[0] = jnp.where(low, v, vr).astype(BF16)
        v2_s[1] = jnp.where(low, vr, v).astype(BF16)

    def run(tk):
        for j in range(ATTN_HEADS // 2):
            g = (2 * j) // (ATTN_HEADS // KV_HEADS)
            qp = norm_rope(q_ref[0, :, j * LANES:(j + 1) * LANES], qg_ref[...], cq_ref[...], sq_ref[...])
            qp = qp * (HEAD_DIM ** -0.5)
            outs = []
            for hh in range(2):
                qm = jnp.where(low if hh == 0 else jnp.logical_not(low), qp, 0.0).astype(BF16)
                s = _dot(qm, kt_s[g, :, 0:tk])
                m = jnp.max(s, axis=-1, keepdims=True)
                p = jnp.exp(s - m)
                l = jnp.sum(p, axis=-1, keepdims=True)
                o = _dot(p.astype(BF16), v2_s[g, 0:tk, :])
                outs.append(o / l)
            o_ref[0, :, j * LANES:(j + 1) * LANES] = jnp.where(low, outs[0], outs[1]).astype(o_ref.dtype)

    @pl.when(i < n_ctx_tiles)
    def _():
        run(ctx_len)

    @pl.when(i >= n_ctx_tiles)
    def _():
        run(t)


def _attention(q, k, v, cos2, sin2, q_gain, k_gain, ctx_len):
    bsz, t, _ = q.shape
    n_ctx_tiles = ctx_len // TM
    qg = jnp.tile(q_gain.reshape(1, HEAD_DIM), (1, 2))
    kg = jnp.tile(k_gain.reshape(1, HEAD_DIM), (1, 2))
    return pl.pallas_call(
        functools.partial(_attn_body, n_ctx_tiles=n_ctx_tiles, ctx_len=ctx_len),
        grid=(bsz, t // TM),
        in_specs=[pl.BlockSpec((1, TM, ATTN_W), lambda b, i: (b, i, 0)),
                  pl.BlockSpec((1, t, KV_W), lambda b, i: (b, 0, 0)),
                  pl.BlockSpec((1, t, KV_W), lambda b, i: (b, 0, 0)),
                  pl.BlockSpec((TM, LANES), lambda b, i: (i, 0)),
                  pl.BlockSpec((TM, LANES), lambda b, i: (i, 0)),
                  _resident((t, LANES)),
                  _resident((t, LANES)),
                  _resident((1, LANES)),
                  _resident((1, LANES))],
        out_specs=pl.BlockSpec((1, TM, ATTN_W), lambda b, i: (b, i, 0)),
        out_shape=jax.ShapeDtypeStruct((bsz, t, ATTN_W), BF16),
        scratch_shapes=[pltpu.VMEM((KV_HEADS, LANES, t), BF16), pltpu.VMEM((KV_HEADS, t, LANES), BF16)],
        compiler_params=_params("arbitrary", "arbitrary"),
        name="attention",
    )(q, k, v, cos2, sin2, cos2, sin2, qg, kg)


def _rope_tables(ctx_len, seq_len):
    rows = seq_len // GRID_W
    t_row = jnp.repeat(jnp.arange(rows, dtype=F32), GRID_W)
    t_col = jnp.tile(jnp.arange(GRID_W, dtype=F32), rows)
    axis_dim = HEAD_DIM // 2
    inv_freq = ROPE_THETA ** (-jnp.arange(0, axis_dim, 2, dtype=F32) / axis_dim)
    ang = jnp.concatenate([t_row[:, None] * inv_freq, t_col[:, None] * inv_freq], axis=-1)
    cos = jnp.repeat(jnp.cos(ang), 2, axis=-1)
    sin = jnp.repeat(jnp.sin(ang), 2, axis=-1) * jnp.tile(jnp.array([-1.0, 1.0], F32), axis_dim)
    cos = jnp.concatenate([jnp.ones((ctx_len, HEAD_DIM), F32), cos], axis=0)
    sin = jnp.concatenate([jnp.zeros((ctx_len, HEAD_DIM), F32), sin], axis=0)
    return jnp.tile(cos, (1, 2)), jnp.tile(sin, (1, 2))


def _fill_padded(src_ref, pad_s, width, ctx_len, seq_len, pad, rows, fn):
    t = ctx_len + seq_len
    z = jnp.zeros((pad + SUBLANES, width), F32)
    pad_s[0:pad, :] = z[0:pad]
    pad_s[pad + ctx_len:2 * pad + ctx_len, :] = z[0:pad]
    pad_s[2 * pad + t:3 * pad + t + SUBLANES, :] = z

    def seg(tok0, off, ntiles):
        def body(n, carry):
            r = pl.multiple_of(tok0 + n * rows, rows)
            pad_s[pl.ds(r + off, rows), :] = fn(src_ref, r, rows)
            return carry
        lax.fori_loop(0, ntiles, body, 0)

    seg(0, pad, ctx_len // rows)
    seg(ctx_len, 2 * pad, seq_len // rows)


def _conv_segments(pad_s, w_ref, taps, ctx_len, seq_len, pad, rows, emit):
    half = taps // 2
    span = rows + SUBLANES

    def seg(tok0, off, ntiles):
        def body(n, carry):
            r = pl.multiple_of(tok0 + n * rows, rows)
            base = r + off - pad
            acc = None
            for rho in range(SUBLANES):
                part = None
                for j in range(taps):
                    dj = pad - half + j
                    if dj % SUBLANES != rho:
                        continue
                    term = w_ref[j:j + 1, :] * pad_s[pl.ds(pl.multiple_of(base + (dj - rho), SUBLANES), span), :]
                    part = term if part is None else part + term
                if part is None:
                    continue
                if rho:
                    part = pltpu.roll(part, span - rho, 0)
                acc = part[0:rows] if acc is None else acc + part[0:rows]
            emit(r, acc)
            return carry
        lax.fori_loop(0, ntiles, body, 0)

    seg(0, pad, ctx_len // rows)
    seg(ctx_len, 2 * pad, seq_len // rows)


CC_PAD = 16
CC_ROWS = 32


def _cconv_body(u_ref, w_ref, b_ref, g_ref, bb_ref, o_ref, pad_s, *, ctx_len, seq_len):
    c = o_ref.shape[2]

    def glu(src_ref, r, rows):
        return src_ref[0, pl.ds(r, rows), 0:c] * jax.nn.sigmoid(src_ref[0, pl.ds(r, rows), c:2 * c])

    _fill_padded(u_ref, pad_s, c, ctx_len, seq_len, CC_PAD, TM, glu)

    def emit(r, acc):
        h = acc + b_ref[...]
        mu = jnp.mean(h, axis=-1, keepdims=True)
        hc = h - mu
        var = jnp.mean(hc * hc, axis=-1, keepdims=True)
        y = hc * lax.rsqrt(var + EPS) * g_ref[...] + bb_ref[...]
        o_ref[0, pl.ds(r, CC_ROWS), :] = _silu(y).astype(o_ref.dtype)

    _conv_segments(pad_s, w_ref, w_ref.shape[0], ctx_len, seq_len, CC_PAD, CC_ROWS, emit)


def _conformer_conv(u, dw_w, dw_b, ln_g, ln_b, ctx_len):
    bsz, t, c2 = u.shape
    c = c2 // 2
    taps = dw_w.shape[0]
    return pl.pallas_call(
        functools.partial(_cconv_body, ctx_len=ctx_len, seq_len=t - ctx_len),
        grid=(bsz,),
        in_specs=[pl.BlockSpec((1, t, c2), lambda b: (b, 0, 0)),
                  _resident((taps, c)), _resident((1, c)), _resident((1, c)), _resident((1, c))],
        out_specs=pl.BlockSpec((1, t, c), lambda b: (b, 0, 0)),
        out_shape=jax.ShapeDtypeStruct((bsz, t, c), BF16),
        scratch_shapes=[pltpu.VMEM((t + 3 * CC_PAD + SUBLANES, c), F32)],
        compiler_params=_params("arbitrary"),
        name="conformer_conv",
    )(u, dw_w, dw_b.reshape(1, c), ln_g.reshape(1, c), ln_b.reshape(1, c))


SC_PAD = 8
SC_ROWS = 64
SC_COLS = 512


def _sconv_body(x_ref, w_ref, b_ref, o_ref, pad_s, *, ctx_len, seq_len):
    def ident(src_ref, r, rows):
        return src_ref[0, pl.ds(r, rows), :]

    _fill_padded(x_ref, pad_s, SC_COLS, ctx_len, seq_len, SC_PAD, TM, ident)

    def emit(r, acc):
        o_ref[0, pl.ds(r, SC_ROWS), :] = _silu(acc + b_ref[...])

    _conv_segments(pad_s, w_ref, w_ref.shape[0], ctx_len, seq_len, SC_PAD, SC_ROWS, emit)


def _ssm_conv(xbc, conv_w, conv_b, ctx_len):
    bsz, t, c = xbc.shape
    taps = conv_w.shape[0]
    return pl.pallas_call(
        functools.partial(_sconv_body, ctx_len=ctx_len, seq_len=t - ctx_len),
        grid=(bsz, c // SC_COLS),
        in_specs=[pl.BlockSpec((1, t, SC_COLS), lambda b, j: (b, 0, j)),
                  pl.BlockSpec((taps, SC_COLS), lambda b, j: (0, j)),
                  pl.BlockSpec((1, SC_COLS), lambda b, j: (0, j))],
        out_specs=pl.BlockSpec((1, t, SC_COLS), lambda b, j: (b, 0, j)),
        out_shape=jax.ShapeDtypeStruct((bsz, t, c), F32),
        scratch_shapes=[pltpu.VMEM((t + 3 * SC_PAD + SUBLANES, SC_COLS), F32)],
        compiler_params=_params("arbitrary", "arbitrary"),
        name="ssm_conv",
    )(xbc, conv_w, conv_b.reshape(1, c))


def _even_ffn_body(x_ref, a_ref, c_ref, mod_ref, n2_ref, wo_ref, w1_ref, w3_ref, w2_ref, o_ref):
    ca = a_ref.shape[2]
    mix = _dot(a_ref[0], wo_ref[0:ca, :]) + _dot(c_ref[0], wo_ref[ca:, :])
    x1 = x_ref[0] + mod_ref[0, 0, 2:3, :] * mix
    h = _rms(x1, n2_ref[...]) * (1.0 + mod_ref[0, 0, 4:5, :]) + mod_ref[0, 0, 3:4, :]
    hb = h.astype(BF16)
    u = _silu(_dot(hb, w1_ref[...])) * _dot(hb, w3_ref[...])
    o_ref[0] = x1 + mod_ref[0, 0, 5:6, :] * _dot(u.astype(BF16), w2_ref[...])


def _even_ffn(x, attn, conv, mod, norm2, wo, w1, w3, w2, n_ctx_tiles):
    bsz, t, d = x.shape
    ca, cc = attn.shape[2], conv.shape[2]
    return pl.pallas_call(
        _even_ffn_body,
        grid=(bsz, t // TM),
        in_specs=[pl.BlockSpec((1, TM, d), lambda b, i: (b, i, 0)),
                  pl.BlockSpec((1, TM, ca), lambda b, i: (b, i, 0)),
                  pl.BlockSpec((1, TM, cc), lambda b, i: (b, i, 0)),
                  pl.BlockSpec((1, 1, 6, d), lambda b, i: (b, (i >= n_ctx_tiles).astype(jnp.int32), 0, 0)),
                  _resident((1, d)), _resident(wo.shape), _resident(w1.shape), _resident(w3.shape),
                  _resident(w2.shape)],
        out_specs=pl.BlockSpec((1, TM, d), lambda b, i: (b, i, 0)),
        out_shape=jax.ShapeDtypeStruct((bsz, t, d), F32),
        compiler_params=_params("arbitrary", "arbitrary"),
        name="even_ffn",
    )(x, attn, conv, mod, norm2.reshape(1, d), wo, w1, w3, w2)


def _softplus(x):
    return jnp.maximum(x, 0.0) + jnp.log(1.0 + jnp.exp(-jnp.abs(x)))


def _ssd_body(xs_ref, bm_ref, cm_ref, dt_ref, dtt_ref, alr_ref, alc_ref, bir_ref, bic_ref, *rest,
              rev, n_ctx_chunks, combine):
    if combine:
        yb_ref, dsk_ref, y_ref, st_s = rest
    else:
        y_ref, st_s = rest
    c = pl.program_id(1)
    nh = alr_ref.shape[1]
    hpg = nh // SSM_GROUPS
    gw = hpg * SSM_HEADDIM
    end = 0 if rev else CHUNK - 1

    @pl.when(c == 0)
    def _():
        st_s[...] = jnp.zeros_like(st_s)

    off = nh if rev else 0
    dt = _softplus(dt_ref[0, :, off:off + nh] + bir_ref[...])
    dtt = _softplus(dtt_ref[0, off:off + nh, :] + bic_ref[...])
    a = dt * (-jnp.exp(alr_ref[...]))
    at = dtt * (-jnp.exp(alc_ref[...]))

    row = lax.broadcasted_iota(jnp.int32, (CHUNK, CHUNK), 0)
    col = lax.broadcasted_iota(jnp.int32, (CHUNK, CHUNK), 1)
    tri = (col >= row) if rev else (col <= row)
    trit = (row >= col) if rev else (row <= col)
    tri_b = tri.astype(BF16)
    trit_b = trit.astype(BF16)
    acs = sum(_dot(tri_b, p) for p in _split3(a))
    acst = sum(_dot(p, trit_b) for p in _split3(at))
    tot = acs[end:end + 1, :]
    w_end = dt * jnp.exp(tot - acs)
    eacs = jnp.exp(acs)
    cdec = jnp.exp(tot)

    lane = lax.broadcasted_iota(jnp.int32, (1, LANES), 1)
    low = lane < SSM_HEADDIM

    def pair(v, e0):
        return jnp.where(low, v[:, e0:e0 + 1], v[:, e0 + 1:e0 + 2])

    is_lat = c >= n_ctx_chunks

    for g in range(SSM_GROUPS):
        bm = bm_ref[0, :, g * D_STATE:(g + 1) * D_STATE]
        cm = cm_ref[0, :, g * D_STATE:(g + 1) * D_STATE]
        bmt = bm.T.astype(BF16)
        cmb = cm.astype(BF16)
        sg = st_s[g]

        xcd = []
        for kp in range(hpg // 2):
            e0 = g * hpg + 2 * kp
            xs_p = xs_ref[0, :, e0 * SSM_HEADDIM:e0 * SSM_HEADDIM + LANES]
            xcd.append((xs_p * pair(w_end, e0)).astype(BF16))
        new_state = _dot(bmt, jnp.concatenate(xcd, axis=1))
        dec = jnp.concatenate([pair(cdec, g * hpg + 2 * kp) for kp in range(hpg // 2)], axis=1)
        st_s[g] = sg * dec + new_state

        @pl.when(is_lat)
        def _():
            cb = _dot(cmb, bmt)
            yoff = _dot(cmb, sg.astype(BF16))
            for kp in range(hpg // 2):
                e0 = g * hpg + 2 * kp
                c0 = e0 * SSM_HEADDIM
                xs_p = xs_ref[0, :, c0:c0 + LANES]
                xcb = (xs_p * pair(dt, e0)).astype(BF16)
                res = []
                for e in (e0, e0 + 1):
                    diff = acs[:, e:e + 1] - acst[e:e + 1, :]
                    m = (jnp.exp(jnp.where(tri, diff, -jnp.inf)) * cb).astype(BF16)
                    res.append(_dot(m, xcb))
                y = jnp.where(low, res[0], res[1]) + yoff[:, kp * LANES:(kp + 1) * LANES] * pair(eacs, e0)
                if combine:
                    y = y + yb_ref[0, :, c0:c0 + LANES] + pair(dsk_ref[...], e0) * xs_p
                y_ref[0, :, c0:c0 + LANES] = y


def _ssd_scan(xbc, dt, dtt, a_log, dt_bias, ctx_len, rev, y_other=None, d_skip=None):
    bsz, t, _ = xbc.shape
    nh = a_log.shape[0]
    d_inner = nh * SSM_HEADDIM
    gn = SSM_GROUPS * D_STATE
    nc = t // CHUNK
    ncc = ctx_len // CHUNK
    seq_len = t - ctx_len
    combine = y_other is not None

    if rev:
        def chunk(i):
            return jnp.where(i < ncc, ncc - 1 - i, nc - 1 + ncc - i)
    else:
        def chunk(i):
            return i

    def ychunk(i):
        return jnp.maximum(chunk(i), ncc) - ncc if not rev else jnp.where(i < ncc, nc - 1 - ncc, chunk(i) - ncc)

    bcol = d_inner // gn
    in_specs = [pl.BlockSpec((1, CHUNK, d_inner), lambda b, i: (b, chunk(i), 0)),
                pl.BlockSpec((1, CHUNK, gn), lambda b, i: (b, chunk(i), bcol)),
                pl.BlockSpec((1, CHUNK, gn), lambda b, i: (b, chunk(i), bcol + 1)),
                pl.BlockSpec((1, CHUNK, 2 * nh), lambda b, i: (b, chunk(i), 0)),
                pl.BlockSpec((1, 2 * nh, CHUNK), lambda b, i: (b, 0, chunk(i))),
                _resident((1, nh)), _resident((nh, 1)), _resident((1, nh)), _resident((nh, 1))]
    args = [xbc, xbc, xbc, dt, dtt, a_log.reshape(1, nh), a_log.reshape(nh, 1),
            dt_bias.reshape(1, nh), dt_bias.reshape(nh, 1)]
    if combine:
        in_specs += [pl.BlockSpec((1, CHUNK, d_inner), lambda b, i: (b, ychunk(i), 0)), _resident((1, nh))]
        args += [y_other, d_skip.reshape(1, nh)]
    return pl.pallas_call(
        functools.partial(_ssd_body, rev=rev, n_ctx_chunks=ncc, combine=combine),
        grid=(bsz, nc),
        in_specs=in_specs,
        out_specs=pl.BlockSpec((1, CHUNK, d_inner), lambda b, i: (b, ychunk(i), 0)),
        out_shape=jax.ShapeDtypeStruct((bsz, seq_len, d_inner), F32),
        scratch_shapes=[pltpu.VMEM((SSM_GROUPS, D_STATE, d_inner // SSM_GROUPS), F32)],
        compiler_params=_params("arbitrary", "arbitrary"),
        name="ssd_bwd" if rev else "ssd_fwd",
    )(*args)


def _odd_out_body(x_ref, y_ref, z_ref, mod_ref, gn_ref, n2_ref, wout_ref, wr_ref, x1_ref, h_ref, ti_ref, tw_ref):
    z = z_ref[0]
    yn = _rms(y_ref[0] * _silu(z), gn_ref[...])
    x1 = x_ref[0] + mod_ref[0, 0, 2:3, :] * _dot(yn.astype(BF16), wout_ref[...])
    x1_ref[0] = x1
    h = _rms(x1, n2_ref[...]) * (1.0 + mod_ref[0, 0, 4:5, :]) + mod_ref[0, 0, 3:4, :]
    h_ref[0] = h
    ne = wr_ref.shape[1]
    h1, h2, _ = _split3(h)
    w1, w2, _ = _split3(wr_ref[...])
    hw = _dot(h1, jnp.concatenate([w1, w2], axis=1))
    lg = hw[:, 0:ne] + hw[:, ne:2 * ne] + _dot(h2, w1)
    idx = lax.broadcasted_iota(jnp.int32, lg.shape, 1)
    m1 = jnp.max(lg, axis=-1, keepdims=True)
    i1 = jnp.min(jnp.where(lg == m1, idx, ne), axis=-1, keepdims=True)
    lg2 = jnp.where(idx == i1, -jnp.inf, lg)
    m2 = jnp.max(lg2, axis=-1, keepdims=True)
    i2 = jnp.min(jnp.where(lg2 == m2, idx, ne), axis=-1, keepdims=True)
    e2 = jnp.exp(m2 - m1)
    den = 1.0 + e2
    ti_ref[0] = jnp.concatenate([i1, i2], axis=1)
    tw_ref[0] = jnp.concatenate([1.0 / den, e2 / den], axis=1)


def _odd_out(x, y, z, mod, gnorm, norm2, w_out, w_router, ctx_len):
    bsz, t, d = x.shape
    seq_len = t - ctx_len
    di = y.shape[2]
    ne = w_router.shape[1]
    nct = ctx_len // TM
    lat = lambda b, i: (b, i + nct, 0)
    own = lambda b, i: (b, i, 0)
    return pl.pallas_call(
        _odd_out_body,
        grid=(bsz, seq_len // TM),
        in_specs=[pl.BlockSpec((1, TM, d), lat),
                  pl.BlockSpec((1, TM, di), own),
                  pl.BlockSpec((1, TM, di), lat),
                  pl.BlockSpec((1, 1, 6, d), lambda b, i: (b, 1, 0, 0)),
                  _resident((1, di)), _resident((1, d)), _resident(w_out.shape), _resident((d, ne))],
        out_specs=[pl.BlockSpec((1, TM, d), own), pl.BlockSpec((1, TM, d), own),
                   pl.BlockSpec((1, TM, TOP_K), own), pl.BlockSpec((1, TM, TOP_K), own)],
        out_shape=[jax.ShapeDtypeStruct((bsz, seq_len, d), F32), jax.ShapeDtypeStruct((bsz, seq_len, d), F32),
                   jax.ShapeDtypeStruct((bsz, seq_len, TOP_K), jnp.int32),
                   jax.ShapeDtypeStruct((bsz, seq_len, TOP_K), F32)],
        compiler_params=_params("arbitrary", "arbitrary"),
        name="odd_out",
    )(x, y, z, mod, gnorm.reshape(1, di), norm2.reshape(1, d), w_out, w_router)


def _gather_rows(idx_ref, src_hbm, dst, sem, n):
    def body(j, carry):
        pltpu.make_async_copy(src_hbm.at[pl.ds(idx_ref[0, 0, j], 1), :], dst.at[pl.ds(j, 1), :], sem).start()
        return carry
    lax.fori_loop(0, n, body, 0)
    pltpu.make_async_copy(src_hbm.at[pl.ds(0, n), :], dst, sem).wait()


def _expert_body(te_ref, na_ref, rows_ref, h_hbm, w1_ref, w3_ref, w2_ref, y_ref, xf_s, xb_s, acc_s, sem):
    ti = pl.program_id(0)
    f = pl.program_id(1)
    active = ti < na_ref[0]

    @pl.when(jnp.logical_and(active, f == 0))
    def _():
        _gather_rows(rows_ref, h_hbm, xf_s, sem, TM_E)
        xb_s[...] = xf_s[...].astype(BF16)
        acc_s[...] = jnp.zeros_like(acc_s)

    @pl.when(active)
    def _():
        xb = xb_s[...]
        u = _silu(_dot(xb, w1_ref[0])) * _dot(xb, w3_ref[0])
        acc_s[...] += _dot(u.astype(BF16), w2_ref[0])

    last = f == pl.num_programs(1) - 1

    @pl.when(jnp.logical_and(active, last))
    def _():
        y_ref[...] = acc_s[...]

    @pl.when(jnp.logical_and(jnp.logical_not(active), last))
    def _():
        y_ref[...] = jnp.zeros_like(y_ref)


def _expert_ffn(h, tile_expert, n_active, row_token, w1, w3, w2):
    n, d = h.shape
    ns = row_token.shape[0]
    nt = ns // TM_E
    dff = w1.shape[2]
    nf = dff // TF_E

    def wcol(ti, f, te, na):
        return (te[ti], 0, jnp.where(ti < na[0], f, nf - 1))

    def wrow(ti, f, te, na):
        return (te[ti], jnp.where(ti < na[0], f, nf - 1), 0)

    grid_spec = pltpu.PrefetchScalarGridSpec(
        num_scalar_prefetch=2,
        grid=(nt, nf),
        in_specs=[pl.BlockSpec((1, 1, TM_E), lambda ti, f, te, na: (ti, 0, 0), memory_space=pltpu.SMEM),
                  pl.BlockSpec(memory_space=pl.ANY),
                  pl.BlockSpec((1, d, TF_E), wcol),
                  pl.BlockSpec((1, d, TF_E), wcol),
                  pl.BlockSpec((1, TF_E, d), wrow)],
        out_specs=pl.BlockSpec((TM_E, d), lambda ti, f, te, na: (ti, 0)),
        scratch_shapes=[pltpu.VMEM((TM_E, d), F32), pltpu.VMEM((TM_E, d), BF16), pltpu.VMEM((TM_E, d), F32),
                        pltpu.SemaphoreType.DMA(())],
    )
    return pl.pallas_call(
        _expert_body,
        grid_spec=grid_spec,
        out_shape=jax.ShapeDtypeStruct((ns, d), F32),
        compiler_params=_params("arbitrary", "arbitrary"),
        name="expert_ffn",
    )(tile_expert, n_active, row_token.reshape(nt, 1, TM_E), h, w1, w3, w2)


def _combine_body(pos_ref, x_ref, tw_ref, g2_ref, fn_ref, y_hbm, o_ref, y_s, sem):
    _gather_rows(pos_ref, y_hbm, y_s, sem, TOP_K * TM_C)
    tw = tw_ref[...]
    moe = tw[:, 0:1] * y_s[0:TM_C, :] + tw[:, 1:2] * y_s[TM_C:2 * TM_C, :]
    o_ref[...] = _rms(x_ref[...] + g2_ref[0] * moe, fn_ref[...])


def _moe_combine(x1, topw, pos, g2, final_norm, y_sorted, tiles_per_batch):
    n, d = x1.shape
    nt = n // TM_C
    pos_t = pos.reshape(nt, TM_C, TOP_K).transpose(0, 2, 1).reshape(nt, 1, TOP_K * TM_C)
    return pl.pallas_call(
        _combine_body,
        grid=(nt,),
        in_specs=[pl.BlockSpec((1, 1, TOP_K * TM_C), lambda i: (i, 0, 0), memory_space=pltpu.SMEM),
                  pl.BlockSpec((TM_C, d), lambda i: (i, 0)),
                  pl.BlockSpec((TM_C, TOP_K), lambda i: (i, 0)),
                  pl.BlockSpec((1, 1, d), lambda i: (i // tiles_per_batch, 0, 0)),
                  _resident((1, d)),
                  pl.BlockSpec(memory_space=pl.ANY)],
        out_specs=pl.BlockSpec((TM_C, d), lambda i: (i, 0)),
        out_shape=jax.ShapeDtypeStruct((n, d), F32),
        scratch_shapes=[pltpu.VMEM((TOP_K * TM_C, d), F32), pltpu.SemaphoreType.DMA(())],
        compiler_params=_params("arbitrary"),
        name="moe_combine",
    )(pos_t, x1, topw, g2, final_norm.reshape(1, d), y_sorted)


def _route(topi, n_slots):
    n = topi.shape[0]
    e_flat = topi.reshape(n * TOP_K)
    oh = (e_flat[:, None] == jnp.arange(N_EXPERTS, dtype=jnp.int32)[None, :]).astype(jnp.int32)
    csum = jnp.cumsum(oh, axis=0)
    rank = jnp.sum(csum * oh, axis=1) - 1
    cnt = csum[-1]
    cnt_pad = (cnt + TM_E - 1) // TM_E * TM_E
    ends = jnp.cumsum(cnt_pad)
    offs = ends - cnt_pad
    slot = jnp.sum(oh * offs[None, :], axis=1) + rank
    n_active = (ends[-1] // TM_E).astype(jnp.int32)
    nt = n_slots // TM_E
    tile_start = jnp.arange(nt, dtype=jnp.int32) * TM_E
    te = jnp.sum((tile_start[:, None] >= ends[None, :]).astype(jnp.int32), axis=1)
    te = jnp.minimum(te, N_EXPERTS - 1)
    te = jnp.where(jnp.arange(nt) < n_active, te, te[jnp.maximum(n_active - 1, 0)])
    row_token = jnp.zeros((n_slots,), jnp.int32).at[slot].set(jnp.arange(n * TOP_K, dtype=jnp.int32) // TOP_K)
    return slot.reshape(n, TOP_K), te.astype(jnp.int32), n_active.reshape(1), row_token


def kernel(x, c, ctx, c_ctx, ev_ada_w, ev_ada_b, ev_norm1, ev_norm2, ev_w_in, ev_q_gain, ev_k_gain, ev_dw_w, ev_dw_b, ev_ln_g, ev_ln_b, ev_w_o, ev_ff_w1, ev_ff_w3, ev_ff_w2, od_ada_w, od_ada_b, od_norm1, od_norm2, od_w_in, od_conv_w, od_conv_b, od_a_log_f, od_a_log_b, od_dt_bias_f, od_dt_bias_b, od_d_skip, od_gnorm, od_w_out, od_router, od_ex_w1, od_ex_w3, od_ex_w2, final_norm):
    bsz, seq_len, d = x.shape
    ctx_len = ctx.shape[1]
    assert ev_ada_w.shape[0] == 1 and od_ada_w.shape[0] == 1, "one even and one odd layer"
    assert ctx_len % TM == 0 and seq_len % TM == 0 and seq_len % GRID_W == 0
    n_ctx_tiles = ctx_len // TM
    xa = jnp.concatenate([ctx, x], axis=1)

    mod = _mod_table(c, c_ctx, ev_ada_w[0], ev_ada_b[0])
    cc = ev_dw_w.shape[2]
    splits = [(0, ATTN_W), (ATTN_W, ATTN_W + KV_W), (ATTN_W + KV_W, ATTN_W + 2 * KV_W),
              (ATTN_W + 2 * KV_W, ATTN_W + 2 * KV_W + 2 * cc)]
    q, k, v, u = _inproj(xa, mod, ev_norm1[0], ev_w_in[0].astype(BF16), splits, n_ctx_tiles)
    cos2, sin2 = _rope_tables(ctx_len, seq_len)
    attn = _attention(q, k, v, cos2, sin2, ev_q_gain[0], ev_k_gain[0], ctx_len)
    conv = _conformer_conv(u, ev_dw_w[0], ev_dw_b[0], ev_ln_g[0], ev_ln_b[0], ctx_len)
    xa = _even_ffn(xa, attn, conv, mod, ev_norm2[0], ev_w_o[0].astype(BF16), ev_ff_w1[0].astype(BF16),
                   ev_ff_w3[0].astype(BF16), ev_ff_w2[0].astype(BF16), n_ctx_tiles)

    mod = _mod_table(c, c_ctx, od_ada_w[0], od_ada_b[0])
    nh = od_a_log_f.shape[1]
    d_inner = nh * SSM_HEADDIM
    conv_dim = od_conv_w.shape[2]
    splits = [(0, d_inner), (d_inner, d_inner + conv_dim), (d_inner + conv_dim, d_inner + conv_dim + 2 * nh)]
    z, xbc, dt = _inproj(xa, mod, od_norm1[0], od_w_in[0].astype(BF16), splits, n_ctx_tiles)
    xbc = _ssm_conv(xbc, od_conv_w[0], od_conv_b[0], ctx_len)
    dtt = dt.transpose(0, 2, 1)
    y_b = _ssd_scan(xbc, dt, dtt, od_a_log_b[0], od_dt_bias_b[0], ctx_len, rev=True)
    y = _ssd_scan(xbc, dt, dtt, od_a_log_f[0], od_dt_bias_f[0], ctx_len, rev=False, y_other=y_b,
                  d_skip=od_d_skip[0])
    x1, h, topi, topw = _odd_out(xa, y, z, mod, od_gnorm[0], od_norm2[0], od_w_out[0].astype(BF16),
                                 od_router[0], ctx_len)

    n = bsz * seq_len
    n_slots = n * TOP_K + N_EXPERTS * TM_E
    pos, tile_expert, n_active, row_token = _route(topi.reshape(n, TOP_K), n_slots)
    y_sorted = _expert_ffn(h.reshape(n, d), tile_expert, n_active, row_token, od_ex_w1[0].astype(BF16),
                           od_ex_w3[0].astype(BF16), od_ex_w2[0].astype(BF16))
    g2 = mod[:, 1, 5:6, :]
    out = _moe_combine(x1.reshape(n, d), topw.reshape(n, TOP_K), pos, g2, final_norm, y_sorted, seq_len // TM_C)
    return out.reshape(bsz, seq_len, d)
```

```python
import functools

import jax
import jax.numpy as jnp
from jax import lax
from jax.experimental import pallas as pl
from jax.experimental.pallas import tpu as pltpu

F32 = jnp.float32
BF16 = jnp.bfloat16
EPS = 1e-6

GRID_W = 64
HEAD_DIM = 64
ATTN_HEADS = 8
KV_HEADS = 2
ATTN_W = ATTN_HEADS * HEAD_DIM
KV_W = KV_HEADS * HEAD_DIM
ROPE_THETA = 10000.0
SSM_HEADDIM = 64
SSM_GROUPS = 4
D_STATE = 128
CHUNK = 128
N_EXPERTS = 8
TOP_K = 2

LANES = 128
SUBLANES = 8
VMEM_LIMIT = 56 * 1024 * 1024

TM = 256
TM_E = 512
TF_E = 896
TM_C = 256


def _params(*sem):
    return pltpu.CompilerParams(dimension_semantics=sem, vmem_limit_bytes=VMEM_LIMIT)


def _resident(shape):
    nd = len(shape)
    return pl.BlockSpec(shape, lambda *_: (0,) * nd, pipeline_mode=pl.Buffered(1))


def _silu(x):
    return x * jax.nn.sigmoid(x)


def _rms(x, g):
    return x * lax.rsqrt(jnp.mean(x * x, axis=-1, keepdims=True) + EPS) * g


def _dot(a, b):
    return jnp.dot(a, b, preferred_element_type=F32)


def _split3(a):
    a1 = a.astype(BF16)
    r1 = a - a1.astype(F32)
    a2 = r1.astype(BF16)
    a3 = (r1 - a2.astype(F32)).astype(BF16)
    return a1, a2, a3


def _ada_body(c_ref, w_ref, b_ref, o_ref):
    s = _silu(c_ref[...]).astype(BF16)
    o_ref[...] = _dot(s, w_ref[...].astype(BF16)) + b_ref[...]


def _adaln(cond, w, b):
    r, d = cond.shape
    n = w.shape[1]
    tn = n // 4
    return pl.pallas_call(
        _ada_body,
        grid=(n // tn,),
        in_specs=[pl.BlockSpec((r, d), lambda j: (0, 0)),
                  pl.BlockSpec((d, tn), lambda j: (0, j)),
                  pl.BlockSpec((1, tn), lambda j: (0, j))],
        out_specs=pl.BlockSpec((r, tn), lambda j: (0, j)),
        out_shape=jax.ShapeDtypeStruct((r, n), F32),
        compiler_params=_params("arbitrary"),
        name="adaln",
    )(cond, w, b.reshape(1, n))


def _mod_table(c, c_ctx, w, b):
    bsz, d = c.shape
    rows = -(-(bsz + 1) // SUBLANES) * SUBLANES
    cond = jnp.zeros((rows, d), F32).at[:bsz].set(c).at[bsz].set(c_ctx)
    m = _adaln(cond, w, b)
    lat = m[:bsz]
    ctx = jnp.broadcast_to(m[bsz][None], lat.shape)
    return jnp.stack([ctx, lat], axis=1).reshape(bsz, 2, 6, d)


def _token_specs(parts, n_ctx_tiles):
    d = parts[0].shape[2]
    if len(parts) == 1:
        return [pl.BlockSpec((1, TM, d), lambda b, i: (b, i, 0))]
    return [pl.BlockSpec((1, TM, d), lambda b, i: (b, jnp.minimum(i, n_ctx_tiles - 1), 0)),
            pl.BlockSpec((1, TM, d), lambda b, i: (b, jnp.maximum(i - n_ctx_tiles, 0), 0))]


def _token_tile(refs, n_ctx_tiles):
    if len(refs) == 1:
        return refs[0][0]
    return jnp.where(pl.program_id(1) < n_ctx_tiles, refs[0][0], refs[1][0])


def _inproj_body(*refs, splits, n_parts, n_ctx_tiles):
    x = _token_tile(refs[:n_parts], n_ctx_tiles)
    mod_ref, n_ref, w_ref = refs[n_parts:n_parts + 3]
    out_refs = refs[n_parts + 3:]
    h = _rms(x, n_ref[...]) * (1.0 + mod_ref[0, 0, 1:2, :]) + mod_ref[0, 0, 0:1, :]
    r = _dot(h.astype(BF16), w_ref[...])
    for o_ref, (lo, hi) in zip(out_refs, splits):
        o_ref[0] = r[:, lo:hi].astype(o_ref.dtype)


def _inproj(parts, mod, norm, w_bf16, splits, n_ctx_tiles):
    bsz, _, d = parts[0].shape
    t = sum(p.shape[1] for p in parts)
    n = w_bf16.shape[1]
    widths = [hi - lo for lo, hi in splits]
    return pl.pallas_call(
        functools.partial(_inproj_body, splits=tuple(splits), n_parts=len(parts), n_ctx_tiles=n_ctx_tiles),
        grid=(bsz, t // TM),
        in_specs=_token_specs(parts, n_ctx_tiles) + [
            pl.BlockSpec((1, 1, 6, d), lambda b, i: (b, (i >= n_ctx_tiles).astype(jnp.int32), 0, 0)),
            _resident((1, d)),
            _resident((d, n))],
        out_specs=[pl.BlockSpec((1, TM, wd), lambda b, i: (b, i, 0)) for wd in widths],
        out_shape=[jax.ShapeDtypeStruct((bsz, t, wd), F32) for wd in widths],
        compiler_params=_params("arbitrary", "arbitrary"),
        name="inproj",
    )(*parts, mod, norm.reshape(1, d), w_bf16)


def _attn_body(q_ref, k_ref, v_ref, cq_ref, sq_ref, ck_ref, sk_ref, qg_ref, kg_ref, o_ref, kt_s, v2_s,
               *, n_ctx_tiles, ctx_len):
    i = pl.program_id(1)
    t = k_ref.shape[1]
    lane = lax.broadcasted_iota(jnp.int32, (1, LANES), 1)
    low = lane < HEAD_DIM
    even = (lane % 2) == 0

    def norm_rope(x, gain, cos, sin):
        x2 = x * x
        s_lo = jnp.sum(jnp.where(low, x2, 0.0), axis=-1, keepdims=True)
        s_hi = jnp.sum(jnp.where(low, 0.0, x2), axis=-1, keepdims=True)
        ms = jnp.where(low, s_lo, s_hi) * (1.0 / HEAD_DIM)
        xn = x * lax.rsqrt(ms + EPS) * gain
        swapped = jnp.where(even, pltpu.roll(xn, LANES - 1, 1), pltpu.roll(xn, 1, 1))
        return xn * cos + swapped * sin

    @pl.when(i == 0)
    def _():
        k = norm_rope(k_ref[0], kg_ref[...], ck_ref[...], sk_ref[...])
        kr = pltpu.roll(k, HEAD_DIM, 1)
        kt_s[0] = jnp.where(low, k, kr).T.astype(BF16)
        kt_s[1] = jnp.where(low, kr, k).T.astype(BF16)
        v = v_ref[0]
        vr = pltpu.roll(v, HEAD_DIM, 1)
        v2_s[0] = jnp.where(low, v, vr).astype(BF16)
        v2_s[1] = jnp.where(low, vr, v).astype(BF16)

    def run(tk):
        for j in range(ATTN_HEADS // 2):
            g = (2 * j) // (ATTN_HEADS // KV_HEADS)
            qp = norm_rope(q_ref[0, :, j * LANES:(j + 1) * LANES], qg_ref[...], cq_ref[...], sq_ref[...])
            qp = qp * (HEAD_DIM ** -0.5)
            outs = []
            for hh in range(2):
                qm = jnp.where(low if hh == 0 else jnp.logical_not(low), qp, 0.0).astype(BF16)
                s = _dot(qm, kt_s[g, :, 0:tk])
                m = jnp.max(s, axis=-1, keepdims=True)
                p = jnp.exp(s - m)
                l = jnp.sum(p, axis=-1, keepdims=True)
                o = _dot(p.astype(BF16), v2_s[g, 0:tk, :])
                outs.append(o / l)
            o_ref[0, :, j * LANES:(j + 1) * LANES] = jnp.where(low, outs[0], outs[1]).astype(o_ref.dtype)

    @pl.when(i < n_ctx_tiles)
    def _():
        run(ctx_len)

    @pl.when(i >= n_ctx_tiles)
    def _():
        run(t)


def _attention(q, k, v, cos2, sin2, q_gain, k_gain, ctx_len):
    bsz, t, _ = q.shape
    n_ctx_tiles = ctx_len // TM
    qg = jnp.tile(q_gain.reshape(1, HEAD_DIM), (1, 2))
    kg = jnp.tile(k_gain.reshape(1, HEAD_DIM), (1, 2))
    return pl.pallas_call(
        functools.partial(_attn_body, n_ctx_tiles=n_ctx_tiles, ctx_len=ctx_len),
        grid=(bsz, t // TM),
        in_specs=[pl.BlockSpec((1, TM, ATTN_W), lambda b, i: (b, i, 0)),
                  pl.BlockSpec((1, t, KV_W), lambda b, i: (b, 0, 0)),
                  pl.BlockSpec((1, t, KV_W), lambda b, i: (b, 0, 0)),
                  pl.BlockSpec((TM, LANES), lambda b, i: (i, 0)),
                  pl.BlockSpec((TM, LANES), lambda b, i: (i, 0)),
                  _resident((t, LANES)),
                  _resident((t, LANES)),
                  _resident((1, LANES)),
                  _resident((1, LANES))],
        out_specs=pl.BlockSpec((1, TM, ATTN_W), lambda b, i: (b, i, 0)),
        out_shape=jax.ShapeDtypeStruct((bsz, t, ATTN_W), BF16),
        scratch_shapes=[pltpu.VMEM((KV_HEADS, LANES, t), BF16), pltpu.VMEM((KV_HEADS, t, LANES), BF16)],
        compiler_params=_params("arbitrary", "arbitrary"),
        name="attention",
    )(q, k, v, cos2, sin2, cos2, sin2, qg, kg)


def _rope_tables(ctx_len, seq_len):
    rows = seq_len // GRID_W
    t_row = jnp.repeat(jnp.arange(rows, dtype=F32), GRID_W)
    t_col = jnp.tile(jnp.arange(GRID_W, dtype=F32), rows)
    axis_dim = HEAD_DIM // 2
    inv_freq = ROPE_THETA ** (-jnp.arange(0, axis_dim, 2, dtype=F32) / axis_dim)
    ang = jnp.concatenate([t_row[:, None] * inv_freq, t_col[:, None] * inv_freq], axis=-1)
    cos = jnp.repeat(jnp.cos(ang), 2, axis=-1)
    sin = jnp.repeat(jnp.sin(ang), 2, axis=-1) * jnp.tile(jnp.array([-1.0, 1.0], F32), axis_dim)
    cos = jnp.concatenate([jnp.ones((ctx_len, HEAD_DIM), F32), cos], axis=0)
    sin = jnp.concatenate([jnp.zeros((ctx_len, HEAD_DIM), F32), sin], axis=0)
    return jnp.tile(cos, (1, 2)), jnp.tile(sin, (1, 2))


def _fill_padded(src_ref, pad_s, width, ctx_len, seq_len, pad, rows, fn):
    t = ctx_len + seq_len
    z = jnp.zeros((pad + SUBLANES, width), F32)
    pad_s[0:pad, :] = z[0:pad]
    pad_s[pad + ctx_len:2 * pad + ctx_len, :] = z[0:pad]
    pad_s[2 * pad + t:3 * pad + t + SUBLANES, :] = z

    def seg(tok0, off, ntiles):
        def body(n, carry):
            r = pl.multiple_of(tok0 + n * rows, rows)
            pad_s[pl.ds(r + off, rows), :] = fn(src_ref, r, rows)
            return carry
        lax.fori_loop(0, ntiles, body, 0)

    seg(0, pad, ctx_len // rows)
    seg(ctx_len, 2 * pad, seq_len // rows)


def _conv_segments(pad_s, w_ref, taps, ctx_len, seq_len, pad, rows, emit):
    half = taps // 2
    span = rows + SUBLANES

    def seg(tok0, off, ntiles):
        def body(n, carry):
            r = pl.multiple_of(tok0 + n * rows, rows)
            base = r + off - pad
            acc = None
            for rho in range(SUBLANES):
                part = None
                for j in range(taps):
                    dj = pad - half + j
                    if dj % SUBLANES != rho:
                        continue
                    term = w_ref[j:j + 1, :] * pad_s[pl.ds(pl.multiple_of(base + (dj - rho), SUBLANES), span), :]
                    part = term if part is None else part + term
                if part is None:
                    continue
                if rho:
                    part = pltpu.roll(part, span - rho, 0)
                acc = part[0:rows] if acc is None else acc + part[0:rows]
            emit(r, acc)
            return carry
        lax.fori_loop(0, ntiles, body, 0)

    seg(0, pad, ctx_len // rows)
    seg(ctx_len, 2 * pad, seq_len // rows)


CC_PAD = 16
CC_ROWS = 32


def _cconv_body(u_ref, w_ref, b_ref, g_ref, bb_ref, o_ref, pad_s, *, ctx_len, seq_len):
    c = o_ref.shape[2]

    def glu(src_ref, r, rows):
        return src_ref[0, pl.ds(r, rows), 0:c] * jax.nn.sigmoid(src_ref[0, pl.ds(r, rows), c:2 * c])

    _fill_padded(u_ref, pad_s, c, ctx_len, seq_len, CC_PAD, TM, glu)

    def emit(r, acc):
        h = acc + b_ref[...]
        mu = jnp.mean(h, axis=-1, keepdims=True)
        hc = h - mu
        var = jnp.mean(hc * hc, axis=-1, keepdims=True)
        y = hc * lax.rsqrt(var + EPS) * g_ref[...] + bb_ref[...]
        o_ref[0, pl.ds(r, CC_ROWS), :] = _silu(y).astype(o_ref.dtype)

    _conv_segments(pad_s, w_ref, w_ref.shape[0], ctx_len, seq_len, CC_PAD, CC_ROWS, emit)


def _conformer_conv(u, dw_w, dw_b, ln_g, ln_b, ctx_len):
    bsz, t, c2 = u.shape
    c = c2 // 2
    taps = dw_w.shape[0]
    return pl.pallas_call(
        functools.partial(_cconv_body, ctx_len=ctx_len, seq_len=t - ctx_len),
        grid=(bsz,),
        in_specs=[pl.BlockSpec((1, t, c2), lambda b: (b, 0, 0)),
                  _resident((taps, c)), _resident((1, c)), _resident((1, c)), _resident((1, c))],
        out_specs=pl.BlockSpec((1, t, c), lambda b: (b, 0, 0)),
        out_shape=jax.ShapeDtypeStruct((bsz, t, c), BF16),
        scratch_shapes=[pltpu.VMEM((t + 3 * CC_PAD + SUBLANES, c), F32)],
        compiler_params=_params("arbitrary"),
        name="conformer_conv",
    )(u, dw_w, dw_b.reshape(1, c), ln_g.reshape(1, c), ln_b.reshape(1, c))


SC_PAD = 8
SC_ROWS = 64
SC_COLS = 512


def _sconv_body(x_ref, w_ref, b_ref, o_ref, pad_s, *, ctx_len, seq_len):
    def ident(src_ref, r, rows):
        return src_ref[0, pl.ds(r, rows), :]

    _fill_padded(x_ref, pad_s, SC_COLS, ctx_len, seq_len, SC_PAD, TM, ident)

    def emit(r, acc):
        o_ref[0, pl.ds(r, SC_ROWS), :] = _silu(acc + b_ref[...])

    _conv_segments(pad_s, w_ref, w_ref.shape[0], ctx_len, seq_len, SC_PAD, SC_ROWS, emit)


def _ssm_conv(xbc, conv_w, conv_b, ctx_len):
    bsz, t, c = xbc.shape
    taps = conv_w.shape[0]
    return pl.pallas_call(
        functools.partial(_sconv_body, ctx_len=ctx_len, seq_len=t - ctx_len),
        grid=(bsz, c // SC_COLS),
        in_specs=[pl.BlockSpec((1, t, SC_COLS), lambda b, j: (b, 0, j)),
                  pl.BlockSpec((taps, SC_COLS), lambda b, j: (0, j)),
                  pl.BlockSpec((1, SC_COLS), lambda b, j: (0, j))],
        out_specs=pl.BlockSpec((1, t, SC_COLS), lambda b, j: (b, 0, j)),
        out_shape=jax.ShapeDtypeStruct((bsz, t, c), F32),
        scratch_shapes=[pltpu.VMEM((t + 3 * SC_PAD + SUBLANES, SC_COLS), F32)],
        compiler_params=_params("arbitrary", "arbitrary"),
        name="ssm_conv",
    )(xbc, conv_w, conv_b.reshape(1, c))


def _even_ffn_body(*refs, n_parts, n_ctx_tiles):
    x = _token_tile(refs[:n_parts], n_ctx_tiles)
    a_ref, c_ref, mod_ref, n2_ref, wo_ref, w1_ref, w3_ref, w2_ref, o_ref = refs[n_parts:]
    ca = a_ref.shape[2]
    mix = _dot(a_ref[0], wo_ref[0:ca, :]) + _dot(c_ref[0], wo_ref[ca:, :])
    x1 = x + mod_ref[0, 0, 2:3, :] * mix
    h = _rms(x1, n2_ref[...]) * (1.0 + mod_ref[0, 0, 4:5, :]) + mod_ref[0, 0, 3:4, :]
    hb = h.astype(BF16)
    u = _silu(_dot(hb, w1_ref[...])) * _dot(hb, w3_ref[...])
    o_ref[0] = x1 + mod_ref[0, 0, 5:6, :] * _dot(u.astype(BF16), w2_ref[...])


def _even_ffn(parts, attn, conv, mod, norm2, wo, w1, w3, w2, n_ctx_tiles):
    bsz, t, ca = attn.shape
    d = parts[0].shape[2]
    cc = conv.shape[2]
    return pl.pallas_call(
        functools.partial(_even_ffn_body, n_parts=len(parts), n_ctx_tiles=n_ctx_tiles),
        grid=(bsz, t // TM),
        in_specs=_token_specs(parts, n_ctx_tiles) + [
                  pl.BlockSpec((1, TM, ca), lambda b, i: (b, i, 0)),
                  pl.BlockSpec((1, TM, cc), lambda b, i: (b, i, 0)),
                  pl.BlockSpec((1, 1, 6, d), lambda b, i: (b, (i >= n_ctx_tiles).astype(jnp.int32), 0, 0)),
                  _resident((1, d)), _resident(wo.shape), _resident(w1.shape), _resident(w3.shape),
                  _resident(w2.shape)],
        out_specs=pl.BlockSpec((1, TM, d), lambda b, i: (b, i, 0)),
        out_shape=jax.ShapeDtypeStruct((bsz, t, d), F32),
        compiler_params=_params("arbitrary", "arbitrary"),
        name="even_ffn",
    )(*parts, attn, conv, mod, norm2.reshape(1, d), wo, w1, w3, w2)


def _softplus(x):
    return jnp.maximum(x, 0.0) + jnp.log(1.0 + jnp.exp(-jnp.abs(x)))


def _ssd_body(xs_ref, bm_ref, cm_ref, dt_ref, dtt_ref, alr_ref, alc_ref, bir_ref, bic_ref, *rest,
              rev, n_ctx_chunks, combine):
    if combine:
        yb_ref, dsk_ref, y_ref, st_s = rest
    else:
        y_ref, st_s = rest
    c = pl.program_id(1)
    nh = alr_ref.shape[1]
    hpg = nh // SSM_GROUPS
    gw = hpg * SSM_HEADDIM
    end = 0 if rev else CHUNK - 1

    @pl.when(c == 0)
    def _():
        st_s[...] = jnp.zeros_like(st_s)

    off = nh if rev else 0
    dt = _softplus(dt_ref[0, :, off:off + nh] + bir_ref[...])
    dtt = _softplus(dtt_ref[0, off:off + nh, :] + bic_ref[...])
    a = dt * (-jnp.exp(alr_ref[...]))
    at = dtt * (-jnp.exp(alc_ref[...]))

    row = lax.broadcasted_iota(jnp.int32, (CHUNK, CHUNK), 0)
    col = lax.broadcasted_iota(jnp.int32, (CHUNK, CHUNK), 1)
    tri = (col >= row) if rev else (col <= row)
    trit = (row >= col) if rev else (row <= col)
    tri_b = tri.astype(BF16)
    trit_b = trit.astype(BF16)
    acs = sum(_dot(tri_b, p) for p in _split3(a))
    acst = sum(_dot(p, trit_b) for p in _split3(at))
    tot = acs[end:end + 1, :]
    w_end = dt * jnp.exp(tot - acs)
    eacs = jnp.exp(acs)
    cdec = jnp.exp(tot)

    lane = lax.broadcasted_iota(jnp.int32, (1, LANES), 1)
    low = lane < SSM_HEADDIM

    def pair(v, e0):
        return jnp.where(low, v[:, e0:e0 + 1], v[:, e0 + 1:e0 + 2])

    def step(with_y):
        for g in range(SSM_GROUPS):
            bm = bm_ref[0, :, g * D_STATE:(g + 1) * D_STATE]
            bmt = bm.T.astype(BF16)
            sg = st_s[g]

            xcd = []
            for kp in range(hpg // 2):
                e0 = g * hpg + 2 * kp
                xs_p = xs_ref[0, :, e0 * SSM_HEADDIM:e0 * SSM_HEADDIM + LANES]
                xcd.append((xs_p * pair(w_end, e0)).astype(BF16))
            new_state = _dot(bmt, jnp.concatenate(xcd, axis=1))
            dec = jnp.concatenate([pair(cdec, g * hpg + 2 * kp) for kp in range(hpg // 2)], axis=1)
            st_s[g] = sg * dec + new_state

            if not with_y:
                continue
            cmb = cm_ref[0, :, g * D_STATE:(g + 1) * D_STATE].astype(BF16)
            cb = _dot(cmb, bmt)
            yoff = _dot(cmb, sg.astype(BF16))
            for kp in range(hpg // 2):
                e0 = g * hpg + 2 * kp
                c0 = e0 * SSM_HEADDIM
                xs_p = xs_ref[0, :, c0:c0 + LANES]
                xcb = (xs_p * pair(dt, e0)).astype(BF16)
                res = []
                for e in (e0, e0 + 1):
                    diff = acs[:, e:e + 1] - acst[e:e + 1, :]
                    m = (jnp.exp(jnp.where(tri, diff, -jnp.inf)) * cb).astype(BF16)
                    res.append(_dot(m, xcb))
                y = jnp.where(low, res[0], res[1]) + yoff[:, kp * LANES:(kp + 1) * LANES] * pair(eacs, e0)
                if combine:
                    y = y + yb_ref[0, :, c0:c0 + LANES] + pair(dsk_ref[...], e0) * xs_p
                y_ref[0, :, c0:c0 + LANES] = y

    @pl.when(c >= n_ctx_chunks)
    def _():
        step(True)

    @pl.when(c < n_ctx_chunks)
    def _():
        step(False)


def _ssd_scan(xbc, dt, dtt, a_log, dt_bias, ctx_len, rev, y_other=None, d_skip=None):
    bsz, t, _ = xbc.shape
    nh = a_log.shape[0]
    d_inner = nh * SSM_HEADDIM
    gn = SSM_GROUPS * D_STATE
    nc = t // CHUNK
    ncc = ctx_len // CHUNK
    seq_len = t - ctx_len
    combine = y_other is not None

    if rev:
        def chunk(i):
            return jnp.where(i < ncc, ncc - 1 - i, nc - 1 + ncc - i)
    else:
        def chunk(i):
            return i

    def ychunk(i):
        return jnp.maximum(chunk(i), ncc) - ncc if not rev else jnp.where(i < ncc, nc - 1 - ncc, chunk(i) - ncc)

    bcol = d_inner // gn
    in_specs = [pl.BlockSpec((1, CHUNK, d_inner), lambda b, i: (b, chunk(i), 0)),
                pl.BlockSpec((1, CHUNK, gn), lambda b, i: (b, chunk(i), bcol)),
                pl.BlockSpec((1, CHUNK, gn), lambda b, i: (b, chunk(i), bcol + 1)),
                pl.BlockSpec((1, CHUNK, 2 * nh), lambda b, i: (b, chunk(i), 0)),
                pl.BlockSpec((1, 2 * nh, CHUNK), lambda b, i: (b, 0, chunk(i))),
                _resident((1, nh)), _resident((nh, 1)), _resident((1, nh)), _resident((nh, 1))]
    args = [xbc, xbc, xbc, dt, dtt, a_log.reshape(1, nh), a_log.reshape(nh, 1),
            dt_bias.reshape(1, nh), dt_bias.reshape(nh, 1)]
    if combine:
        in_specs += [pl.BlockSpec((1, CHUNK, d_inner), lambda b, i: (b, ychunk(i), 0)), _resident((1, nh))]
        args += [y_other, d_skip.reshape(1, nh)]
    return pl.pallas_call(
        functools.partial(_ssd_body, rev=rev, n_ctx_chunks=ncc, combine=combine),
        grid=(bsz, nc),
        in_specs=in_specs,
        out_specs=pl.BlockSpec((1, CHUNK, d_inner), lambda b, i: (b, ychunk(i), 0)),
        out_shape=jax.ShapeDtypeStruct((bsz, seq_len, d_inner), F32),
        scratch_shapes=[pltpu.VMEM((SSM_GROUPS, D_STATE, d_inner // SSM_GROUPS), F32)],
        compiler_params=_params("arbitrary", "arbitrary"),
        name="ssd_bwd" if rev else "ssd_fwd",
    )(*args)


def _odd_out_body(x_ref, y_ref, z_ref, mod_ref, gn_ref, n2_ref, wout_ref, wr_ref, x1_ref, h_ref, ti_ref, tw_ref):
    z = z_ref[0]
    yn = _rms(y_ref[0] * _silu(z), gn_ref[...])
    x1 = x_ref[0] + mod_ref[0, 0, 2:3, :] * _dot(yn.astype(BF16), wout_ref[...])
    x1_ref[0] = x1
    h = _rms(x1, n2_ref[...]) * (1.0 + mod_ref[0, 0, 4:5, :]) + mod_ref[0, 0, 3:4, :]
    h_ref[0] = h
    ne = wr_ref.shape[1]
    h1, h2, _ = _split3(h)
    w1, w2, _ = _split3(wr_ref[...])
    hw = _dot(h1, jnp.concatenate([w1, w2], axis=1))
    lg = hw[:, 0:ne] + hw[:, ne:2 * ne] + _dot(h2, w1)
    idx = lax.broadcasted_iota(jnp.int32, lg.shape, 1)
    m1 = jnp.max(lg, axis=-1, keepdims=True)
    i1 = jnp.min(jnp.where(lg == m1, idx, ne), axis=-1, keepdims=True)
    lg2 = jnp.where(idx == i1, -jnp.inf, lg)
    m2 = jnp.max(lg2, axis=-1, keepdims=True)
    i2 = jnp.min(jnp.where(lg2 == m2, idx, ne), axis=-1, keepdims=True)
    e2 = jnp.exp(m2 - m1)
    den = 1.0 + e2
    ti_ref[0] = jnp.concatenate([i1, i2], axis=1)
    tw_ref[0] = jnp.concatenate([1.0 / den, e2 / den], axis=1)


def _odd_out(x, y, z, mod, gnorm, norm2, w_out, w_router, ctx_len):
    bsz, t, d = x.shape
    seq_len = t - ctx_len
    di = y.shape[2]
    ne = w_router.shape[1]
    nct = ctx_len // TM
    lat = lambda b, i: (b, i + nct, 0)
    own = lambda b, i: (b, i, 0)
    return pl.pallas_call(
        _odd_out_body,
        grid=(bsz, seq_len // TM),
        in_specs=[pl.BlockSpec((1, TM, d), lat),
                  pl.BlockSpec((1, TM, di), own),
                  pl.BlockSpec((1, TM, di), lat),
                  pl.BlockSpec((1, 1, 6, d), lambda b, i: (b, 1, 0, 0)),
                  _resident((1, di)), _resident((1, d)), _resident(w_out.shape), _resident((d, ne))],
        out_specs=[pl.BlockSpec((1, TM, d), own), pl.BlockSpec((1, TM, d), own),
                   pl.BlockSpec((1, TM, TOP_K), own), pl.BlockSpec((1, TM, TOP_K), own)],
        out_shape=[jax.ShapeDtypeStruct((bsz, seq_len, d), F32), jax.ShapeDtypeStruct((bsz, seq_len, d), F32),
                   jax.ShapeDtypeStruct((bsz, seq_len, TOP_K), jnp.int32),
                   jax.ShapeDtypeStruct((bsz, seq_len, TOP_K), F32)],
        compiler_params=_params("arbitrary", "arbitrary"),
        name="odd_out",
    )(x, y, z, mod, gnorm.reshape(1, di), norm2.reshape(1, d), w_out, w_router)


def _start_row(idx_ref, j, src_hbm, dst, sem):
    pltpu.make_async_copy(src_hbm.at[pl.ds(idx_ref[0, 0, j], 1), :], dst.at[pl.ds(j, 1), :], sem).start()


def _start_rows(idx_ref, src_hbm, dst, sem, n):
    def body(j, carry):
        _start_row(idx_ref, j, src_hbm, dst, sem)
        return carry
    lax.fori_loop(0, n, body, 0)


def _wait_rows(src_hbm, dst, sem, n):
    pltpu.make_async_copy(src_hbm.at[pl.ds(0, n), :], dst, sem).wait()


def _expert_body(te_ref, na_ref, rows_ref, next_rows_ref, h_hbm, w1_ref, w3_ref, w2_ref, y_ref,
                 xf_s, xb_s, acc_s, sem, *, nf):
    ti = pl.program_id(0)
    f = pl.program_id(1)
    n_active = na_ref[0]
    active = ti < n_active
    has_next = ti + 1 < n_active
    slot = ti % 2
    part = TM_E // nf

    @pl.when(jnp.logical_and(ti == 0, f == 0))
    def _():
        _start_rows(rows_ref, h_hbm, xf_s.at[0], sem.at[0], TM_E)

    @pl.when(jnp.logical_and(active, f == 0))
    def _():
        _wait_rows(h_hbm, xf_s.at[slot], sem.at[slot], TM_E)
        xb_s[...] = xf_s[slot].astype(BF16)
        acc_s[...] = jnp.zeros_like(acc_s)

    def compute():
        xb = xb_s[...]
        u = _silu(_dot(xb, w1_ref[0])) * _dot(xb, w3_ref[0])
        acc_s[...] += _dot(u.astype(BF16), w2_ref[0])

    @pl.when(has_next)
    def _():
        base = f * part
        for j in range(part):
            _start_row(next_rows_ref, base + j, h_hbm, xf_s.at[1 - slot], sem.at[1 - slot])
        compute()

    @pl.when(jnp.logical_and(active, jnp.logical_not(has_next)))
    def _():
        compute()

    last = f == nf - 1

    @pl.when(jnp.logical_and(active, last))
    def _():
        y_ref[...] = acc_s[...]

    @pl.when(jnp.logical_and(jnp.logical_not(active), last))
    def _():
        y_ref[...] = jnp.zeros_like(y_ref)


def _expert_ffn(h, tile_expert, n_active, row_token, w1, w3, w2):
    n, d = h.shape
    ns = row_token.shape[0]
    nt = ns // TM_E
    dff = w1.shape[2]
    nf = dff // TF_E

    def wcol(ti, f, te, na):
        return (te[ti], 0, jnp.where(ti < na[0], f, nf - 1))

    def wrow(ti, f, te, na):
        return (te[ti], jnp.where(ti < na[0], f, nf - 1), 0)

    grid_spec = pltpu.PrefetchScalarGridSpec(
        num_scalar_prefetch=2,
        grid=(nt, nf),
        in_specs=[pl.BlockSpec((1, 1, TM_E), lambda ti, f, te, na: (ti, 0, 0), memory_space=pltpu.SMEM),
                  pl.BlockSpec((1, 1, TM_E), lambda ti, f, te, na: (jnp.minimum(ti + 1, nt - 1), 0, 0),
                               memory_space=pltpu.SMEM),
                  pl.BlockSpec(memory_space=pl.ANY),
                  pl.BlockSpec((1, d, TF_E), wcol),
                  pl.BlockSpec((1, d, TF_E), wcol),
                  pl.BlockSpec((1, TF_E, d), wrow)],
        out_specs=pl.BlockSpec((TM_E, d), lambda ti, f, te, na: (ti, 0)),
        scratch_shapes=[pltpu.VMEM((2, TM_E, d), F32), pltpu.VMEM((TM_E, d), BF16), pltpu.VMEM((TM_E, d), F32),
                        pltpu.SemaphoreType.DMA((2,))],
    )
    rows = row_token.reshape(nt, 1, TM_E)
    return pl.pallas_call(
        functools.partial(_expert_body, nf=nf),
        grid_spec=grid_spec,
        out_shape=jax.ShapeDtypeStruct((ns, d), F32),
        compiler_params=_params("arbitrary", "arbitrary"),
        name="expert_ffn",
    )(tile_expert, n_active, rows, rows, h, w1, w3, w2)


def _combine_body(pos_ref, next_pos_ref, x_ref, tw_ref, g2_ref, fn_ref, y_hbm, o_ref, y_s, sem):
    i = pl.program_id(0)
    slot = i % 2
    n = TOP_K * TM_C

    @pl.when(i == 0)
    def _():
        _start_rows(pos_ref, y_hbm, y_s.at[0], sem.at[0], n)

    @pl.when(i + 1 < pl.num_programs(0))
    def _():
        _start_rows(next_pos_ref, y_hbm, y_s.at[1 - slot], sem.at[1 - slot], n)

    _wait_rows(y_hbm, y_s.at[slot], sem.at[slot], n)
    tw = tw_ref[...]
    moe = tw[:, 0:1] * y_s[slot, 0:TM_C, :] + tw[:, 1:2] * y_s[slot, TM_C:2 * TM_C, :]
    o_ref[...] = _rms(x_ref[...] + g2_ref[0] * moe, fn_ref[...])


def _moe_combine(x1, topw, pos, g2, final_norm, y_sorted, tiles_per_batch):
    n, d = x1.shape
    nt = n // TM_C
    pos_t = pos.reshape(nt, TM_C, TOP_K).transpose(0, 2, 1).reshape(nt, 1, TOP_K * TM_C)
    return pl.pallas_call(
        _combine_body,
        grid=(nt,),
        in_specs=[pl.BlockSpec((1, 1, TOP_K * TM_C), lambda i: (i, 0, 0), memory_space=pltpu.SMEM),
                  pl.BlockSpec((1, 1, TOP_K * TM_C), lambda i: (jnp.minimum(i + 1, nt - 1), 0, 0),
                               memory_space=pltpu.SMEM),
                  pl.BlockSpec((TM_C, d), lambda i: (i, 0)),
                  pl.BlockSpec((TM_C, TOP_K), lambda i: (i, 0)),
                  pl.BlockSpec((1, 1, d), lambda i: (i // tiles_per_batch, 0, 0)),
                  _resident((1, d)),
                  pl.BlockSpec(memory_space=pl.ANY)],
        out_specs=pl.BlockSpec((TM_C, d), lambda i: (i, 0)),
        out_shape=jax.ShapeDtypeStruct((n, d), F32),
        scratch_shapes=[pltpu.VMEM((2, TOP_K * TM_C, d), F32), pltpu.SemaphoreType.DMA((2,))],
        compiler_params=_params("arbitrary"),
        name="moe_combine",
    )(pos_t, pos_t, x1, topw, g2, final_norm.reshape(1, d), y_sorted)


def _route(topi, n_slots):
    n = topi.shape[0]
    e_flat = topi.reshape(n * TOP_K)
    oh = (e_flat[:, None] == jnp.arange(N_EXPERTS, dtype=jnp.int32)[None, :]).astype(jnp.int32)
    csum = jnp.cumsum(oh, axis=0)
    rank = jnp.sum(csum * oh, axis=1) - 1
    cnt = csum[-1]
    cnt_pad = (cnt + TM_E - 1) // TM_E * TM_E
    ends = jnp.cumsum(cnt_pad)
    offs = ends - cnt_pad
    slot = jnp.sum(oh * offs[None, :], axis=1) + rank
    n_active = (ends[-1] // TM_E).astype(jnp.int32)
    nt = n_slots // TM_E
    tile_start = jnp.arange(nt, dtype=jnp.int32) * TM_E
    te = jnp.sum((tile_start[:, None] >= ends[None, :]).astype(jnp.int32), axis=1)
    te = jnp.minimum(te, N_EXPERTS - 1)
    te = jnp.where(jnp.arange(nt) < n_active, te, te[jnp.maximum(n_active - 1, 0)])
    row_token = jnp.zeros((n_slots,), jnp.int32).at[slot].set(jnp.arange(n * TOP_K, dtype=jnp.int32) // TOP_K)
    return slot.reshape(n, TOP_K), te.astype(jnp.int32), n_active.reshape(1), row_token


def kernel(x, c, ctx, c_ctx, ev_ada_w, ev_ada_b, ev_norm1, ev_norm2, ev_w_in, ev_q_gain, ev_k_gain, ev_dw_w, ev_dw_b, ev_ln_g, ev_ln_b, ev_w_o, ev_ff_w1, ev_ff_w3, ev_ff_w2, od_ada_w, od_ada_b, od_norm1, od_norm2, od_w_in, od_conv_w, od_conv_b, od_a_log_f, od_a_log_b, od_dt_bias_f, od_dt_bias_b, od_d_skip, od_gnorm, od_w_out, od_router, od_ex_w1, od_ex_w3, od_ex_w2, final_norm):
    bsz, seq_len, d = x.shape
    ctx_len = ctx.shape[1]
    assert ev_ada_w.shape[0] == 1 and od_ada_w.shape[0] == 1, "one even and one odd layer"
    assert ctx_len % TM == 0 and seq_len % TM == 0 and seq_len % GRID_W == 0
    n_ctx_tiles = ctx_len // TM
    mod = _mod_table(c, c_ctx, ev_ada_w[0], ev_ada_b[0])
    cc = ev_dw_w.shape[2]
    splits = [(0, ATTN_W), (ATTN_W, ATTN_W + KV_W), (ATTN_W + KV_W, ATTN_W + 2 * KV_W),
              (ATTN_W + 2 * KV_W, ATTN_W + 2 * KV_W + 2 * cc)]
    q, k, v, u = _inproj((ctx, x), mod, ev_norm1[0], ev_w_in[0].astype(BF16), splits, n_ctx_tiles)
    cos2, sin2 = _rope_tables(ctx_len, seq_len)
    attn = _attention(q, k, v, cos2, sin2, ev_q_gain[0], ev_k_gain[0], ctx_len)
    conv = _conformer_conv(u, ev_dw_w[0], ev_dw_b[0], ev_ln_g[0], ev_ln_b[0], ctx_len)
    xa = _even_ffn((ctx, x), attn, conv, mod, ev_norm2[0], ev_w_o[0].astype(BF16), ev_ff_w1[0].astype(BF16),
                   ev_ff_w3[0].astype(BF16), ev_ff_w2[0].astype(BF16), n_ctx_tiles)

    mod = _mod_table(c, c_ctx, od_ada_w[0], od_ada_b[0])
    nh = od_a_log_f.shape[1]
    d_inner = nh * SSM_HEADDIM
    conv_dim = od_conv_w.shape[2]
    splits = [(0, d_inner), (d_inner, d_inner + conv_dim), (d_inner + conv_dim, d_inner + conv_dim + 2 * nh)]
    z, xbc, dt = _inproj((xa,), mod, od_norm1[0], od_w_in[0].astype(BF16), splits, n_ctx_tiles)
    xbc = _ssm_conv(xbc, od_conv_w[0], od_conv_b[0], ctx_len)
    dtt = dt.transpose(0, 2, 1)
    y_b = _ssd_scan(xbc, dt, dtt, od_a_log_b[0], od_dt_bias_b[0], ctx_len, rev=True)
    y = _ssd_scan(xbc, dt, dtt, od_a_log_f[0], od_dt_bias_f[0], ctx_len, rev=False, y_other=y_b,
                  d_skip=od_d_skip[0])
    x1, h, topi, topw = _odd_out(xa, y, z, mod, od_gnorm[0], od_norm2[0], od_w_out[0].astype(BF16),
                                 od_router[0], ctx_len)

    n = bsz * seq_len
    n_slots = n * TOP_K + N_EXPERTS * TM_E
    pos, tile_expert, n_active, row_token = _route(topi.reshape(n, TOP_K), n_slots)
    y_sorted = _expert_ffn(h.reshape(n, d), tile_expert, n_active, row_token, od_ex_w1[0].astype(BF16),
                           od_ex_w3[0].astype(BF16), od_ex_w2[0].astype(BF16))
    g2 = mod[:, 1, 5:6, :]
    out = _moe_combine(x1.reshape(n, d), topw.reshape(n, TOP_K), pos, g2, final_norm, y_sorted, seq_len // TM_C)
    return out.reshape(bsz, seq_len, d)
```

```python
import functools

import jax
import jax.numpy as jnp
from jax import lax
from jax.experimental import pallas as pl
from jax.experimental.pallas import tpu as pltpu

F32 = jnp.float32
BF16 = jnp.bfloat16
EPS = 1e-6
LOG2_E = 1.4426950408889634

GRID_W = 64
HEAD_DIM = 64
ATTN_HEADS = 8
KV_HEADS = 2
ATTN_W = ATTN_HEADS * HEAD_DIM
KV_W = KV_HEADS * HEAD_DIM
ROPE_THETA = 10000.0
SSM_HEADDIM = 64
SSM_GROUPS = 4
D_STATE = 128
CHUNK = 128
N_EXPERTS = 8
TOP_K = 2

LANES = 128
SUBLANES = 8
VMEM_LIMIT = 56 * 1024 * 1024

TM = 256
TM_E = 512
TF_E = 1792
TM_C = 256


def _params(*sem):
    return pltpu.CompilerParams(dimension_semantics=sem, vmem_limit_bytes=VMEM_LIMIT)


def _resident(shape):
    nd = len(shape)
    return pl.BlockSpec(shape, lambda *_: (0,) * nd, pipeline_mode=pl.Buffered(1))


def _silu(x):
    return x * jax.nn.sigmoid(x)


def _rms(x, g):
    return x * lax.rsqrt(jnp.mean(x * x, axis=-1, keepdims=True) + EPS) * g


def _dot(a, b):
    return jnp.dot(a, b, preferred_element_type=F32)


def _split3(a):
    a1 = a.astype(BF16)
    r1 = a - a1.astype(F32)
    a2 = r1.astype(BF16)
    a3 = (r1 - a2.astype(F32)).astype(BF16)
    return a1, a2, a3


def _ada_body(c_ref, w_ref, b_ref, o_ref):
    s = _silu(c_ref[...]).astype(BF16)
    o_ref[...] = _dot(s, w_ref[...].astype(BF16)) + b_ref[...]


def _adaln(cond, w, b):
    r, d = cond.shape
    n = w.shape[1]
    tn = n // 4
    return pl.pallas_call(
        _ada_body,
        grid=(n // tn,),
        in_specs=[pl.BlockSpec((r, d), lambda j: (0, 0)),
                  pl.BlockSpec((d, tn), lambda j: (0, j)),
                  pl.BlockSpec((1, tn), lambda j: (0, j))],
        out_specs=pl.BlockSpec((r, tn), lambda j: (0, j)),
        out_shape=jax.ShapeDtypeStruct((r, n), F32),
        compiler_params=_params("arbitrary"),
        name="adaln",
    )(cond, w, b.reshape(1, n))


def _mod_table(c, c_ctx, w, b):
    bsz, d = c.shape
    rows = -(-(bsz + 1) // SUBLANES) * SUBLANES
    cond = jnp.zeros((rows, d), F32).at[:bsz].set(c).at[bsz].set(c_ctx)
    m = _adaln(cond, w, b)
    lat = m[:bsz]
    ctx = jnp.broadcast_to(m[bsz][None], lat.shape)
    return jnp.stack([ctx, lat], axis=1).reshape(bsz, 2, 6, d)


def _token_specs(parts, n_ctx_tiles):
    d = parts[0].shape[2]
    if len(parts) == 1:
        return [pl.BlockSpec((1, TM, d), lambda b, i: (b, i, 0))]
    return [pl.BlockSpec((1, TM, d), lambda b, i: (b, jnp.minimum(i, n_ctx_tiles - 1), 0)),
            pl.BlockSpec((1, TM, d), lambda b, i: (b, jnp.maximum(i - n_ctx_tiles, 0), 0))]


def _token_tile(refs, n_ctx_tiles):
    if len(refs) == 1:
        return refs[0][0]
    return jnp.where(pl.program_id(1) < n_ctx_tiles, refs[0][0], refs[1][0])


def _inproj_body(*refs, splits, n_parts, n_ctx_tiles):
    x = _token_tile(refs[:n_parts], n_ctx_tiles)
    mod_ref, n_ref, w_ref = refs[n_parts:n_parts + 3]
    out_refs = refs[n_parts + 3:]
    h = _rms(x, n_ref[...]) * (1.0 + mod_ref[0, 0, 1:2, :]) + mod_ref[0, 0, 0:1, :]
    r = _dot(h.astype(BF16), w_ref[...])
    for o_ref, (lo, hi) in zip(out_refs, splits):
        o_ref[0] = r[:, lo:hi].astype(o_ref.dtype)


def _inproj(parts, mod, norm, w_bf16, splits, dtypes, n_ctx_tiles):
    bsz, _, d = parts[0].shape
    t = sum(p.shape[1] for p in parts)
    n = w_bf16.shape[1]
    widths = [hi - lo for lo, hi in splits]
    return pl.pallas_call(
        functools.partial(_inproj_body, splits=tuple(splits), n_parts=len(parts), n_ctx_tiles=n_ctx_tiles),
        grid=(bsz, t // TM),
        in_specs=_token_specs(parts, n_ctx_tiles) + [
            pl.BlockSpec((1, 1, 6, d), lambda b, i: (b, (i >= n_ctx_tiles).astype(jnp.int32), 0, 0)),
            _resident((1, d)),
            _resident((d, n))],
        out_specs=[pl.BlockSpec((1, TM, wd), lambda b, i: (b, i, 0)) for wd in widths],
        out_shape=[jax.ShapeDtypeStruct((bsz, t, wd), dt) for wd, dt in zip(widths, dtypes)],
        compiler_params=_params("arbitrary", "arbitrary"),
        name="inproj",
    )(*parts, mod, norm.reshape(1, d), w_bf16)


def _attn_body(q_ref, k_ref, v_ref, cq_ref, sq_ref, ck_ref, sk_ref, qg_ref, kg_ref, o_ref, kt_s, v2_s,
               *, n_ctx_tiles, ctx_len):
    i = pl.program_id(1)
    t = k_ref.shape[1]
    lane = lax.broadcasted_iota(jnp.int32, (1, LANES), 1)
    low = lane < HEAD_DIM
    even = (lane % 2) == 0

    def norm_rope(x, gain, cos, sin):
        x2 = x * x
        s_lo = jnp.sum(jnp.where(low, x2, 0.0), axis=-1, keepdims=True)
        s_hi = jnp.sum(jnp.where(low, 0.0, x2), axis=-1, keepdims=True)
        ms = jnp.where(low, s_lo, s_hi) * (1.0 / HEAD_DIM)
        xn = x * lax.rsqrt(ms + EPS) * gain
        swapped = jnp.where(even, pltpu.roll(xn, LANES - 1, 1), pltpu.roll(xn, 1, 1))
        return xn * cos + swapped * sin

    @pl.when(i == 0)
    def _():
        k = norm_rope(k_ref[0], kg_ref[...], ck_ref[...], sk_ref[...])
        kr = pltpu.roll(k, HEAD_DIM, 1)
        kt_s[0] = jnp.where(low, k, kr).T.astype(BF16)
        kt_s[1] = jnp.where(low, kr, k).T.astype(BF16)
        v = v_ref[0].astype(F32)
        vr = pltpu.roll(v, HEAD_DIM, 1)
        v2_s[0] = jnp.where(low, v, vr).astype(BF16)
        v2_s[1] = jnp.where(low, vr, v).astype(BF16)

    def run(tk):
        for j in range(ATTN_HEADS // 2):
            g = (2 * j) // (ATTN_HEADS // KV_HEADS)
            qp = norm_rope(q_ref[0, :, j * LANES:(j + 1) * LANES], qg_ref[...], cq_ref[...], sq_ref[...])
            qp = qp * (HEAD_DIM ** -0.5 * LOG2_E)
            outs = []
            for hh in range(2):
                qm = jnp.where(low if hh == 0 else jnp.logical_not(low), qp, 0.0).astype(BF16)
                s = _dot(qm, kt_s[g, :, 0:tk])
                m = jnp.max(s, axis=-1, keepdims=True)
                p = jnp.exp2(s - m)
                l = jnp.sum(p, axis=-1, keepdims=True)
                o = _dot(p.astype(BF16), v2_s[g, 0:tk, :])
                outs.append(o / l)
            o_ref[0, :, j * LANES:(j + 1) * LANES] = jnp.where(low, outs[0], outs[1]).astype(o_ref.dtype)

    @pl.when(i < n_ctx_tiles)
    def _():
        run(ctx_len)

    @pl.when(i >= n_ctx_tiles)
    def _():
        run(t)


def _attention(q, k, v, cos2, sin2, q_gain, k_gain, ctx_len):
    bsz, t, _ = q.shape
    n_ctx_tiles = ctx_len // TM
    qg = jnp.tile(q_gain.reshape(1, HEAD_DIM), (1, 2))
    kg = jnp.tile(k_gain.reshape(1, HEAD_DIM), (1, 2))
    return pl.pallas_call(
        functools.partial(_attn_body, n_ctx_tiles=n_ctx_tiles, ctx_len=ctx_len),
        grid=(bsz, t // TM),
        in_specs=[pl.BlockSpec((1, TM, ATTN_W), lambda b, i: (b, i, 0)),
                  pl.BlockSpec((1, t, KV_W), lambda b, i: (b, 0, 0)),
                  pl.BlockSpec((1, t, KV_W), lambda b, i: (b, 0, 0)),
                  pl.BlockSpec((TM, LANES), lambda b, i: (i, 0)),
                  pl.BlockSpec((TM, LANES), lambda b, i: (i, 0)),
                  _resident((t, LANES)),
                  _resident((t, LANES)),
                  _resident((1, LANES)),
                  _resident((1, LANES))],
        out_specs=pl.BlockSpec((1, TM, ATTN_W), lambda b, i: (b, i, 0)),
        out_shape=jax.ShapeDtypeStruct((bsz, t, ATTN_W), BF16),
        scratch_shapes=[pltpu.VMEM((KV_HEADS, LANES, t), BF16), pltpu.VMEM((KV_HEADS, t, LANES), BF16)],
        compiler_params=_params("arbitrary", "arbitrary"),
        name="attention",
    )(q, k, v, cos2, sin2, cos2, sin2, qg, kg)


def _rope_tables(ctx_len, seq_len):
    rows = seq_len // GRID_W
    t_row = jnp.repeat(jnp.arange(rows, dtype=F32), GRID_W)
    t_col = jnp.tile(jnp.arange(GRID_W, dtype=F32), rows)
    axis_dim = HEAD_DIM // 2
    inv_freq = ROPE_THETA ** (-jnp.arange(0, axis_dim, 2, dtype=F32) / axis_dim)
    ang = jnp.concatenate([t_row[:, None] * inv_freq, t_col[:, None] * inv_freq], axis=-1)
    cos = jnp.repeat(jnp.cos(ang), 2, axis=-1)
    sin = jnp.repeat(jnp.sin(ang), 2, axis=-1) * jnp.tile(jnp.array([-1.0, 1.0], F32), axis_dim)
    cos = jnp.concatenate([jnp.ones((ctx_len, HEAD_DIM), F32), cos], axis=0)
    sin = jnp.concatenate([jnp.zeros((ctx_len, HEAD_DIM), F32), sin], axis=0)
    return jnp.tile(cos, (1, 2)), jnp.tile(sin, (1, 2))


def _fill_padded(src_ref, pad_s, width, ctx_len, seq_len, pad, rows, fn):
    t = ctx_len + seq_len
    z = jnp.zeros((pad + SUBLANES, width), F32)
    pad_s[0:pad, :] = z[0:pad]
    pad_s[pad + ctx_len:2 * pad + ctx_len, :] = z[0:pad]
    pad_s[2 * pad + t:3 * pad + t + SUBLANES, :] = z

    def seg(tok0, off, ntiles):
        def body(n, carry):
            r = pl.multiple_of(tok0 + n * rows, rows)
            pad_s[pl.ds(r + off, rows), :] = fn(src_ref, r, rows)
            return carry
        lax.fori_loop(0, ntiles, body, 0)

    seg(0, pad, ctx_len // rows)
    seg(ctx_len, 2 * pad, seq_len // rows)


def _conv_segments(pad_s, w_ref, taps, ctx_len, seq_len, pad, rows, emit):
    half = taps // 2
    span = rows + SUBLANES

    def seg(tok0, off, ntiles):
        def body(n, carry):
            r = pl.multiple_of(tok0 + n * rows, rows)
            base = r + off - pad
            acc = None
            for rho in range(SUBLANES):
                part = None
                for j in range(taps):
                    dj = pad - half + j
                    if dj % SUBLANES != rho:
                        continue
                    term = w_ref[j:j + 1, :] * pad_s[pl.ds(pl.multiple_of(base + (dj - rho), SUBLANES), span), :]
                    part = term if part is None else part + term
                if part is None:
                    continue
                if rho:
                    part = pltpu.roll(part, span - rho, 0)
                acc = part[0:rows] if acc is None else acc + part[0:rows]
            emit(r, acc)
            return carry
        lax.fori_loop(0, ntiles, body, 0)

    seg(0, pad, ctx_len // rows)
    seg(ctx_len, 2 * pad, seq_len // rows)


CC_PAD = 16
CC_ROWS = 32


def _cconv_body(u_ref, w_ref, b_ref, g_ref, bb_ref, o_ref, pad_s, *, ctx_len, seq_len):
    c = o_ref.shape[2]

    def glu(src_ref, r, rows):
        return src_ref[0, pl.ds(r, rows), 0:c] * jax.nn.sigmoid(src_ref[0, pl.ds(r, rows), c:2 * c])

    _fill_padded(u_ref, pad_s, c, ctx_len, seq_len, CC_PAD, TM, glu)

    def emit(r, acc):
        h = acc + b_ref[...]
        mu = jnp.mean(h, axis=-1, keepdims=True)
        hc = h - mu
        var = jnp.mean(hc * hc, axis=-1, keepdims=True)
        y = hc * lax.rsqrt(var + EPS) * g_ref[...] + bb_ref[...]
        o_ref[0, pl.ds(r, CC_ROWS), :] = _silu(y).astype(o_ref.dtype)

    _conv_segments(pad_s, w_ref, w_ref.shape[0], ctx_len, seq_len, CC_PAD, CC_ROWS, emit)


def _conformer_conv(u, dw_w, dw_b, ln_g, ln_b, ctx_len):
    bsz, t, c2 = u.shape
    c = c2 // 2
    taps = dw_w.shape[0]
    return pl.pallas_call(
        functools.partial(_cconv_body, ctx_len=ctx_len, seq_len=t - ctx_len),
        grid=(bsz,),
        in_specs=[pl.BlockSpec((1, t, c2), lambda b: (b, 0, 0)),
                  _resident((taps, c)), _resident((1, c)), _resident((1, c)), _resident((1, c))],
        out_specs=pl.BlockSpec((1, t, c), lambda b: (b, 0, 0)),
        out_shape=jax.ShapeDtypeStruct((bsz, t, c), BF16),
        scratch_shapes=[pltpu.VMEM((t + 3 * CC_PAD + SUBLANES, c), F32)],
        compiler_params=_params("arbitrary"),
        name="conformer_conv",
    )(u, dw_w, dw_b.reshape(1, c), ln_g.reshape(1, c), ln_b.reshape(1, c))


SC_PAD = 8
SC_ROWS = 64
SC_COLS = 512


def _sconv_body(x_ref, w_ref, b_ref, o_ref, pad_s, *, ctx_len, seq_len):
    def ident(src_ref, r, rows):
        return src_ref[0, pl.ds(r, rows), :]

    _fill_padded(x_ref, pad_s, SC_COLS, ctx_len, seq_len, SC_PAD, TM, ident)

    def emit(r, acc):
        o_ref[0, pl.ds(r, SC_ROWS), :] = _silu(acc + b_ref[...]).astype(o_ref.dtype)

    _conv_segments(pad_s, w_ref, w_ref.shape[0], ctx_len, seq_len, SC_PAD, SC_ROWS, emit)


def _ssm_conv(xbc, conv_w, conv_b, ctx_len):
    bsz, t, c = xbc.shape
    taps = conv_w.shape[0]
    return pl.pallas_call(
        functools.partial(_sconv_body, ctx_len=ctx_len, seq_len=t - ctx_len),
        grid=(bsz, c // SC_COLS),
        in_specs=[pl.BlockSpec((1, t, SC_COLS), lambda b, j: (b, 0, j)),
                  pl.BlockSpec((taps, SC_COLS), lambda b, j: (0, j)),
                  pl.BlockSpec((1, SC_COLS), lambda b, j: (0, j))],
        out_specs=pl.BlockSpec((1, t, SC_COLS), lambda b, j: (b, 0, j)),
        out_shape=jax.ShapeDtypeStruct((bsz, t, c), BF16),
        scratch_shapes=[pltpu.VMEM((t + 3 * SC_PAD + SUBLANES, SC_COLS), F32)],
        compiler_params=_params("arbitrary", "arbitrary"),
        name="ssm_conv",
    )(xbc, conv_w, conv_b.reshape(1, c))


def _even_ffn_body(*refs, n_parts, n_ctx_tiles):
    x = _token_tile(refs[:n_parts], n_ctx_tiles)
    a_ref, c_ref, mod_ref, n2_ref, wo_ref, w1_ref, w3_ref, w2_ref, o_ref = refs[n_parts:]
    ca = a_ref.shape[2]
    mix = _dot(a_ref[0], wo_ref[0:ca, :]) + _dot(c_ref[0], wo_ref[ca:, :])
    x1 = x + mod_ref[0, 0, 2:3, :] * mix
    h = _rms(x1, n2_ref[...]) * (1.0 + mod_ref[0, 0, 4:5, :]) + mod_ref[0, 0, 3:4, :]
    hb = h.astype(BF16)
    u = _silu(_dot(hb, w1_ref[...])) * _dot(hb, w3_ref[...])
    o_ref[0] = x1 + mod_ref[0, 0, 5:6, :] * _dot(u.astype(BF16), w2_ref[...])


def _even_ffn(parts, attn, conv, mod, norm2, wo, w1, w3, w2, n_ctx_tiles):
    bsz, t, ca = attn.shape
    d = parts[0].shape[2]
    cc = conv.shape[2]
    return pl.pallas_call(
        functools.partial(_even_ffn_body, n_parts=len(parts), n_ctx_tiles=n_ctx_tiles),
        grid=(bsz, t // TM),
        in_specs=_token_specs(parts, n_ctx_tiles) + [
                  pl.BlockSpec((1, TM, ca), lambda b, i: (b, i, 0)),
                  pl.BlockSpec((1, TM, cc), lambda b, i: (b, i, 0)),
                  pl.BlockSpec((1, 1, 6, d), lambda b, i: (b, (i >= n_ctx_tiles).astype(jnp.int32), 0, 0)),
                  _resident((1, d)), _resident(wo.shape), _resident(w1.shape), _resident(w3.shape),
                  _resident(w2.shape)],
        out_specs=pl.BlockSpec((1, TM, d), lambda b, i: (b, i, 0)),
        out_shape=jax.ShapeDtypeStruct((bsz, t, d), F32),
        compiler_params=_params("arbitrary", "arbitrary"),
        name="even_ffn",
    )(*parts, attn, conv, mod, norm2.reshape(1, d), wo, w1, w3, w2)


def _softplus(x):
    return jnp.maximum(x, 0.0) + jnp.log(1.0 + jnp.exp(-jnp.abs(x)))


def _ssd_body(xs_ref, bm_ref, cm_ref, dt_ref, dtt_ref, alr_ref, alc_ref, bir_ref, bic_ref, exp_ref, sel_ref,
              *rest, rev, n_ctx_chunks, combine):
    if combine:
        yb_ref, dsk_ref, y_ref, st_s = rest
    else:
        y_ref, st_s = rest
    c = pl.program_id(1)
    nh = alr_ref.shape[1]
    hpg = nh // SSM_GROUPS
    gw = hpg * SSM_HEADDIM
    end = 0 if rev else CHUNK - 1

    @pl.when(c == 0)
    def _():
        st_s[...] = jnp.zeros_like(st_s)

    off = nh if rev else 0
    dt = _softplus(dt_ref[0, :, off:off + nh] + bir_ref[...])
    dtt = _softplus(dtt_ref[0, off:off + nh, :] + bic_ref[...])
    a = dt * (-jnp.exp(alr_ref[...]))
    at = dtt * (-jnp.exp(alc_ref[...]))

    row = lax.broadcasted_iota(jnp.int32, (CHUNK, CHUNK), 0)
    col = lax.broadcasted_iota(jnp.int32, (CHUNK, CHUNK), 1)
    tri = (col >= row) if rev else (col <= row)
    trit = (row >= col) if rev else (row <= col)
    tri_b = tri.astype(BF16)
    trit_b = trit.astype(BF16)
    acs = sum(_dot(tri_b, p) for p in _split3(a))
    acst = sum(_dot(p, trit_b) for p in _split3(at))
    tot = acs[end:end + 1, :]
    w_end = dt * jnp.exp(tot - acs)
    eacs = jnp.exp(acs)
    cdec = jnp.exp(tot)

    lane = lax.broadcasted_iota(jnp.int32, (1, LANES), 1)
    low = lane < SSM_HEADDIM

    def cat3(v):
        return jnp.concatenate(_split3(v), axis=1)

    def step(with_y):
        pieces = [w_end, jnp.broadcast_to(cdec, (SUBLANES, nh))]
        if with_y:
            pieces = [dt, eacs] + pieces
        spread = _dot(cat3(jnp.concatenate(pieces, axis=0)), exp_ref[...])
        r0 = 2 * CHUNK if with_y else 0
        wex = spread[r0:r0 + CHUNK]
        dec = spread[r0 + CHUNK:r0 + CHUNK + 1]
        xs = xs_ref[0].astype(F32)
        xcd = (xs * wex).astype(BF16)
        if with_y:
            xc = (xs * spread[0:CHUNK]).astype(BF16)
            eax = spread[CHUNK:2 * CHUNK]
            acs_l = _dot(cat3(acs), sel_ref[...])
        for g in range(SSM_GROUPS):
            gs = slice(g * gw, (g + 1) * gw)
            bm = bm_ref[0, :, g * D_STATE:(g + 1) * D_STATE]
            bmt = bm.astype(F32).T.astype(BF16)
            sg = st_s[g]
            st_s[g] = sg * dec[:, gs] + _dot(bmt, xcd[:, gs])

            if not with_y:
                continue
            cmb = cm_ref[0, :, g * D_STATE:(g + 1) * D_STATE].astype(BF16)
            cb = _dot(cmb, bmt)
            yoff = _dot(cmb, sg.astype(BF16))
            for kp in range(hpg // 2):
                e0 = g * hpg + 2 * kp
                c0 = e0 * SSM_HEADDIM
                xcb = xc[:, c0:c0 + LANES]
                res = []
                for e in (e0, e0 + 1):
                    diff = acs_l[:, e * LANES:(e + 1) * LANES] - acst[e:e + 1, :]
                    m = (jnp.exp(jnp.where(tri, diff, -jnp.inf)) * cb).astype(BF16)
                    res.append(_dot(m, xcb))
                y = jnp.where(low, res[0], res[1]) + yoff[:, kp * LANES:(kp + 1) * LANES] * eax[:, c0:c0 + LANES]
                if combine:
                    y = (y + yb_ref[0, :, c0:c0 + LANES].astype(F32)
                         + dsk_ref[:, c0:c0 + LANES] * xs[:, c0:c0 + LANES])
                y_ref[0, :, c0:c0 + LANES] = y.astype(y_ref.dtype)

    @pl.when(c >= n_ctx_chunks)
    def _():
        step(True)

    @pl.when(c < n_ctx_chunks)
    def _():
        step(False)


def _ssd_scan(xbc, dt, dtt, a_log, dt_bias, ctx_len, rev, y_other=None, d_skip=None):
    bsz, t, _ = xbc.shape
    nh = a_log.shape[0]
    d_inner = nh * SSM_HEADDIM
    gn = SSM_GROUPS * D_STATE
    nc = t // CHUNK
    ncc = ctx_len // CHUNK
    seq_len = t - ctx_len
    combine = y_other is not None

    if rev:
        def chunk(i):
            return jnp.where(i < ncc, ncc - 1 - i, nc - 1 + ncc - i)
    else:
        def chunk(i):
            return i

    def ychunk(i):
        return jnp.maximum(chunk(i), ncc) - ncc if not rev else jnp.where(i < ncc, nc - 1 - ncc, chunk(i) - ncc)

    bcol = d_inner // gn
    in_specs = [pl.BlockSpec((1, CHUNK, d_inner), lambda b, i: (b, chunk(i), 0)),
                pl.BlockSpec((1, CHUNK, gn), lambda b, i: (b, chunk(i), bcol)),
                pl.BlockSpec((1, CHUNK, gn), lambda b, i: (b, chunk(i), bcol + 1)),
                pl.BlockSpec((1, CHUNK, 2 * nh), lambda b, i: (b, chunk(i), 0)),
                pl.BlockSpec((1, 2 * nh, CHUNK), lambda b, i: (b, 0, chunk(i))),
                _resident((1, nh)), _resident((nh, 1)), _resident((1, nh)), _resident((nh, 1)),
                _resident((3 * nh, d_inner)), _resident((3 * nh, nh * LANES))]
    heads = jnp.arange(nh, dtype=jnp.int32)[:, None]
    spread_p = jnp.tile((jnp.arange(d_inner, dtype=jnp.int32)[None, :] // SSM_HEADDIM == heads).astype(BF16), (3, 1))
    spread_l = jnp.tile((jnp.arange(nh * LANES, dtype=jnp.int32)[None, :] // LANES == heads).astype(BF16), (3, 1))
    args = [xbc, xbc, xbc, dt, dtt, a_log.reshape(1, nh), a_log.reshape(nh, 1),
            dt_bias.reshape(1, nh), dt_bias.reshape(nh, 1), spread_p, spread_l]
    if combine:
        in_specs += [pl.BlockSpec((1, CHUNK, d_inner), lambda b, i: (b, ychunk(i), 0)), _resident((1, d_inner))]
        args += [y_other, jnp.repeat(d_skip, SSM_HEADDIM).reshape(1, d_inner)]
    return pl.pallas_call(
        functools.partial(_ssd_body, rev=rev, n_ctx_chunks=ncc, combine=combine),
        grid=(bsz, nc),
        in_specs=in_specs,
        out_specs=pl.BlockSpec((1, CHUNK, d_inner), lambda b, i: (b, ychunk(i), 0)),
        out_shape=jax.ShapeDtypeStruct((bsz, seq_len, d_inner), BF16),
        scratch_shapes=[pltpu.VMEM((SSM_GROUPS, D_STATE, d_inner // SSM_GROUPS), F32)],
        compiler_params=_params("arbitrary", "arbitrary"),
        name="ssd_bwd" if rev else "ssd_fwd",
    )(*args)


def _odd_out_body(x_ref, y_ref, z_ref, mod_ref, gn_ref, n2_ref, wout_ref, wr_ref, x1_ref, h_ref, ti_ref, tw_ref):
    z = z_ref[0].astype(F32)
    yn = _rms(y_ref[0].astype(F32) * _silu(z), gn_ref[...])
    x1 = x_ref[0] + mod_ref[0, 0, 2:3, :] * _dot(yn.astype(BF16), wout_ref[...])
    x1_ref[0] = x1
    h = _rms(x1, n2_ref[...]) * (1.0 + mod_ref[0, 0, 4:5, :]) + mod_ref[0, 0, 3:4, :]
    h_ref[0] = h
    ne = wr_ref.shape[1]
    h1, h2, _ = _split3(h)
    w1, w2, _ = _split3(wr_ref[...])
    hw = _dot(h1, jnp.concatenate([w1, w2], axis=1))
    lg = hw[:, 0:ne] + hw[:, ne:2 * ne] + _dot(h2, w1)
    idx = lax.broadcasted_iota(jnp.int32, lg.shape, 1)
    m1 = jnp.max(lg, axis=-1, keepdims=True)
    i1 = jnp.min(jnp.where(lg == m1, idx, ne), axis=-1, keepdims=True)
    lg2 = jnp.where(idx == i1, -jnp.inf, lg)
    m2 = jnp.max(lg2, axis=-1, keepdims=True)
    i2 = jnp.min(jnp.where(lg2 == m2, idx, ne), axis=-1, keepdims=True)
    e2 = jnp.exp(m2 - m1)
    den = 1.0 + e2
    ti_ref[0] = jnp.concatenate([i1, i2], axis=1)
    tw_ref[0] = jnp.concatenate([1.0 / den, e2 / den], axis=1)


def _odd_out(x, y, z, mod, gnorm, norm2, w_out, w_router, ctx_len):
    bsz, t, d = x.shape
    seq_len = t - ctx_len
    di = y.shape[2]
    ne = w_router.shape[1]
    nct = ctx_len // TM
    lat = lambda b, i: (b, i + nct, 0)
    own = lambda b, i: (b, i, 0)
    return pl.pallas_call(
        _odd_out_body,
        grid=(bsz, seq_len // TM),
        in_specs=[pl.BlockSpec((1, TM, d), lat),
                  pl.BlockSpec((1, TM, di), own),
                  pl.BlockSpec((1, TM, di), lat),
                  pl.BlockSpec((1, 1, 6, d), lambda b, i: (b, 1, 0, 0)),
                  _resident((1, di)), _resident((1, d)), _resident(w_out.shape), _resident((d, ne))],
        out_specs=[pl.BlockSpec((1, TM, d), own), pl.BlockSpec((1, TM, d), own),
                   pl.BlockSpec((1, TM, TOP_K), own), pl.BlockSpec((1, TM, TOP_K), own)],
        out_shape=[jax.ShapeDtypeStruct((bsz, seq_len, d), F32), jax.ShapeDtypeStruct((bsz, seq_len, d), F32),
                   jax.ShapeDtypeStruct((bsz, seq_len, TOP_K), jnp.int32),
                   jax.ShapeDtypeStruct((bsz, seq_len, TOP_K), F32)],
        compiler_params=_params("arbitrary", "arbitrary"),
        name="odd_out",
    )(x, y, z, mod, gnorm.reshape(1, di), norm2.reshape(1, d), w_out, w_router)


def _start_row(idx_ref, j, src_hbm, dst, sem, priority):
    pltpu.make_async_copy(src_hbm.at[pl.ds(idx_ref[0, 0, j], 1), :], dst.at[pl.ds(j, 1), :], sem).start(
        priority=priority)


def _start_rows(idx_ref, src_hbm, dst, sem, n):
    def body(j, carry):
        _start_row(idx_ref, 2 * j, src_hbm, dst, sem, 0)
        _start_row(idx_ref, 2 * j + 1, src_hbm, dst, sem, 1)
        return carry
    lax.fori_loop(0, n // 2, body, 0)


def _wait_rows(src_hbm, dst, sem, n):
    pltpu.make_async_copy(src_hbm.at[pl.ds(0, n), :], dst, sem).wait()


def _expert_body(te_ref, na_ref, rows_ref, next_rows_ref, h_hbm, w1_ref, w3_ref, w2_ref, y_ref,
                 xf_s, xb_s, acc_s, sem, *, nf):
    ti = pl.program_id(0)
    f = pl.program_id(1)
    n_active = na_ref[0]
    active = ti < n_active
    has_next = ti + 1 < n_active
    slot = ti % 2
    part = TM_E // nf

    @pl.when(jnp.logical_and(ti == 0, f == 0))
    def _():
        _start_rows(rows_ref, h_hbm, xf_s.at[0], sem.at[0], TM_E)

    @pl.when(jnp.logical_and(active, f == 0))
    def _():
        _wait_rows(h_hbm, xf_s.at[slot], sem.at[slot], TM_E)
        xb_s[...] = xf_s[slot].astype(BF16)
        acc_s[...] = jnp.zeros_like(acc_s)

    def compute():
        xb = xb_s[...]
        u = _silu(_dot(xb, w1_ref[0])) * _dot(xb, w3_ref[0])
        acc_s[...] += _dot(u.astype(BF16), w2_ref[0])

    @pl.when(has_next)
    def _():
        base = f * part
        for j in range(part):
            _start_row(next_rows_ref, base + j, h_hbm, xf_s.at[1 - slot], sem.at[1 - slot], 1)
        compute()

    @pl.when(jnp.logical_and(active, jnp.logical_not(has_next)))
    def _():
        compute()

    last = f == nf - 1

    @pl.when(jnp.logical_and(active, last))
    def _():
        y_ref[...] = acc_s[...]

    @pl.when(jnp.logical_and(jnp.logical_not(active), last))
    def _():
        y_ref[...] = jnp.zeros_like(y_ref)


def _expert_ffn(h, tile_expert, n_active, row_token, w1, w3, w2):
    n, d = h.shape
    ns = row_token.shape[0]
    nt = ns // TM_E
    dff = w1.shape[2]
    nf = dff // TF_E

    def wcol(ti, f, te, na):
        return (te[ti], 0, jnp.where(ti < na[0], f, nf - 1))

    def wrow(ti, f, te, na):
        return (te[ti], jnp.where(ti < na[0], f, nf - 1), 0)

    grid_spec = pltpu.PrefetchScalarGridSpec(
        num_scalar_prefetch=2,
        grid=(nt, nf),
        in_specs=[pl.BlockSpec((1, 1, TM_E), lambda ti, f, te, na: (ti, 0, 0), memory_space=pltpu.SMEM),
                  pl.BlockSpec((1, 1, TM_E), lambda ti, f, te, na: (jnp.minimum(ti + 1, nt - 1), 0, 0),
                               memory_space=pltpu.SMEM),
                  pl.BlockSpec(memory_space=pl.ANY),
                  pl.BlockSpec((1, d, TF_E), wcol),
                  pl.BlockSpec((1, d, TF_E), wcol),
                  pl.BlockSpec((1, TF_E, d), wrow)],
        out_specs=pl.BlockSpec((TM_E, d), lambda ti, f, te, na: (ti, 0)),
        scratch_shapes=[pltpu.VMEM((2, TM_E, d), F32), pltpu.VMEM((TM_E, d), BF16), pltpu.VMEM((TM_E, d), F32),
                        pltpu.SemaphoreType.DMA((2,))],
    )
    rows = row_token.reshape(nt, 1, TM_E)
    return pl.pallas_call(
        functools.partial(_expert_body, nf=nf),
        grid_spec=grid_spec,
        out_shape=jax.ShapeDtypeStruct((ns, d), F32),
        compiler_params=_params("arbitrary", "arbitrary"),
        name="expert_ffn",
    )(tile_expert, n_active, rows, rows, h, w1, w3, w2)


def _combine_body(pos_ref, next_pos_ref, x_ref, tw_ref, g2_ref, fn_ref, y_hbm, o_ref, y_s, sem):
    i = pl.program_id(0)
    slot = i % 2
    n = TOP_K * TM_C

    @pl.when(i == 0)
    def _():
        _start_rows(pos_ref, y_hbm, y_s.at[0], sem.at[0], n)

    @pl.when(i + 1 < pl.num_programs(0))
    def _():
        _start_rows(next_pos_ref, y_hbm, y_s.at[1 - slot], sem.at[1 - slot], n)

    _wait_rows(y_hbm, y_s.at[slot], sem.at[slot], n)
    tw = tw_ref[...]
    moe = tw[:, 0:1] * y_s[slot, 0:TM_C, :] + tw[:, 1:2] * y_s[slot, TM_C:2 * TM_C, :]
    o_ref[...] = _rms(x_ref[...] + g2_ref[0] * moe, fn_ref[...])


def _moe_combine(x1, topw, pos, g2, final_norm, y_sorted, tiles_per_batch):
    n, d = x1.shape
    nt = n // TM_C
    pos_t = pos.reshape(nt, TM_C, TOP_K).transpose(0, 2, 1).reshape(nt, 1, TOP_K * TM_C)
    return pl.pallas_call(
        _combine_body,
        grid=(nt,),
        in_specs=[pl.BlockSpec((1, 1, TOP_K * TM_C), lambda i: (i, 0, 0), memory_space=pltpu.SMEM),
                  pl.BlockSpec((1, 1, TOP_K * TM_C), lambda i: (jnp.minimum(i + 1, nt - 1), 0, 0),
                               memory_space=pltpu.SMEM),
                  pl.BlockSpec((TM_C, d), lambda i: (i, 0)),
                  pl.BlockSpec((TM_C, TOP_K), lambda i: (i, 0)),
                  pl.BlockSpec((1, 1, d), lambda i: (i // tiles_per_batch, 0, 0)),
                  _resident((1, d)),
                  pl.BlockSpec(memory_space=pl.ANY)],
        out_specs=pl.BlockSpec((TM_C, d), lambda i: (i, 0)),
        out_shape=jax.ShapeDtypeStruct((n, d), F32),
        scratch_shapes=[pltpu.VMEM((2, TOP_K * TM_C, d), F32), pltpu.SemaphoreType.DMA((2,))],
        compiler_params=_params("arbitrary"),
        name="moe_combine",
    )(pos_t, pos_t, x1, topw, g2, final_norm.reshape(1, d), y_sorted)


def _route(topi, n_slots):
    n = topi.shape[0]
    e_flat = topi.reshape(n * TOP_K)
    oh = (e_flat[:, None] == jnp.arange(N_EXPERTS, dtype=jnp.int32)[None, :]).astype(jnp.int32)
    csum = jnp.cumsum(oh, axis=0)
    rank = jnp.sum(csum * oh, axis=1) - 1
    cnt = csum[-1]
    cnt_pad = (cnt + TM_E - 1) // TM_E * TM_E
    ends = jnp.cumsum(cnt_pad)
    offs = ends - cnt_pad
    slot = jnp.sum(oh * offs[None, :], axis=1) + rank
    n_active = (ends[-1] // TM_E).astype(jnp.int32)
    nt = n_slots // TM_E
    tile_start = jnp.arange(nt, dtype=jnp.int32) * TM_E
    te = jnp.sum((tile_start[:, None] >= ends[None, :]).astype(jnp.int32), axis=1)
    te = jnp.minimum(te, N_EXPERTS - 1)
    te = jnp.where(jnp.arange(nt) < n_active, te, te[jnp.maximum(n_active - 1, 0)])
    row_token = jnp.zeros((n_slots,), jnp.int32).at[slot].set(jnp.arange(n * TOP_K, dtype=jnp.int32) // TOP_K)
    return slot.reshape(n, TOP_K), te.astype(jnp.int32), n_active.reshape(1), row_token


def kernel(x, c, ctx, c_ctx, ev_ada_w, ev_ada_b, ev_norm1, ev_norm2, ev_w_in, ev_q_gain, ev_k_gain, ev_dw_w, ev_dw_b, ev_ln_g, ev_ln_b, ev_w_o, ev_ff_w1, ev_ff_w3, ev_ff_w2, od_ada_w, od_ada_b, od_norm1, od_norm2, od_w_in, od_conv_w, od_conv_b, od_a_log_f, od_a_log_b, od_dt_bias_f, od_dt_bias_b, od_d_skip, od_gnorm, od_w_out, od_router, od_ex_w1, od_ex_w3, od_ex_w2, final_norm):
    bsz, seq_len, d = x.shape
    ctx_len = ctx.shape[1]
    assert ev_ada_w.shape[0] == 1 and od_ada_w.shape[0] == 1, "one even and one odd layer"
    assert ctx_len % TM == 0 and seq_len % TM == 0 and seq_len % GRID_W == 0
    n_ctx_tiles = ctx_len // TM
    mod = _mod_table(c, c_ctx, ev_ada_w[0], ev_ada_b[0])
    cc = ev_dw_w.shape[2]
    splits = [(0, ATTN_W), (ATTN_W, ATTN_W + KV_W), (ATTN_W + KV_W, ATTN_W + 2 * KV_W),
              (ATTN_W + 2 * KV_W, ATTN_W + 2 * KV_W + 2 * cc)]
    q, k, v, u = _inproj((ctx, x), mod, ev_norm1[0], ev_w_in[0].astype(BF16), splits, (F32, F32, BF16, F32),
                         n_ctx_tiles)
    cos2, sin2 = _rope_tables(ctx_len, seq_len)
    attn = _attention(q, k, v, cos2, sin2, ev_q_gain[0], ev_k_gain[0], ctx_len)
    conv = _conformer_conv(u, ev_dw_w[0], ev_dw_b[0], ev_ln_g[0], ev_ln_b[0], ctx_len)
    xa = _even_ffn((ctx, x), attn, conv, mod, ev_norm2[0], ev_w_o[0].astype(BF16), ev_ff_w1[0].astype(BF16),
                   ev_ff_w3[0].astype(BF16), ev_ff_w2[0].astype(BF16), n_ctx_tiles)

    mod = _mod_table(c, c_ctx, od_ada_w[0], od_ada_b[0])
    nh = od_a_log_f.shape[1]
    d_inner = nh * SSM_HEADDIM
    conv_dim = od_conv_w.shape[2]
    splits = [(0, d_inner), (d_inner, d_inner + conv_dim), (d_inner + conv_dim, d_inner + conv_dim + 2 * nh)]
    z, xbc, dt = _inproj((xa,), mod, od_norm1[0], od_w_in[0].astype(BF16), splits, (BF16, F32, F32), n_ctx_tiles)
    xbc = _ssm_conv(xbc, od_conv_w[0], od_conv_b[0], ctx_len)
    dtt = dt.transpose(0, 2, 1)
    y_b = _ssd_scan(xbc, dt, dtt, od_a_log_b[0], od_dt_bias_b[0], ctx_len, rev=True)
    y = _ssd_scan(xbc, dt, dtt, od_a_log_f[0], od_dt_bias_f[0], ctx_len, rev=False, y_other=y_b,
                  d_skip=od_d_skip[0])
    x1, h, topi, topw = _odd_out(xa, y, z, mod, od_gnorm[0], od_norm2[0], od_w_out[0].astype(BF16),
                                 od_router[0], ctx_len)

    n = bsz * seq_len
    n_slots = n * TOP_K + N_EXPERTS * TM_E
    pos, tile_expert, n_active, row_token = _route(topi.reshape(n, TOP_K), n_slots)
    y_sorted = _expert_ffn(h.reshape(n, d), tile_expert, n_active, row_token, od_ex_w1[0].astype(BF16),
                           od_ex_w3[0].astype(BF16), od_ex_w2[0].astype(BF16))
    g2 = mod[:, 1, 5:6, :]
    out = _moe_combine(x1.reshape(n, d), topw.reshape(n, TOP_K), pos, g2, final_norm, y_sorted, seq_len // TM_C)
    return out.reshape(bsz, seq_len, d)
```

```python
import functools

import jax
import jax.numpy as jnp
from jax import lax
from jax.experimental import pallas as pl
from jax.experimental.pallas import tpu as pltpu

F32 = jnp.float32
BF16 = jnp.bfloat16
EPS = 1e-6
LOG2_E = 1.4426950408889634

GRID_W = 64
HEAD_DIM = 64
ATTN_HEADS = 8
KV_HEADS = 2
ATTN_W = ATTN_HEADS * HEAD_DIM
KV_W = KV_HEADS * HEAD_DIM
ROPE_THETA = 10000.0
SSM_HEADDIM = 64
SSM_GROUPS = 4
D_STATE = 128
CHUNK = 128
N_EXPERTS = 8
TOP_K = 2

LANES = 128
SUBLANES = 8
VMEM_LIMIT = 56 * 1024 * 1024

TM = 256
TM_E = 512
TF_E = 1792
TM_C = 256
SSD_CPS = 2


def _params(*sem):
    return pltpu.CompilerParams(dimension_semantics=sem, vmem_limit_bytes=VMEM_LIMIT)


def _resident(shape):
    nd = len(shape)
    return pl.BlockSpec(shape, lambda *_: (0,) * nd, pipeline_mode=pl.Buffered(1))


def _silu(x):
    return x * jax.nn.sigmoid(x)


def _rms(x, g):
    return x * lax.rsqrt(jnp.mean(x * x, axis=-1, keepdims=True) + EPS) * g


def _dot(a, b):
    return jnp.dot(a, b, preferred_element_type=F32)


def _split3(a):
    a1 = a.astype(BF16)
    r1 = a - a1.astype(F32)
    a2 = r1.astype(BF16)
    a3 = (r1 - a2.astype(F32)).astype(BF16)
    return a1, a2, a3


def _ada_body(c_ref, w_ref, b_ref, o_ref):
    s = _silu(c_ref[...]).astype(BF16)
    o_ref[...] = _dot(s, w_ref[...].astype(BF16)) + b_ref[...]


def _adaln(cond, w, b):
    r, d = cond.shape
    n = w.shape[1]
    tn = n // 4
    return pl.pallas_call(
        _ada_body,
        grid=(n // tn,),
        in_specs=[pl.BlockSpec((r, d), lambda j: (0, 0)),
                  pl.BlockSpec((d, tn), lambda j: (0, j)),
                  pl.BlockSpec((1, tn), lambda j: (0, j))],
        out_specs=pl.BlockSpec((r, tn), lambda j: (0, j)),
        out_shape=jax.ShapeDtypeStruct((r, n), F32),
        compiler_params=_params("arbitrary"),
        name="adaln",
    )(cond, w, b.reshape(1, n))


def _mod_table(c, c_ctx, w, b):
    bsz, d = c.shape
    rows = -(-(bsz + 1) // SUBLANES) * SUBLANES
    cond = jnp.zeros((rows, d), F32).at[:bsz].set(c).at[bsz].set(c_ctx)
    m = _adaln(cond, w, b)
    lat = m[:bsz]
    ctx = jnp.broadcast_to(m[bsz][None], lat.shape)
    return jnp.stack([ctx, lat], axis=1).reshape(bsz, 2, 6, d)


def _token_specs(parts, n_ctx_tiles):
    d = parts[0].shape[2]
    if len(parts) == 1:
        return [pl.BlockSpec((1, TM, d), lambda b, i: (b, i, 0))]
    return [pl.BlockSpec((1, TM, d), lambda b, i: (b, jnp.minimum(i, n_ctx_tiles - 1), 0)),
            pl.BlockSpec((1, TM, d), lambda b, i: (b, jnp.maximum(i - n_ctx_tiles, 0), 0))]


def _token_tile(refs, n_ctx_tiles):
    if len(refs) == 1:
        return refs[0][0]
    return jnp.where(pl.program_id(1) < n_ctx_tiles, refs[0][0], refs[1][0])


def _inproj_body(*refs, splits, n_parts, n_ctx_tiles):
    x = _token_tile(refs[:n_parts], n_ctx_tiles)
    mod_ref, n_ref, w_ref = refs[n_parts:n_parts + 3]
    out_refs = refs[n_parts + 3:]
    h = _rms(x, n_ref[...]) * (1.0 + mod_ref[0, 0, 1:2, :]) + mod_ref[0, 0, 0:1, :]
    r = _dot(h.astype(BF16), w_ref[...])
    for o_ref, (lo, hi) in zip(out_refs, splits):
        o_ref[0] = r[:, lo:hi].astype(o_ref.dtype)


def _inproj(parts, mod, norm, w_bf16, splits, dtypes, n_ctx_tiles):
    bsz, _, d = parts[0].shape
    t = sum(p.shape[1] for p in parts)
    n = w_bf16.shape[1]
    widths = [hi - lo for lo, hi in splits]
    return pl.pallas_call(
        functools.partial(_inproj_body, splits=tuple(splits), n_parts=len(parts), n_ctx_tiles=n_ctx_tiles),
        grid=(bsz, t // TM),
        in_specs=_token_specs(parts, n_ctx_tiles) + [
            pl.BlockSpec((1, 1, 6, d), lambda b, i: (b, (i >= n_ctx_tiles).astype(jnp.int32), 0, 0)),
            _resident((1, d)),
            _resident((d, n))],
        out_specs=[pl.BlockSpec((1, TM, wd), lambda b, i: (b, i, 0)) for wd in widths],
        out_shape=[jax.ShapeDtypeStruct((bsz, t, wd), dt) for wd, dt in zip(widths, dtypes)],
        compiler_params=_params("arbitrary", "arbitrary"),
        name="inproj",
    )(*parts, mod, norm.reshape(1, d), w_bf16)


def _attn_body(q_ref, k_ref, v_ref, cq_ref, sq_ref, ck_ref, sk_ref, qg_ref, kg_ref, o_ref, kt_s, v2_s,
               *, n_ctx_tiles, ctx_len):
    i = pl.program_id(1)
    t = k_ref.shape[1]
    lane = lax.broadcasted_iota(jnp.int32, (1, LANES), 1)
    low = lane < HEAD_DIM
    even = (lane % 2) == 0

    def norm_rope(x, gain, cos, sin):
        x2 = x * x
        s_lo = jnp.sum(jnp.where(low, x2, 0.0), axis=-1, keepdims=True)
        s_hi = jnp.sum(jnp.where(low, 0.0, x2), axis=-1, keepdims=True)
        ms = jnp.where(low, s_lo, s_hi) * (1.0 / HEAD_DIM)
        xn = x * lax.rsqrt(ms + EPS) * gain
        swapped = jnp.where(even, pltpu.roll(xn, LANES - 1, 1), pltpu.roll(xn, 1, 1))
        return xn * cos + swapped * sin

    @pl.when(i == 0)
    def _():
        k = norm_rope(k_ref[0], kg_ref[...], ck_ref[...], sk_ref[...])
        kr = pltpu.roll(k, HEAD_DIM, 1)
        kt_s[0] = jnp.where(low, k, kr).T.astype(BF16)
        kt_s[1] = jnp.where(low, kr, k).T.astype(BF16)
        v = v_ref[0].astype(F32)
        vr = pltpu.roll(v, HEAD_DIM, 1)
        v2_s[0] = jnp.where(low, v, vr).astype(BF16)
        v2_s[1] = jnp.where(low, vr, v).astype(BF16)

    def run(tk):
        for j in range(ATTN_HEADS // 2):
            g = (2 * j) // (ATTN_HEADS // KV_HEADS)
            qp = norm_rope(q_ref[0, :, j * LANES:(j + 1) * LANES], qg_ref[...], cq_ref[...], sq_ref[...])
            qp = qp * (HEAD_DIM ** -0.5 * LOG2_E)
            outs = []
            for hh in range(2):
                qm = jnp.where(low if hh == 0 else jnp.logical_not(low), qp, 0.0).astype(BF16)
                s = _dot(qm, kt_s[g, :, 0:tk])
                m = jnp.max(s, axis=-1, keepdims=True)
                p = jnp.exp2(s - m)
                l = jnp.sum(p, axis=-1, keepdims=True)
                o = _dot(p.astype(BF16), v2_s[g, 0:tk, :])
                outs.append(o / l)
            o_ref[0, :, j * LANES:(j + 1) * LANES] = jnp.where(low, outs[0], outs[1]).astype(o_ref.dtype)

    @pl.when(i < n_ctx_tiles)
    def _():
        run(ctx_len)

    @pl.when(i >= n_ctx_tiles)
    def _():
        run(t)


def _attention(q, k, v, cos2, sin2, q_gain, k_gain, ctx_len):
    bsz, t, _ = q.shape
    n_ctx_tiles = ctx_len // TM
    qg = jnp.tile(q_gain.reshape(1, HEAD_DIM), (1, 2))
    kg = jnp.tile(k_gain.reshape(1, HEAD_DIM), (1, 2))
    return pl.pallas_call(
        functools.partial(_attn_body, n_ctx_tiles=n_ctx_tiles, ctx_len=ctx_len),
        grid=(bsz, t // TM),
        in_specs=[pl.BlockSpec((1, TM, ATTN_W), lambda b, i: (b, i, 0)),
                  pl.BlockSpec((1, t, KV_W), lambda b, i: (b, 0, 0)),
                  pl.BlockSpec((1, t, KV_W), lambda b, i: (b, 0, 0)),
                  pl.BlockSpec((TM, LANES), lambda b, i: (i, 0)),
                  pl.BlockSpec((TM, LANES), lambda b, i: (i, 0)),
                  _resident((t, LANES)),
                  _resident((t, LANES)),
                  _resident((1, LANES)),
                  _resident((1, LANES))],
        out_specs=pl.BlockSpec((1, TM, ATTN_W), lambda b, i: (b, i, 0)),
        out_shape=jax.ShapeDtypeStruct((bsz, t, ATTN_W), BF16),
        scratch_shapes=[pltpu.VMEM((KV_HEADS, LANES, t), BF16), pltpu.VMEM((KV_HEADS, t, LANES), BF16)],
        compiler_params=_params("arbitrary", "arbitrary"),
        name="attention",
    )(q, k, v, cos2, sin2, cos2, sin2, qg, kg)


def _rope_tables(ctx_len, seq_len):
    rows = seq_len // GRID_W
    t_row = jnp.repeat(jnp.arange(rows, dtype=F32), GRID_W)
    t_col = jnp.tile(jnp.arange(GRID_W, dtype=F32), rows)
    axis_dim = HEAD_DIM // 2
    inv_freq = ROPE_THETA ** (-jnp.arange(0, axis_dim, 2, dtype=F32) / axis_dim)
    ang = jnp.concatenate([t_row[:, None] * inv_freq, t_col[:, None] * inv_freq], axis=-1)
    cos = jnp.repeat(jnp.cos(ang), 2, axis=-1)
    sin = jnp.repeat(jnp.sin(ang), 2, axis=-1) * jnp.tile(jnp.array([-1.0, 1.0], F32), axis_dim)
    cos = jnp.concatenate([jnp.ones((ctx_len, HEAD_DIM), F32), cos], axis=0)
    sin = jnp.concatenate([jnp.zeros((ctx_len, HEAD_DIM), F32), sin], axis=0)
    return jnp.tile(cos, (1, 2)), jnp.tile(sin, (1, 2))


def _fill_padded(src_ref, pad_s, width, ctx_len, seq_len, pad, rows, fn):
    t = ctx_len + seq_len
    z = jnp.zeros((pad + SUBLANES, width), F32)
    pad_s[0:pad, :] = z[0:pad]
    pad_s[pad + ctx_len:2 * pad + ctx_len, :] = z[0:pad]
    pad_s[2 * pad + t:3 * pad + t + SUBLANES, :] = z

    def seg(tok0, off, ntiles):
        def body(n, carry):
            r = pl.multiple_of(tok0 + n * rows, rows)
            pad_s[pl.ds(r + off, rows), :] = fn(src_ref, r, rows)
            return carry
        lax.fori_loop(0, ntiles, body, 0)

    seg(0, pad, ctx_len // rows)
    seg(ctx_len, 2 * pad, seq_len // rows)


def _conv_segments(pad_s, w_ref, taps, ctx_len, seq_len, pad, rows, emit):
    half = taps // 2
    span = rows + SUBLANES

    def seg(tok0, off, ntiles):
        def body(n, carry):
            r = pl.multiple_of(tok0 + n * rows, rows)
            base = r + off - pad
            acc = None
            for rho in range(SUBLANES):
                part = None
                for j in range(taps):
                    dj = pad - half + j
                    if dj % SUBLANES != rho:
                        continue
                    term = w_ref[j:j + 1, :] * pad_s[pl.ds(pl.multiple_of(base + (dj - rho), SUBLANES), span), :]
                    part = term if part is None else part + term
                if part is None:
                    continue
                if rho:
                    part = pltpu.roll(part, span - rho, 0)
                acc = part[0:rows] if acc is None else acc + part[0:rows]
            emit(r, acc)
            return carry
        lax.fori_loop(0, ntiles, body, 0)

    seg(0, pad, ctx_len // rows)
    seg(ctx_len, 2 * pad, seq_len // rows)


CC_PAD = 16
CC_ROWS = 32


def _cconv_body(u_ref, w_ref, b_ref, g_ref, bb_ref, o_ref, pad_s, *, ctx_len, seq_len):
    c = o_ref.shape[2]

    def glu(src_ref, r, rows):
        return src_ref[0, pl.ds(r, rows), 0:c] * jax.nn.sigmoid(src_ref[0, pl.ds(r, rows), c:2 * c])

    _fill_padded(u_ref, pad_s, c, ctx_len, seq_len, CC_PAD, TM, glu)

    def emit(r, acc):
        h = acc + b_ref[...]
        mu = jnp.mean(h, axis=-1, keepdims=True)
        hc = h - mu
        var = jnp.mean(hc * hc, axis=-1, keepdims=True)
        y = hc * lax.rsqrt(var + EPS) * g_ref[...] + bb_ref[...]
        o_ref[0, pl.ds(r, CC_ROWS), :] = _silu(y).astype(o_ref.dtype)

    _conv_segments(pad_s, w_ref, w_ref.shape[0], ctx_len, seq_len, CC_PAD, CC_ROWS, emit)


def _conformer_conv(u, dw_w, dw_b, ln_g, ln_b, ctx_len):
    bsz, t, c2 = u.shape
    c = c2 // 2
    taps = dw_w.shape[0]
    return pl.pallas_call(
        functools.partial(_cconv_body, ctx_len=ctx_len, seq_len=t - ctx_len),
        grid=(bsz,),
        in_specs=[pl.BlockSpec((1, t, c2), lambda b: (b, 0, 0)),
                  _resident((taps, c)), _resident((1, c)), _resident((1, c)), _resident((1, c))],
        out_specs=pl.BlockSpec((1, t, c), lambda b: (b, 0, 0)),
        out_shape=jax.ShapeDtypeStruct((bsz, t, c), BF16),
        scratch_shapes=[pltpu.VMEM((t + 3 * CC_PAD + SUBLANES, c), F32)],
        compiler_params=_params("arbitrary"),
        name="conformer_conv",
    )(u, dw_w, dw_b.reshape(1, c), ln_g.reshape(1, c), ln_b.reshape(1, c))


SC_PAD = 8
SC_ROWS = 64
SC_COLS = 512


def _sconv_body(x_ref, w_ref, b_ref, o_ref, pad_s, *, ctx_len, seq_len):
    def ident(src_ref, r, rows):
        return src_ref[0, pl.ds(r, rows), :]

    _fill_padded(x_ref, pad_s, SC_COLS, ctx_len, seq_len, SC_PAD, TM, ident)

    def emit(r, acc):
        o_ref[0, pl.ds(r, SC_ROWS), :] = _silu(acc + b_ref[...]).astype(o_ref.dtype)

    _conv_segments(pad_s, w_ref, w_ref.shape[0], ctx_len, seq_len, SC_PAD, SC_ROWS, emit)


def _ssm_conv(xbc, conv_w, conv_b, ctx_len):
    bsz, t, c = xbc.shape
    taps = conv_w.shape[0]
    return pl.pallas_call(
        functools.partial(_sconv_body, ctx_len=ctx_len, seq_len=t - ctx_len),
        grid=(bsz, c // SC_COLS),
        in_specs=[pl.BlockSpec((1, t, SC_COLS), lambda b, j: (b, 0, j)),
                  pl.BlockSpec((taps, SC_COLS), lambda b, j: (0, j)),
                  pl.BlockSpec((1, SC_COLS), lambda b, j: (0, j))],
        out_specs=pl.BlockSpec((1, t, SC_COLS), lambda b, j: (b, 0, j)),
        out_shape=jax.ShapeDtypeStruct((bsz, t, c), BF16),
        scratch_shapes=[pltpu.VMEM((t + 3 * SC_PAD + SUBLANES, SC_COLS), F32)],
        compiler_params=_params("arbitrary", "arbitrary"),
        name="ssm_conv",
    )(xbc, conv_w, conv_b.reshape(1, c))


def _even_ffn_body(*refs, n_parts, n_ctx_tiles):
    x = _token_tile(refs[:n_parts], n_ctx_tiles)
    a_ref, c_ref, mod_ref, n2_ref, wo_ref, w1_ref, w3_ref, w2_ref, o_ref = refs[n_parts:]
    ca = a_ref.shape[2]
    mix = _dot(a_ref[0], wo_ref[0:ca, :]) + _dot(c_ref[0], wo_ref[ca:, :])
    x1 = x + mod_ref[0, 0, 2:3, :] * mix
    h = _rms(x1, n2_ref[...]) * (1.0 + mod_ref[0, 0, 4:5, :]) + mod_ref[0, 0, 3:4, :]
    hb = h.astype(BF16)
    u = _silu(_dot(hb, w1_ref[...])) * _dot(hb, w3_ref[...])
    o_ref[0] = x1 + mod_ref[0, 0, 5:6, :] * _dot(u.astype(BF16), w2_ref[...])


def _even_ffn(parts, attn, conv, mod, norm2, wo, w1, w3, w2, n_ctx_tiles):
    bsz, t, ca = attn.shape
    d = parts[0].shape[2]
    cc = conv.shape[2]
    return pl.pallas_call(
        functools.partial(_even_ffn_body, n_parts=len(parts), n_ctx_tiles=n_ctx_tiles),
        grid=(bsz, t // TM),
        in_specs=_token_specs(parts, n_ctx_tiles) + [
                  pl.BlockSpec((1, TM, ca), lambda b, i: (b, i, 0)),
                  pl.BlockSpec((1, TM, cc), lambda b, i: (b, i, 0)),
                  pl.BlockSpec((1, 1, 6, d), lambda b, i: (b, (i >= n_ctx_tiles).astype(jnp.int32), 0, 0)),
                  _resident((1, d)), _resident(wo.shape), _resident(w1.shape), _resident(w3.shape),
                  _resident(w2.shape)],
        out_specs=pl.BlockSpec((1, TM, d), lambda b, i: (b, i, 0)),
        out_shape=jax.ShapeDtypeStruct((bsz, t, d), F32),
        compiler_params=_params("arbitrary", "arbitrary"),
        name="even_ffn",
    )(*parts, attn, conv, mod, norm2.reshape(1, d), wo, w1, w3, w2)


def _softplus(x):
    return jnp.maximum(x, 0.0) + jnp.log(1.0 + jnp.exp(-jnp.abs(x)))


def _ssd_body(xs_ref, bm_ref, cm_ref, dt_ref, dtt_ref, alr_ref, alc_ref, bir_ref, bic_ref, exp_ref, sel_ref,
              *rest, rev, n_ctx_steps, combine):
    if combine:
        yb_ref, dsk_ref, y_ref, st_s = rest
    else:
        y_ref, st_s = rest
    c = pl.program_id(1)
    nh = alr_ref.shape[1]
    hpg = nh // SSM_GROUPS
    gw = hpg * SSM_HEADDIM
    end = 0 if rev else CHUNK - 1

    @pl.when(c == 0)
    def _():
        st_s[...] = jnp.zeros_like(st_s)

    off = nh if rev else 0
    row = lax.broadcasted_iota(jnp.int32, (CHUNK, CHUNK), 0)
    col = lax.broadcasted_iota(jnp.int32, (CHUNK, CHUNK), 1)
    tri = (col >= row) if rev else (col <= row)
    trit = (row >= col) if rev else (row <= col)
    tri_b = tri.astype(BF16)
    trit_b = trit.astype(BF16)
    lane = lax.broadcasted_iota(jnp.int32, (1, LANES), 1)
    low = lane < SSM_HEADDIM

    def cat3(v):
        return jnp.concatenate(_split3(v), axis=1)

    def step(with_y, k):
        rows = slice(k * CHUNK, (k + 1) * CHUNK)
        dt = _softplus(dt_ref[0, rows, off:off + nh] + bir_ref[...])
        dtt = _softplus(dtt_ref[0, off:off + nh, rows] + bic_ref[...])
        a = dt * (-jnp.exp(alr_ref[...]))
        at = dtt * (-jnp.exp(alc_ref[...]))
        acs = sum(_dot(tri_b, p) for p in _split3(a))
        acst = sum(_dot(p, trit_b) for p in _split3(at))
        tot = acs[end:end + 1, :]
        w_end = dt * jnp.exp(tot - acs)
        eacs = jnp.exp(acs)
        cdec = jnp.exp(tot)
        pieces = [w_end, jnp.broadcast_to(cdec, (SUBLANES, nh))]
        if with_y:
            pieces = [dt, eacs] + pieces
        spread = _dot(cat3(jnp.concatenate(pieces, axis=0)), exp_ref[...])
        r0 = 2 * CHUNK if with_y else 0
        wex = spread[r0:r0 + CHUNK]
        dec = spread[r0 + CHUNK:r0 + CHUNK + 1]
        xs = xs_ref[0, rows, :].astype(F32)
        xcd = (xs * wex).astype(BF16)
        if with_y:
            xc = (xs * spread[0:CHUNK]).astype(BF16)
            eax = spread[CHUNK:2 * CHUNK]
            acs_l = _dot(cat3(acs), sel_ref[...])
        for g in range(SSM_GROUPS):
            gs = slice(g * gw, (g + 1) * gw)
            bm = bm_ref[0, rows, g * D_STATE:(g + 1) * D_STATE]
            bmt = bm.astype(F32).T.astype(BF16)
            sg = st_s[g]
            st_s[g] = sg * dec[:, gs] + _dot(bmt, xcd[:, gs])

            if not with_y:
                continue
            cmb = cm_ref[0, rows, g * D_STATE:(g + 1) * D_STATE].astype(BF16)
            cb = _dot(cmb, bmt)
            yoff = _dot(cmb, sg.astype(BF16))
            for kp in range(hpg // 2):
                e0 = g * hpg + 2 * kp
                c0 = e0 * SSM_HEADDIM
                xcb = xc[:, c0:c0 + LANES]
                res = []
                for e in (e0, e0 + 1):
                    diff = acs_l[:, e * LANES:(e + 1) * LANES] - acst[e:e + 1, :]
                    m = (jnp.exp(jnp.where(tri, diff, -jnp.inf)) * cb).astype(BF16)
                    res.append(_dot(m, xcb))
                y = jnp.where(low, res[0], res[1]) + yoff[:, kp * LANES:(kp + 1) * LANES] * eax[:, c0:c0 + LANES]
                if combine:
                    y = (y + yb_ref[0, rows, c0:c0 + LANES].astype(F32)
                         + dsk_ref[:, c0:c0 + LANES] * xs[:, c0:c0 + LANES])
                y_ref[0, rows, c0:c0 + LANES] = y.astype(y_ref.dtype)

    order = range(SSD_CPS - 1, -1, -1) if rev else range(SSD_CPS)

    @pl.when(c >= n_ctx_steps)
    def _():
        for k in order:
            step(True, k)

    @pl.when(c < n_ctx_steps)
    def _():
        for k in order:
            step(False, k)


def _ssd_scan(xbc, dt, dtt, a_log, dt_bias, ctx_len, rev, y_other=None, d_skip=None):
    bsz, t, _ = xbc.shape
    nh = a_log.shape[0]
    d_inner = nh * SSM_HEADDIM
    gn = SSM_GROUPS * D_STATE
    rows = SSD_CPS * CHUNK
    assert t % rows == 0 and ctx_len % rows == 0
    nc = t // rows
    ncc = ctx_len // rows
    seq_len = t - ctx_len
    combine = y_other is not None

    if rev:
        def chunk(i):
            return jnp.where(i < ncc, ncc - 1 - i, nc - 1 + ncc - i)
    else:
        def chunk(i):
            return i

    def ychunk(i):
        return jnp.maximum(chunk(i), ncc) - ncc if not rev else jnp.where(i < ncc, nc - 1 - ncc, chunk(i) - ncc)

    bcol = d_inner // gn
    in_specs = [pl.BlockSpec((1, rows, d_inner), lambda b, i: (b, chunk(i), 0)),
                pl.BlockSpec((1, rows, gn), lambda b, i: (b, chunk(i), bcol)),
                pl.BlockSpec((1, rows, gn), lambda b, i: (b, chunk(i), bcol + 1)),
                pl.BlockSpec((1, rows, 2 * nh), lambda b, i: (b, chunk(i), 0)),
                pl.BlockSpec((1, 2 * nh, rows), lambda b, i: (b, 0, chunk(i))),
                _resident((1, nh)), _resident((nh, 1)), _resident((1, nh)), _resident((nh, 1)),
                _resident((3 * nh, d_inner)), _resident((3 * nh, nh * LANES))]
    heads = jnp.arange(nh, dtype=jnp.int32)[:, None]
    spread_p = jnp.tile((jnp.arange(d_inner, dtype=jnp.int32)[None, :] // SSM_HEADDIM == heads).astype(BF16), (3, 1))
    spread_l = jnp.tile((jnp.arange(nh * LANES, dtype=jnp.int32)[None, :] // LANES == heads).astype(BF16), (3, 1))
    args = [xbc, xbc, xbc, dt, dtt, a_log.reshape(1, nh), a_log.reshape(nh, 1),
            dt_bias.reshape(1, nh), dt_bias.reshape(nh, 1), spread_p, spread_l]
    if combine:
        in_specs += [pl.BlockSpec((1, rows, d_inner), lambda b, i: (b, ychunk(i), 0)), _resident((1, d_inner))]
        args += [y_other, jnp.repeat(d_skip, SSM_HEADDIM).reshape(1, d_inner)]
    return pl.pallas_call(
        functools.partial(_ssd_body, rev=rev, n_ctx_steps=ncc, combine=combine),
        grid=(bsz, nc),
        in_specs=in_specs,
        out_specs=pl.BlockSpec((1, rows, d_inner), lambda b, i: (b, ychunk(i), 0)),
        out_shape=jax.ShapeDtypeStruct((bsz, seq_len, d_inner), BF16),
        scratch_shapes=[pltpu.VMEM((SSM_GROUPS, D_STATE, d_inner // SSM_GROUPS), F32)],
        compiler_params=_params("arbitrary", "arbitrary"),
        name="ssd_bwd" if rev else "ssd_fwd",
    )(*args)


def _odd_out_body(x_ref, y_ref, z_ref, mod_ref, gn_ref, n2_ref, wout_ref, wr_ref, x1_ref, h_ref, ti_ref, tw_ref):
    z = z_ref[0].astype(F32)
    yn = _rms(y_ref[0].astype(F32) * _silu(z), gn_ref[...])
    x1 = x_ref[0] + mod_ref[0, 0, 2:3, :] * _dot(yn.astype(BF16), wout_ref[...])
    x1_ref[0] = x1
    h = _rms(x1, n2_ref[...]) * (1.0 + mod_ref[0, 0, 4:5, :]) + mod_ref[0, 0, 3:4, :]
    h_ref[0] = h
    ne = wr_ref.shape[1]
    h1, h2, _ = _split3(h)
    w1, w2, _ = _split3(wr_ref[...])
    hw = _dot(h1, jnp.concatenate([w1, w2], axis=1))
    lg = hw[:, 0:ne] + hw[:, ne:2 * ne] + _dot(h2, w1)
    idx = lax.broadcasted_iota(jnp.int32, lg.shape, 1)
    m1 = jnp.max(lg, axis=-1, keepdims=True)
    i1 = jnp.min(jnp.where(lg == m1, idx, ne), axis=-1, keepdims=True)
    lg2 = jnp.where(idx == i1, -jnp.inf, lg)
    m2 = jnp.max(lg2, axis=-1, keepdims=True)
    i2 = jnp.min(jnp.where(lg2 == m2, idx, ne), axis=-1, keepdims=True)
    e2 = jnp.exp(m2 - m1)
    den = 1.0 + e2
    ti_ref[0] = jnp.concatenate([i1, i2], axis=1)
    tw_ref[0] = jnp.concatenate([1.0 / den, e2 / den], axis=1)


def _odd_out(x, y, z, mod, gnorm, norm2, w_out, w_router, ctx_len):
    bsz, t, d = x.shape
    seq_len = t - ctx_len
    di = y.shape[2]
    ne = w_router.shape[1]
    nct = ctx_len // TM
    lat = lambda b, i: (b, i + nct, 0)
    own = lambda b, i: (b, i, 0)
    return pl.pallas_call(
        _odd_out_body,
        grid=(bsz, seq_len // TM),
        in_specs=[pl.BlockSpec((1, TM, d), lat),
                  pl.BlockSpec((1, TM, di), own),
                  pl.BlockSpec((1, TM, di), lat),
                  pl.BlockSpec((1, 1, 6, d), lambda b, i: (b, 1, 0, 0)),
                  _resident((1, di)), _resident((1, d)), _resident(w_out.shape), _resident((d, ne))],
        out_specs=[pl.BlockSpec((1, TM, d), own), pl.BlockSpec((1, TM, d), own),
                   pl.BlockSpec((1, TM, TOP_K), own), pl.BlockSpec((1, TM, TOP_K), own)],
        out_shape=[jax.ShapeDtypeStruct((bsz, seq_len, d), F32), jax.ShapeDtypeStruct((bsz, seq_len, d), F32),
                   jax.ShapeDtypeStruct((bsz, seq_len, TOP_K), jnp.int32),
                   jax.ShapeDtypeStruct((bsz, seq_len, TOP_K), F32)],
        compiler_params=_params("arbitrary", "arbitrary"),
        name="odd_out",
    )(x, y, z, mod, gnorm.reshape(1, di), norm2.reshape(1, d), w_out, w_router)


def _start_row(idx_ref, j, src_hbm, dst, sem, priority):
    pltpu.make_async_copy(src_hbm.at[pl.ds(idx_ref[0, 0, j], 1), :], dst.at[pl.ds(j, 1), :], sem).start(
        priority=priority)


def _start_rows(idx_ref, src_hbm, dst, sem, n):
    def body(j, carry):
        _start_row(idx_ref, 2 * j, src_hbm, dst, sem, 0)
        _start_row(idx_ref, 2 * j + 1, src_hbm, dst, sem, 1)
        return carry
    lax.fori_loop(0, n // 2, body, 0)


def _wait_rows(src_hbm, dst, sem, n):
    pltpu.make_async_copy(src_hbm.at[pl.ds(0, n), :], dst, sem).wait()


def _expert_body(te_ref, na_ref, rows_ref, next_rows_ref, dst_ref, h_hbm, w1_ref, w3_ref, w2_ref, y_hbm,
                 xf_s, xb_s, acc_s, out_s, gsem, ssem, *, nf, nt):
    ti = pl.program_id(0)
    f = pl.program_id(1)
    real = ti < nt
    active = ti < na_ref[0]
    slot = ti % 2
    part = TM_E // nf
    base = f * part
    last = f == nf - 1

    @pl.when(jnp.logical_and(ti == 0, f == 0))
    def _():
        out_s[...] = jnp.zeros_like(out_s)
        _start_rows(rows_ref, h_hbm, xf_s.at[0], gsem.at[0], TM_E)

    @pl.when(f == 0)
    def _():
        _wait_rows(h_hbm, xf_s.at[slot], gsem.at[slot], TM_E)

    @pl.when(jnp.logical_and(active, f == 0))
    def _():
        xb_s[...] = xf_s[slot].astype(BF16)
        acc_s[...] = jnp.zeros_like(acc_s)

    def scatter_part():
        for j in range(part):
            pltpu.make_async_copy(out_s.at[pl.ds(base + j, 1), :],
                                  y_hbm.at[pl.ds(dst_ref[0, 0, base + j], 1), :], ssem).start(priority=j % 2)

    def gather_part():
        for j in range(part):
            _start_row(next_rows_ref, base + j, h_hbm, xf_s.at[1 - slot], gsem.at[1 - slot], 1)

    @pl.when(active)
    def _():
        scatter_part()
        gather_part()
        xb = xb_s[...]
        u = _silu(_dot(xb, w1_ref[0])) * _dot(xb, w3_ref[0])
        acc_s[...] += _dot(u.astype(BF16), w2_ref[0])

    @pl.when(jnp.logical_and(real, jnp.logical_not(active)))
    def _():
        scatter_part()
        gather_part()

    @pl.when(jnp.logical_not(real))
    def _():
        scatter_part()

    @pl.when(last)
    def _():
        pltpu.make_async_copy(out_s, y_hbm.at[pl.ds(0, TM_E), :], ssem).wait()

    @pl.when(jnp.logical_and(active, last))
    def _():
        out_s[...] = acc_s[...]


def _expert_ffn(h, tile_expert, n_active, row_token, row_dst, n_y_rows, w1, w3, w2):
    n, d = h.shape
    ns = row_token.shape[0]
    nt = ns // TM_E
    dff = w1.shape[2]
    nf = dff // TF_E

    def wcol(ti, f, te, na):
        return (te[jnp.minimum(ti, nt - 1)], 0, jnp.where(ti < na[0], f, nf - 1))

    def wrow(ti, f, te, na):
        return (te[jnp.minimum(ti, nt - 1)], jnp.where(ti < na[0], f, nf - 1), 0)

    def idx_block(shift):
        return pl.BlockSpec((1, 1, TM_E), lambda ti, f, te, na: (jnp.minimum(ti + shift, nt), 0, 0),
                            memory_space=pltpu.SMEM)

    grid_spec = pltpu.PrefetchScalarGridSpec(
        num_scalar_prefetch=2,
        grid=(nt + 1, nf),
        in_specs=[idx_block(0), idx_block(1), idx_block(0),
                  pl.BlockSpec(memory_space=pl.ANY),
                  pl.BlockSpec((1, d, TF_E), wcol),
                  pl.BlockSpec((1, d, TF_E), wcol),
                  pl.BlockSpec((1, TF_E, d), wrow)],
        out_specs=pl.BlockSpec(memory_space=pl.ANY),
        scratch_shapes=[pltpu.VMEM((2, TM_E, d), F32), pltpu.VMEM((TM_E, d), BF16), pltpu.VMEM((TM_E, d), F32),
                        pltpu.VMEM((TM_E, d), F32), pltpu.SemaphoreType.DMA((2,)), pltpu.SemaphoreType.DMA(())],
    )
    rows = jnp.concatenate([row_token, jnp.zeros((TM_E,), jnp.int32)]).reshape(nt + 1, 1, TM_E)
    dst = row_dst.reshape(nt + 1, 1, TM_E)
    return pl.pallas_call(
        functools.partial(_expert_body, nf=nf, nt=nt),
        grid_spec=grid_spec,
        out_shape=jax.ShapeDtypeStruct((n_y_rows, d), F32),
        compiler_params=_params("arbitrary", "arbitrary"),
        name="expert_ffn",
    )(tile_expert, n_active, rows, rows, dst, h, w1, w3, w2)


def _combine_body(x_ref, y_ref, tw_ref, g2_ref, fn_ref, o_ref):
    d = x_ref.shape[1]
    tw = tw_ref[...]
    moe = tw[:, 0:1] * y_ref[:, 0:d] + tw[:, 1:2] * y_ref[:, d:2 * d]
    o_ref[...] = _rms(x_ref[...] + g2_ref[0] * moe, fn_ref[...])


def _moe_combine(x1, topw, g2, final_norm, y, tiles_per_batch):
    n, d = x1.shape
    y2 = y.reshape(y.shape[0] // TOP_K, TOP_K * d)
    return pl.pallas_call(
        _combine_body,
        grid=(n // TM_C,),
        in_specs=[pl.BlockSpec((TM_C, d), lambda i: (i, 0)),
                  pl.BlockSpec((TM_C, TOP_K * d), lambda i: (i, 0)),
                  pl.BlockSpec((TM_C, TOP_K), lambda i: (i, 0)),
                  pl.BlockSpec((1, 1, d), lambda i: (i // tiles_per_batch, 0, 0)),
                  _resident((1, d))],
        out_specs=pl.BlockSpec((TM_C, d), lambda i: (i, 0)),
        out_shape=jax.ShapeDtypeStruct((n, d), F32),
        compiler_params=_params("arbitrary"),
        name="moe_combine",
    )(x1, y2, topw, g2, final_norm.reshape(1, d))


def _route(topi, n_slots):
    n = topi.shape[0]
    e_flat = topi.reshape(n * TOP_K)
    oh = (e_flat[:, None] == jnp.arange(N_EXPERTS, dtype=jnp.int32)[None, :]).astype(jnp.int32)
    csum = jnp.cumsum(oh, axis=0)
    rank = jnp.sum(csum * oh, axis=1) - 1
    cnt = csum[-1]
    cnt_pad = (cnt + TM_E - 1) // TM_E * TM_E
    ends = jnp.cumsum(cnt_pad)
    offs = ends - cnt_pad
    slot = jnp.sum(oh * offs[None, :], axis=1) + rank
    n_active = (ends[-1] // TM_E).astype(jnp.int32)
    nt = n_slots // TM_E
    tile_start = jnp.arange(nt, dtype=jnp.int32) * TM_E
    te = jnp.sum((tile_start[:, None] >= ends[None, :]).astype(jnp.int32), axis=1)
    te = jnp.minimum(te, N_EXPERTS - 1)
    te = jnp.where(jnp.arange(nt) < n_active, te, te[jnp.maximum(n_active - 1, 0)])
    n_routed = n * TOP_K
    owner = jnp.full((n_slots,), -1, jnp.int32).at[slot].set(jnp.arange(n_routed, dtype=jnp.int32))
    is_pad = owner < 0
    spare = n_routed + TM_E + jnp.cumsum(is_pad.astype(jnp.int32)) - 1
    row_token = jnp.where(is_pad, 0, owner // TOP_K)
    row_dst = jnp.concatenate([n_routed + jnp.arange(TM_E, dtype=jnp.int32), jnp.where(is_pad, spare, owner)])
    n_y_rows = n_slots + TM_E
    return te.astype(jnp.int32), n_active.reshape(1), row_token, row_dst, n_y_rows


def kernel(x, c, ctx, c_ctx, ev_ada_w, ev_ada_b, ev_norm1, ev_norm2, ev_w_in, ev_q_gain, ev_k_gain, ev_dw_w, ev_dw_b, ev_ln_g, ev_ln_b, ev_w_o, ev_ff_w1, ev_ff_w3, ev_ff_w2, od_ada_w, od_ada_b, od_norm1, od_norm2, od_w_in, od_conv_w, od_conv_b, od_a_log_f, od_a_log_b, od_dt_bias_f, od_dt_bias_b, od_d_skip, od_gnorm, od_w_out, od_router, od_ex_w1, od_ex_w3, od_ex_w2, final_norm):
    bsz, seq_len, d = x.shape
    ctx_len = ctx.shape[1]
    assert ev_ada_w.shape[0] == 1 and od_ada_w.shape[0] == 1, "one even and one odd layer"
    assert ctx_len % TM == 0 and seq_len % TM == 0 and seq_len % GRID_W == 0
    n_ctx_tiles = ctx_len // TM
    mod = _mod_table(c, c_ctx, ev_ada_w[0], ev_ada_b[0])
    cc = ev_dw_w.shape[2]
    splits = [(0, ATTN_W), (ATTN_W, ATTN_W + KV_W), (ATTN_W + KV_W, ATTN_W + 2 * KV_W),
              (ATTN_W + 2 * KV_W, ATTN_W + 2 * KV_W + 2 * cc)]
    q, k, v, u = _inproj((ctx, x), mod, ev_norm1[0], ev_w_in[0].astype(BF16), splits, (F32, F32, BF16, F32),
                         n_ctx_tiles)
    cos2, sin2 = _rope_tables(ctx_len, seq_len)
    attn = _attention(q, k, v, cos2, sin2, ev_q_gain[0], ev_k_gain[0], ctx_len)
    conv = _conformer_conv(u, ev_dw_w[0], ev_dw_b[0], ev_ln_g[0], ev_ln_b[0], ctx_len)
    xa = _even_ffn((ctx, x), attn, conv, mod, ev_norm2[0], ev_w_o[0].astype(BF16), ev_ff_w1[0].astype(BF16),
                   ev_ff_w3[0].astype(BF16), ev_ff_w2[0].astype(BF16), n_ctx_tiles)

    mod = _mod_table(c, c_ctx, od_ada_w[0], od_ada_b[0])
    nh = od_a_log_f.shape[1]
    d_inner = nh * SSM_HEADDIM
    conv_dim = od_conv_w.shape[2]
    splits = [(0, d_inner), (d_inner, d_inner + conv_dim), (d_inner + conv_dim, d_inner + conv_dim + 2 * nh)]
    z, xbc, dt = _inproj((xa,), mod, od_norm1[0], od_w_in[0].astype(BF16), splits, (BF16, F32, F32), n_ctx_tiles)
    xbc = _ssm_conv(xbc, od_conv_w[0], od_conv_b[0], ctx_len)
    dtt = dt.transpose(0, 2, 1)
    y_b = _ssd_scan(xbc, dt, dtt, od_a_log_b[0], od_dt_bias_b[0], ctx_len, rev=True)
    y = _ssd_scan(xbc, dt, dtt, od_a_log_f[0], od_dt_bias_f[0], ctx_len, rev=False, y_other=y_b,
                  d_skip=od_d_skip[0])
    x1, h, topi, topw = _odd_out(xa, y, z, mod, od_gnorm[0], od_norm2[0], od_w_out[0].astype(BF16),
                                 od_router[0], ctx_len)

    n = bsz * seq_len
    n_slots = n * TOP_K + N_EXPERTS * TM_E
    tile_expert, n_active, row_token, row_dst, n_y_rows = _route(topi.reshape(n, TOP_K), n_slots)
    y_moe = _expert_ffn(h.reshape(n, d), tile_expert, n_active, row_token, row_dst, n_y_rows,
                        od_ex_w1[0].astype(BF16), od_ex_w3[0].astype(BF16), od_ex_w2[0].astype(BF16))
    g2 = mod[:, 1, 5:6, :]
    out = _moe_combine(x1.reshape(n, d), topw.reshape(n, TOP_K), g2, final_norm, y_moe, seq_len // TM_C)
    return out.reshape(bsz, seq_len, d)
```

```python
import functools

import jax
import jax.numpy as jnp
from jax import lax
from jax.experimental import pallas as pl
from jax.experimental.pallas import tpu as pltpu

F32 = jnp.float32
BF16 = jnp.bfloat16
EPS = 1e-6
LOG2_E = 1.4426950408889634

GRID_W = 64
HEAD_DIM = 64
ATTN_HEADS = 8
KV_HEADS = 2
ATTN_W = ATTN_HEADS * HEAD_DIM
KV_W = KV_HEADS * HEAD_DIM
ROPE_THETA = 10000.0
SSM_HEADDIM = 64
SSM_GROUPS = 4
D_STATE = 128
CHUNK = 128
N_EXPERTS = 8
TOP_K = 2

LANES = 128
SUBLANES = 8
VMEM_LIMIT = 56 * 1024 * 1024

TM = 256
TM_E = 512
TF_E = 1792
TM_C = 256
SSD_CPS = 2


def _params(*sem):
    return pltpu.CompilerParams(dimension_semantics=sem, vmem_limit_bytes=VMEM_LIMIT)


def _resident(shape):
    nd = len(shape)
    return pl.BlockSpec(shape, lambda *_: (0,) * nd, pipeline_mode=pl.Buffered(1))


def _silu(x):
    return x * jax.nn.sigmoid(x)


def _rms(x, g):
    return x * lax.rsqrt(jnp.mean(x * x, axis=-1, keepdims=True) + EPS) * g


def _dot(a, b):
    return jnp.dot(a, b, preferred_element_type=F32)


def _split3(a):
    a1 = a.astype(BF16)
    r1 = a - a1.astype(F32)
    a2 = r1.astype(BF16)
    a3 = (r1 - a2.astype(F32)).astype(BF16)
    return a1, a2, a3


def _ada_body(c_ref, w_ref, b_ref, o_ref):
    s = _silu(c_ref[...]).astype(BF16)
    o_ref[...] = _dot(s, w_ref[...].astype(BF16)) + b_ref[...]


def _adaln(cond, w, b):
    r, d = cond.shape
    n = w.shape[1]
    tn = n // 4
    return pl.pallas_call(
        _ada_body,
        grid=(n // tn,),
        in_specs=[pl.BlockSpec((r, d), lambda j: (0, 0)),
                  pl.BlockSpec((d, tn), lambda j: (0, j)),
                  pl.BlockSpec((1, tn), lambda j: (0, j))],
        out_specs=pl.BlockSpec((r, tn), lambda j: (0, j)),
        out_shape=jax.ShapeDtypeStruct((r, n), F32),
        compiler_params=_params("arbitrary"),
        name="adaln",
    )(cond, w, b.reshape(1, n))


def _mod_table(c, c_ctx, w, b):
    bsz, d = c.shape
    rows = -(-(bsz + 1) // SUBLANES) * SUBLANES
    cond = jnp.zeros((rows, d), F32).at[:bsz].set(c).at[bsz].set(c_ctx)
    m = _adaln(cond, w, b)
    lat = m[:bsz]
    ctx = jnp.broadcast_to(m[bsz][None], lat.shape)
    return jnp.stack([ctx, lat], axis=1).reshape(bsz, 2, 6, d)


def _token_specs(parts, n_ctx_tiles):
    d = parts[0].shape[2]
    if len(parts) == 1:
        return [pl.BlockSpec((1, TM, d), lambda b, i: (b, i, 0))]
    return [pl.BlockSpec((1, TM, d), lambda b, i: (b, jnp.minimum(i, n_ctx_tiles - 1), 0)),
            pl.BlockSpec((1, TM, d), lambda b, i: (b, jnp.maximum(i - n_ctx_tiles, 0), 0))]


def _token_tile(refs, n_ctx_tiles):
    if len(refs) == 1:
        return refs[0][0]
    return jnp.where(pl.program_id(1) < n_ctx_tiles, refs[0][0], refs[1][0])


def _inproj_body(*refs, splits, n_parts, n_ctx_tiles):
    x = _token_tile(refs[:n_parts], n_ctx_tiles)
    mod_ref, n_ref, w_ref = refs[n_parts:n_parts + 3]
    out_refs = refs[n_parts + 3:]
    h = _rms(x, n_ref[...]) * (1.0 + mod_ref[0, 0, 1:2, :]) + mod_ref[0, 0, 0:1, :]
    r = _dot(h.astype(BF16), w_ref[...])
    for o_ref, (lo, hi) in zip(out_refs, splits):
        o_ref[0] = r[:, lo:hi].astype(o_ref.dtype)


def _inproj(parts, mod, norm, w_bf16, splits, dtypes, n_ctx_tiles):
    bsz, _, d = parts[0].shape
    t = sum(p.shape[1] for p in parts)
    n = w_bf16.shape[1]
    widths = [hi - lo for lo, hi in splits]
    return pl.pallas_call(
        functools.partial(_inproj_body, splits=tuple(splits), n_parts=len(parts), n_ctx_tiles=n_ctx_tiles),
        grid=(bsz, t // TM),
        in_specs=_token_specs(parts, n_ctx_tiles) + [
            pl.BlockSpec((1, 1, 6, d), lambda b, i: (b, (i >= n_ctx_tiles).astype(jnp.int32), 0, 0)),
            _resident((1, d)),
            _resident((d, n))],
        out_specs=[pl.BlockSpec((1, TM, wd), lambda b, i: (b, i, 0)) for wd in widths],
        out_shape=[jax.ShapeDtypeStruct((bsz, t, wd), dt) for wd, dt in zip(widths, dtypes)],
        compiler_params=_params("arbitrary", "arbitrary"),
        name="inproj",
    )(*parts, mod, norm.reshape(1, d), w_bf16)


def _attn_body(q_ref, k_ref, v_ref, cq_ref, sq_ref, ck_ref, sk_ref, qg_ref, kg_ref, o_ref, kt_s, v2_s,
               *, n_ctx_tiles, ctx_len):
    i = pl.program_id(1)
    t = k_ref.shape[1]
    lane = lax.broadcasted_iota(jnp.int32, (1, LANES), 1)
    low = lane < HEAD_DIM
    even = (lane % 2) == 0

    def norm_rope(x, gain, cos, sin):
        x2 = x * x
        s_lo = jnp.sum(jnp.where(low, x2, 0.0), axis=-1, keepdims=True)
        s_hi = jnp.sum(jnp.where(low, 0.0, x2), axis=-1, keepdims=True)
        ms = jnp.where(low, s_lo, s_hi) * (1.0 / HEAD_DIM)
        xn = x * lax.rsqrt(ms + EPS) * gain
        swapped = jnp.where(even, pltpu.roll(xn, LANES - 1, 1), pltpu.roll(xn, 1, 1))
        return xn * cos + swapped * sin

    @pl.when(i == 0)
    def _():
        k = norm_rope(k_ref[0], kg_ref[...], ck_ref[...], sk_ref[...])
        kr = pltpu.roll(k, HEAD_DIM, 1)
        kt_s[0] = jnp.where(low, k, kr).T.astype(BF16)
        kt_s[1] = jnp.where(low, kr, k).T.astype(BF16)
        v = v_ref[0].astype(F32)
        vr = pltpu.roll(v, HEAD_DIM, 1)
        v2_s[0] = jnp.where(low, v, vr).astype(BF16)
        v2_s[1] = jnp.where(low, vr, v).astype(BF16)

    def run(tk):
        for j in range(ATTN_HEADS // 2):
            g = (2 * j) // (ATTN_HEADS // KV_HEADS)
            qp = norm_rope(q_ref[0, :, j * LANES:(j + 1) * LANES], qg_ref[...], cq_ref[...], sq_ref[...])
            qp = qp * (HEAD_DIM ** -0.5 * LOG2_E)
            outs = []
            for hh in range(2):
                qm = jnp.where(low if hh == 0 else jnp.logical_not(low), qp, 0.0).astype(BF16)
                s = _dot(qm, kt_s[g, :, 0:tk])
                m = jnp.max(s, axis=-1, keepdims=True)
                p = jnp.exp2(s - m)
                l = jnp.sum(p, axis=-1, keepdims=True)
                o = _dot(p.astype(BF16), v2_s[g, 0:tk, :])
                outs.append(o / l)
            o_ref[0, :, j * LANES:(j + 1) * LANES] = jnp.where(low, outs[0], outs[1]).astype(o_ref.dtype)

    @pl.when(i < n_ctx_tiles)
    def _():
        run(ctx_len)

    @pl.when(i >= n_ctx_tiles)
    def _():
        run(t)


def _attention(q, k, v, cos2, sin2, q_gain, k_gain, ctx_len):
    bsz, t, _ = q.shape
    n_ctx_tiles = ctx_len // TM
    qg = jnp.tile(q_gain.reshape(1, HEAD_DIM), (1, 2))
    kg = jnp.tile(k_gain.reshape(1, HEAD_DIM), (1, 2))
    return pl.pallas_call(
        functools.partial(_attn_body, n_ctx_tiles=n_ctx_tiles, ctx_len=ctx_len),
        grid=(bsz, t // TM),
        in_specs=[pl.BlockSpec((1, TM, ATTN_W), lambda b, i: (b, i, 0)),
                  pl.BlockSpec((1, t, KV_W), lambda b, i: (b, 0, 0)),
                  pl.BlockSpec((1, t, KV_W), lambda b, i: (b, 0, 0)),
                  pl.BlockSpec((TM, LANES), lambda b, i: (i, 0)),
                  pl.BlockSpec((TM, LANES), lambda b, i: (i, 0)),
                  _resident((t, LANES)),
                  _resident((t, LANES)),
                  _resident((1, LANES)),
                  _resident((1, LANES))],
        out_specs=pl.BlockSpec((1, TM, ATTN_W), lambda b, i: (b, i, 0)),
        out_shape=jax.ShapeDtypeStruct((bsz, t, ATTN_W), BF16),
        scratch_shapes=[pltpu.VMEM((KV_HEADS, LANES, t), BF16), pltpu.VMEM((KV_HEADS, t, LANES), BF16)],
        compiler_params=_params("arbitrary", "arbitrary"),
        name="attention",
    )(q, k, v, cos2, sin2, cos2, sin2, qg, kg)


def _rope_tables(ctx_len, seq_len):
    rows = seq_len // GRID_W
    t_row = jnp.repeat(jnp.arange(rows, dtype=F32), GRID_W)
    t_col = jnp.tile(jnp.arange(GRID_W, dtype=F32), rows)
    axis_dim = HEAD_DIM // 2
    inv_freq = ROPE_THETA ** (-jnp.arange(0, axis_dim, 2, dtype=F32) / axis_dim)
    ang = jnp.concatenate([t_row[:, None] * inv_freq, t_col[:, None] * inv_freq], axis=-1)
    cos = jnp.repeat(jnp.cos(ang), 2, axis=-1)
    sin = jnp.repeat(jnp.sin(ang), 2, axis=-1) * jnp.tile(jnp.array([-1.0, 1.0], F32), axis_dim)
    cos = jnp.concatenate([jnp.ones((ctx_len, HEAD_DIM), F32), cos], axis=0)
    sin = jnp.concatenate([jnp.zeros((ctx_len, HEAD_DIM), F32), sin], axis=0)
    return jnp.tile(cos, (1, 2)), jnp.tile(sin, (1, 2))


def _fill_padded(src_ref, pad_s, width, ctx_len, seq_len, pad, rows, fn):
    t = ctx_len + seq_len
    z = jnp.zeros((pad + SUBLANES, width), F32)
    pad_s[0:pad, :] = z[0:pad]
    pad_s[pad + ctx_len:2 * pad + ctx_len, :] = z[0:pad]
    pad_s[2 * pad + t:3 * pad + t + SUBLANES, :] = z

    def seg(tok0, off, ntiles):
        def body(n, carry):
            r = pl.multiple_of(tok0 + n * rows, rows)
            pad_s[pl.ds(r + off, rows), :] = fn(src_ref, r, rows)
            return carry
        lax.fori_loop(0, ntiles, body, 0)

    seg(0, pad, ctx_len // rows)
    seg(ctx_len, 2 * pad, seq_len // rows)


def _conv_segments(pad_s, w_ref, taps, ctx_len, seq_len, pad, rows, emit):
    half = taps // 2
    span = rows + SUBLANES

    def seg(tok0, off, ntiles):
        def body(n, carry):
            r = pl.multiple_of(tok0 + n * rows, rows)
            base = r + off - pad
            acc = None
            for rho in range(SUBLANES):
                part = None
                for j in range(taps):
                    dj = pad - half + j
                    if dj % SUBLANES != rho:
                        continue
                    term = w_ref[j:j + 1, :] * pad_s[pl.ds(pl.multiple_of(base + (dj - rho), SUBLANES), span), :]
                    part = term if part is None else part + term
                if part is None:
                    continue
                if rho:
                    part = pltpu.roll(part, span - rho, 0)
                acc = part[0:rows] if acc is None else acc + part[0:rows]
            emit(r, acc)
            return carry
        lax.fori_loop(0, ntiles, body, 0)

    seg(0, pad, ctx_len // rows)
    seg(ctx_len, 2 * pad, seq_len // rows)


CC_PAD = 16
CC_ROWS = 32


def _cconv_body(u_ref, w_ref, b_ref, g_ref, bb_ref, o_ref, pad_s, *, ctx_len, seq_len):
    c = o_ref.shape[2]

    def glu(src_ref, r, rows):
        return src_ref[0, pl.ds(r, rows), 0:c] * jax.nn.sigmoid(src_ref[0, pl.ds(r, rows), c:2 * c])

    _fill_padded(u_ref, pad_s, c, ctx_len, seq_len, CC_PAD, TM, glu)

    def emit(r, acc):
        h = acc + b_ref[...]
        mu = jnp.mean(h, axis=-1, keepdims=True)
        hc = h - mu
        var = jnp.mean(hc * hc, axis=-1, keepdims=True)
        y = hc * lax.rsqrt(var + EPS) * g_ref[...] + bb_ref[...]
        o_ref[0, pl.ds(r, CC_ROWS), :] = _silu(y).astype(o_ref.dtype)

    _conv_segments(pad_s, w_ref, w_ref.shape[0], ctx_len, seq_len, CC_PAD, CC_ROWS, emit)


def _conformer_conv(u, dw_w, dw_b, ln_g, ln_b, ctx_len):
    bsz, t, c2 = u.shape
    c = c2 // 2
    taps = dw_w.shape[0]
    return pl.pallas_call(
        functools.partial(_cconv_body, ctx_len=ctx_len, seq_len=t - ctx_len),
        grid=(bsz,),
        in_specs=[pl.BlockSpec((1, t, c2), lambda b: (b, 0, 0)),
                  _resident((taps, c)), _resident((1, c)), _resident((1, c)), _resident((1, c))],
        out_specs=pl.BlockSpec((1, t, c), lambda b: (b, 0, 0)),
        out_shape=jax.ShapeDtypeStruct((bsz, t, c), BF16),
        scratch_shapes=[pltpu.VMEM((t + 3 * CC_PAD + SUBLANES, c), F32)],
        compiler_params=_params("arbitrary"),
        name="conformer_conv",
    )(u, dw_w, dw_b.reshape(1, c), ln_g.reshape(1, c), ln_b.reshape(1, c))


SC_PAD = 8
SC_ROWS = 64
SC_COLS = 512


def _sconv_body(x_ref, w_ref, b_ref, o_ref, pad_s, *, ctx_len, seq_len):
    def ident(src_ref, r, rows):
        return src_ref[0, pl.ds(r, rows), :]

    _fill_padded(x_ref, pad_s, SC_COLS, ctx_len, seq_len, SC_PAD, TM, ident)

    def emit(r, acc):
        o_ref[0, pl.ds(r, SC_ROWS), :] = _silu(acc + b_ref[...]).astype(o_ref.dtype)

    _conv_segments(pad_s, w_ref, w_ref.shape[0], ctx_len, seq_len, SC_PAD, SC_ROWS, emit)


def _ssm_conv(xbc, conv_w, conv_b, ctx_len):
    bsz, t, c = xbc.shape
    taps = conv_w.shape[0]
    return pl.pallas_call(
        functools.partial(_sconv_body, ctx_len=ctx_len, seq_len=t - ctx_len),
        grid=(bsz, c // SC_COLS),
        in_specs=[pl.BlockSpec((1, t, SC_COLS), lambda b, j: (b, 0, j)),
                  pl.BlockSpec((taps, SC_COLS), lambda b, j: (0, j)),
                  pl.BlockSpec((1, SC_COLS), lambda b, j: (0, j))],
        out_specs=pl.BlockSpec((1, t, SC_COLS), lambda b, j: (b, 0, j)),
        out_shape=jax.ShapeDtypeStruct((bsz, t, c), BF16),
        scratch_shapes=[pltpu.VMEM((t + 3 * SC_PAD + SUBLANES, SC_COLS), F32)],
        compiler_params=_params("arbitrary", "arbitrary"),
        name="ssm_conv",
    )(xbc, conv_w, conv_b.reshape(1, c))


def _even_ffn_body(*refs, n_parts, n_ctx_tiles, n_cast):
    x = _token_tile(refs[:n_parts], n_ctx_tiles)
    a_ref, c_ref, mod_ref, n2_ref, wo_ref, w1_ref, w3_ref, w2_ref = refs[n_parts:n_parts + 8]
    cast_in = refs[n_parts + 8:n_parts + 8 + n_cast]
    o_ref = refs[n_parts + 8 + n_cast]
    cast_out = refs[n_parts + 9 + n_cast:]
    for src, dst in zip(cast_in, cast_out):
        dst[...] = src[...].astype(dst.dtype)
    ca = a_ref.shape[2]
    mix = _dot(a_ref[0], wo_ref[0:ca, :]) + _dot(c_ref[0], wo_ref[ca:, :])
    x1 = x + mod_ref[0, 0, 2:3, :] * mix
    h = _rms(x1, n2_ref[...]) * (1.0 + mod_ref[0, 0, 4:5, :]) + mod_ref[0, 0, 3:4, :]
    hb = h.astype(BF16)
    u = _silu(_dot(hb, w1_ref[...])) * _dot(hb, w3_ref[...])
    o_ref[0] = x1 + mod_ref[0, 0, 5:6, :] * _dot(u.astype(BF16), w2_ref[...])


CAST_ROW_BLOCKS = 8


def _even_ffn(parts, attn, conv, mod, norm2, wo, w1, w3, w2, n_ctx_tiles, to_cast):
    bsz, t, ca = attn.shape
    d = parts[0].shape[2]
    cc = conv.shape[2]
    nt = t // TM
    row_blocks = max([r for r in (CAST_ROW_BLOCKS, 4, 2, 1) if to_cast[0].shape[0] * r <= bsz * nt], default=0)
    if row_blocks == 0:
        extra, to_cast = tuple(w.astype(BF16) for w in to_cast), ()
    else:
        extra = ()
    n_blocks = (to_cast[0].shape[0] if to_cast else 0) * row_blocks

    def cast_block(b, i):
        blk = jnp.minimum(b * nt + i, n_blocks - 1)
        return (blk // row_blocks, blk % row_blocks, 0)

    cast_specs = [pl.BlockSpec((1, w.shape[1] // row_blocks, w.shape[2]), cast_block) for w in to_cast]
    res = pl.pallas_call(
        functools.partial(_even_ffn_body, n_parts=len(parts), n_ctx_tiles=n_ctx_tiles, n_cast=len(to_cast)),
        grid=(bsz, nt),
        in_specs=_token_specs(parts, n_ctx_tiles) + [
                  pl.BlockSpec((1, TM, ca), lambda b, i: (b, i, 0)),
                  pl.BlockSpec((1, TM, cc), lambda b, i: (b, i, 0)),
                  pl.BlockSpec((1, 1, 6, d), lambda b, i: (b, (i >= n_ctx_tiles).astype(jnp.int32), 0, 0)),
                  _resident((1, d)), _resident(wo.shape), _resident(w1.shape), _resident(w3.shape),
                  _resident(w2.shape)] + cast_specs,
        out_specs=[pl.BlockSpec((1, TM, d), lambda b, i: (b, i, 0))] + cast_specs,
        out_shape=[jax.ShapeDtypeStruct((bsz, t, d), F32)] + [jax.ShapeDtypeStruct(w.shape, BF16) for w in to_cast],
        compiler_params=_params("arbitrary", "arbitrary"),
        name="even_ffn",
    )(*parts, attn, conv, mod, norm2.reshape(1, d), wo, w1, w3, w2, *to_cast)
    return res[0], tuple(res[1:]) + extra


def _softplus(x):
    return jnp.maximum(x, 0.0) + jnp.log(1.0 + jnp.exp(-jnp.abs(x)))


def _ssd_body(xs_ref, bm_ref, cm_ref, dt_ref, dtt_ref, alr_ref, alc_ref, bir_ref, bic_ref, exp_ref, sel_ref,
              *rest, rev, n_ctx_steps, combine):
    if combine:
        yb_ref, dsk_ref, y_ref, st_s = rest
    else:
        y_ref, st_s = rest
    c = pl.program_id(1)
    nh = alr_ref.shape[1]
    hpg = nh // SSM_GROUPS
    gw = hpg * SSM_HEADDIM
    end = 0 if rev else CHUNK - 1

    @pl.when(c == 0)
    def _():
        st_s[...] = jnp.zeros_like(st_s)

    off = nh if rev else 0
    row = lax.broadcasted_iota(jnp.int32, (CHUNK, CHUNK), 0)
    col = lax.broadcasted_iota(jnp.int32, (CHUNK, CHUNK), 1)
    tri = (col >= row) if rev else (col <= row)
    trit = (row >= col) if rev else (row <= col)
    tri_b = tri.astype(BF16)
    trit_b = trit.astype(BF16)
    lane = lax.broadcasted_iota(jnp.int32, (1, LANES), 1)
    low = lane < SSM_HEADDIM

    def cat3(v):
        return jnp.concatenate(_split3(v), axis=1)

    def step(with_y, k):
        rows = slice(k * CHUNK, (k + 1) * CHUNK)
        dt = _softplus(dt_ref[0, rows, off:off + nh] + bir_ref[...])
        dtt = _softplus(dtt_ref[0, off:off + nh, rows] + bic_ref[...])
        a = dt * (-jnp.exp(alr_ref[...]))
        at = dtt * (-jnp.exp(alc_ref[...]))
        acs = sum(_dot(tri_b, p) for p in _split3(a))
        acst = sum(_dot(p, trit_b) for p in _split3(at))
        tot = acs[end:end + 1, :]
        w_end = dt * jnp.exp(tot - acs)
        eacs = jnp.exp(acs)
        cdec = jnp.exp(tot)
        pieces = [w_end, jnp.broadcast_to(cdec, (SUBLANES, nh))]
        if with_y:
            pieces = [dt, eacs] + pieces
        spread = _dot(cat3(jnp.concatenate(pieces, axis=0)), exp_ref[...])
        r0 = 2 * CHUNK if with_y else 0
        wex = spread[r0:r0 + CHUNK]
        dec = spread[r0 + CHUNK:r0 + CHUNK + 1]
        xs = xs_ref[0, rows, :].astype(F32)
        xcd = (xs * wex).astype(BF16)
        if with_y:
            xc = (xs * spread[0:CHUNK]).astype(BF16)
            eax = spread[CHUNK:2 * CHUNK]
            acs_l = _dot(cat3(acs), sel_ref[...])
        for g in range(SSM_GROUPS):
            gs = slice(g * gw, (g + 1) * gw)
            bm = bm_ref[0, rows, g * D_STATE:(g + 1) * D_STATE]
            bmt = bm.astype(F32).T.astype(BF16)
            sg = st_s[g]
            st_s[g] = sg * dec[:, gs] + _dot(bmt, xcd[:, gs])

            if not with_y:
                continue
            cmb = cm_ref[0, rows, g * D_STATE:(g + 1) * D_STATE].astype(BF16)
            cb = _dot(cmb, bmt)
            yoff = _dot(cmb, sg.astype(BF16))
            for kp in range(hpg // 2):
                e0 = g * hpg + 2 * kp
                c0 = e0 * SSM_HEADDIM
                xcb = xc[:, c0:c0 + LANES]
                res = []
                for e in (e0, e0 + 1):
                    diff = acs_l[:, e * LANES:(e + 1) * LANES] - acst[e:e + 1, :]
                    m = (jnp.exp(jnp.where(tri, diff, -jnp.inf)) * cb).astype(BF16)
                    res.append(_dot(m, xcb))
                y = jnp.where(low, res[0], res[1]) + yoff[:, kp * LANES:(kp + 1) * LANES] * eax[:, c0:c0 + LANES]
                if combine:
                    y = (y + yb_ref[0, rows, c0:c0 + LANES].astype(F32)
                         + dsk_ref[:, c0:c0 + LANES] * xs[:, c0:c0 + LANES])
                y_ref[0, rows, c0:c0 + LANES] = y.astype(y_ref.dtype)

    order = range(SSD_CPS - 1, -1, -1) if rev else range(SSD_CPS)

    @pl.when(c >= n_ctx_steps)
    def _():
        for k in order:
            step(True, k)

    @pl.when(c < n_ctx_steps)
    def _():
        for k in order:
            step(False, k)


def _ssd_scan(xbc, dt, dtt, a_log, dt_bias, ctx_len, rev, y_other=None, d_skip=None):
    bsz, t, _ = xbc.shape
    nh = a_log.shape[0]
    d_inner = nh * SSM_HEADDIM
    gn = SSM_GROUPS * D_STATE
    rows = SSD_CPS * CHUNK
    assert t % rows == 0 and ctx_len % rows == 0
    nc = t // rows
    ncc = ctx_len // rows
    seq_len = t - ctx_len
    combine = y_other is not None

    if rev:
        def chunk(i):
            return jnp.where(i < ncc, ncc - 1 - i, nc - 1 + ncc - i)
    else:
        def chunk(i):
            return i

    def ychunk(i):
        return jnp.maximum(chunk(i), ncc) - ncc if not rev else jnp.where(i < ncc, nc - 1 - ncc, chunk(i) - ncc)

    bcol = d_inner // gn
    in_specs = [pl.BlockSpec((1, rows, d_inner), lambda b, i: (b, chunk(i), 0)),
                pl.BlockSpec((1, rows, gn), lambda b, i: (b, chunk(i), bcol)),
                pl.BlockSpec((1, rows, gn), lambda b, i: (b, chunk(i), bcol + 1)),
                pl.BlockSpec((1, rows, 2 * nh), lambda b, i: (b, chunk(i), 0)),
                pl.BlockSpec((1, 2 * nh, rows), lambda b, i: (b, 0, chunk(i))),
                _resident((1, nh)), _resident((nh, 1)), _resident((1, nh)), _resident((nh, 1)),
                _resident((3 * nh, d_inner)), _resident((3 * nh, nh * LANES))]
    heads = jnp.arange(nh, dtype=jnp.int32)[:, None]
    spread_p = jnp.tile((jnp.arange(d_inner, dtype=jnp.int32)[None, :] // SSM_HEADDIM == heads).astype(BF16), (3, 1))
    spread_l = jnp.tile((jnp.arange(nh * LANES, dtype=jnp.int32)[None, :] // LANES == heads).astype(BF16), (3, 1))
    args = [xbc, xbc, xbc, dt, dtt, a_log.reshape(1, nh), a_log.reshape(nh, 1),
            dt_bias.reshape(1, nh), dt_bias.reshape(nh, 1), spread_p, spread_l]
    if combine:
        in_specs += [pl.BlockSpec((1, rows, d_inner), lambda b, i: (b, ychunk(i), 0)), _resident((1, d_inner))]
        args += [y_other, jnp.repeat(d_skip, SSM_HEADDIM).reshape(1, d_inner)]
    return pl.pallas_call(
        functools.partial(_ssd_body, rev=rev, n_ctx_steps=ncc, combine=combine),
        grid=(bsz, nc),
        in_specs=in_specs,
        out_specs=pl.BlockSpec((1, rows, d_inner), lambda b, i: (b, ychunk(i), 0)),
        out_shape=jax.ShapeDtypeStruct((bsz, seq_len, d_inner), BF16),
        scratch_shapes=[pltpu.VMEM((SSM_GROUPS, D_STATE, d_inner // SSM_GROUPS), F32)],
        compiler_params=_params("arbitrary", "arbitrary"),
        name="ssd_bwd" if rev else "ssd_fwd",
    )(*args)


def _odd_out_body(x_ref, y_ref, z_ref, mod_ref, gn_ref, n2_ref, wout_ref, wr_ref, x1_ref, h_ref, ti_ref, tw_ref):
    z = z_ref[0].astype(F32)
    yn = _rms(y_ref[0].astype(F32) * _silu(z), gn_ref[...])
    x1 = x_ref[0] + mod_ref[0, 0, 2:3, :] * _dot(yn.astype(BF16), wout_ref[...])
    x1_ref[0] = x1
    h = _rms(x1, n2_ref[...]) * (1.0 + mod_ref[0, 0, 4:5, :]) + mod_ref[0, 0, 3:4, :]
    h_ref[0] = h
    ne = wr_ref.shape[1]
    h1, h2, _ = _split3(h)
    w1, w2, _ = _split3(wr_ref[...])
    hw = _dot(h1, jnp.concatenate([w1, w2], axis=1))
    lg = hw[:, 0:ne] + hw[:, ne:2 * ne] + _dot(h2, w1)
    idx = lax.broadcasted_iota(jnp.int32, lg.shape, 1)
    m1 = jnp.max(lg, axis=-1, keepdims=True)
    i1 = jnp.min(jnp.where(lg == m1, idx, ne), axis=-1, keepdims=True)
    lg2 = jnp.where(idx == i1, -jnp.inf, lg)
    m2 = jnp.max(lg2, axis=-1, keepdims=True)
    i2 = jnp.min(jnp.where(lg2 == m2, idx, ne), axis=-1, keepdims=True)
    e2 = jnp.exp(m2 - m1)
    den = 1.0 + e2
    ti_ref[0] = jnp.concatenate([i1, i2], axis=1)
    tw_ref[0] = jnp.concatenate([1.0 / den, e2 / den], axis=1)


def _odd_out(x, y, z, mod, gnorm, norm2, w_out, w_router, ctx_len):
    bsz, t, d = x.shape
    seq_len = t - ctx_len
    di = y.shape[2]
    ne = w_router.shape[1]
    nct = ctx_len // TM
    lat = lambda b, i: (b, i + nct, 0)
    own = lambda b, i: (b, i, 0)
    return pl.pallas_call(
        _odd_out_body,
        grid=(bsz, seq_len // TM),
        in_specs=[pl.BlockSpec((1, TM, d), lat),
                  pl.BlockSpec((1, TM, di), own),
                  pl.BlockSpec((1, TM, di), lat),
                  pl.BlockSpec((1, 1, 6, d), lambda b, i: (b, 1, 0, 0)),
                  _resident((1, di)), _resident((1, d)), _resident(w_out.shape), _resident((d, ne))],
        out_specs=[pl.BlockSpec((1, TM, d), own), pl.BlockSpec((1, TM, d), own),
                   pl.BlockSpec((1, TM, TOP_K), own), pl.BlockSpec((1, TM, TOP_K), own)],
        out_shape=[jax.ShapeDtypeStruct((bsz, seq_len, d), F32), jax.ShapeDtypeStruct((bsz, seq_len, d), F32),
                   jax.ShapeDtypeStruct((bsz, seq_len, TOP_K), jnp.int32),
                   jax.ShapeDtypeStruct((bsz, seq_len, TOP_K), F32)],
        compiler_params=_params("arbitrary", "arbitrary"),
        name="odd_out",
    )(x, y, z, mod, gnorm.reshape(1, di), norm2.reshape(1, d), w_out, w_router)


def _expert_body(te_ref, na_ref, rows_ref, next_rows_ref, dst_ref, h_hbm, w1_ref, w3_ref, w2_ref, y_hbm,
                 xf_s, xb_s, acc_s, out_s, gsem, ssem, *, nf, nt):
    ti = pl.program_id(0)
    f = pl.program_id(1)
    real = ti < nt
    active = ti < na_ref[0]
    part = TM_E // nf
    base = f * part
    last = f == nf - 1

    def gather_row(idx_ref, q, j, priority):
        pltpu.make_async_copy(h_hbm.at[pl.ds(idx_ref[0, 0, q * part + j], 1), :],
                              xf_s.at[q, pl.ds(j, 1), :], gsem).start(priority=priority)

    @pl.when(jnp.logical_and(ti == 0, f == 0))
    def _():
        out_s[...] = jnp.zeros_like(out_s)
        for q in range(nf):
            def body(j, carry):
                gather_row(rows_ref, q, 2 * j, 0)
                gather_row(rows_ref, q, 2 * j + 1, 1)
                return carry
            lax.fori_loop(0, part // 2, body, 0)

    @pl.when(f == 0)
    def _():
        for q in range(nf):
            pltpu.make_async_copy(h_hbm.at[pl.ds(0, part), :], xf_s.at[q], gsem).wait()

    @pl.when(jnp.logical_and(active, f == 0))
    def _():
        for q in range(nf):
            xb_s[q * part:(q + 1) * part, :] = xf_s[q].astype(BF16)
        acc_s[...] = jnp.zeros_like(acc_s)

    def scatter_part():
        for j in range(part):
            pltpu.make_async_copy(out_s.at[f, pl.ds(j, 1), :],
                                  y_hbm.at[pl.ds(dst_ref[0, 0, base + j], 1), :], ssem).start(priority=j % 2)

    def gather_part():
        for j in range(part):
            gather_row(next_rows_ref, f, j, 1)

    @pl.when(active)
    def _():
        scatter_part()
        gather_part()
        xb = xb_s[...]
        u = _silu(_dot(xb, w1_ref[0])) * _dot(xb, w3_ref[0])
        acc_s[...] += _dot(u.astype(BF16), w2_ref[0])

    @pl.when(jnp.logical_and(real, jnp.logical_not(active)))
    def _():
        scatter_part()
        gather_part()

    @pl.when(jnp.logical_not(real))
    def _():
        scatter_part()

    @pl.when(last)
    def _():
        for q in range(nf):
            pltpu.make_async_copy(out_s.at[q], y_hbm.at[pl.ds(0, part), :], ssem).wait()

    @pl.when(jnp.logical_and(active, last))
    def _():
        for q in range(nf):
            out_s[q] = acc_s[q * part:(q + 1) * part, :]


def _expert_ffn(h, tile_expert, n_active, row_token, row_dst, n_y_rows, w1, w3, w2):
    n, d = h.shape
    ns = row_token.shape[0]
    nt = ns // TM_E
    dff = w1.shape[2]
    nf = dff // TF_E

    def wcol(ti, f, te, na):
        return (te[jnp.minimum(ti, nt - 1)], 0, jnp.where(ti < na[0], f, nf - 1))

    def wrow(ti, f, te, na):
        return (te[jnp.minimum(ti, nt - 1)], jnp.where(ti < na[0], f, nf - 1), 0)

    def idx_block(shift):
        return pl.BlockSpec((1, 1, TM_E), lambda ti, f, te, na: (jnp.minimum(ti + shift, nt), 0, 0),
                            memory_space=pltpu.SMEM)

    grid_spec = pltpu.PrefetchScalarGridSpec(
        num_scalar_prefetch=2,
        grid=(nt + 1, nf),
        in_specs=[idx_block(0), idx_block(1), idx_block(0),
                  pl.BlockSpec(memory_space=pl.ANY),
                  pl.BlockSpec((1, d, TF_E), wcol),
                  pl.BlockSpec((1, d, TF_E), wcol),
                  pl.BlockSpec((1, TF_E, d), wrow)],
        out_specs=pl.BlockSpec(memory_space=pl.ANY),
        scratch_shapes=[pltpu.VMEM((nf, TM_E // nf, d), F32), pltpu.VMEM((TM_E, d), BF16),
                        pltpu.VMEM((TM_E, d), F32), pltpu.VMEM((nf, TM_E // nf, d), F32),
                        pltpu.SemaphoreType.DMA(()), pltpu.SemaphoreType.DMA(())],
    )
    rows = jnp.concatenate([row_token, jnp.zeros((TM_E,), jnp.int32)]).reshape(nt + 1, 1, TM_E)
    dst = row_dst.reshape(nt + 1, 1, TM_E)
    return pl.pallas_call(
        functools.partial(_expert_body, nf=nf, nt=nt),
        grid_spec=grid_spec,
        out_shape=jax.ShapeDtypeStruct((n_y_rows, d), F32),
        compiler_params=_params("arbitrary", "arbitrary"),
        name="expert_ffn",
    )(tile_expert, n_active, rows, rows, dst, h, w1, w3, w2)


def _combine_body(x_ref, y0_ref, y1_ref, tw_ref, g2_ref, fn_ref, o_ref):
    tw = tw_ref[...]
    moe = tw[:, 0:1] * y0_ref[...] + tw[:, 1:2] * y1_ref[...]
    o_ref[...] = _rms(x_ref[...] + g2_ref[0] * moe, fn_ref[...])


def _moe_combine(x1, topw, g2, final_norm, y, tiles_per_batch):
    n, d = x1.shape
    nt = n // TM_C
    return pl.pallas_call(
        _combine_body,
        grid=(nt,),
        in_specs=[pl.BlockSpec((TM_C, d), lambda i: (i, 0)),
                  pl.BlockSpec((TM_C, d), lambda i: (i, 0)),
                  pl.BlockSpec((TM_C, d), lambda i: (i + nt, 0)),
                  pl.BlockSpec((TM_C, TOP_K), lambda i: (i, 0)),
                  pl.BlockSpec((1, 1, d), lambda i: (i // tiles_per_batch, 0, 0)),
                  _resident((1, d))],
        out_specs=pl.BlockSpec((TM_C, d), lambda i: (i, 0)),
        out_shape=jax.ShapeDtypeStruct((n, d), F32),
        compiler_params=_params("arbitrary"),
        name="moe_combine",
    )(x1, y, y, topw, g2, final_norm.reshape(1, d))


def _route(topi, n_slots):
    n = topi.shape[0]
    e_flat = topi.reshape(n * TOP_K)
    oh = (e_flat[:, None] == jnp.arange(N_EXPERTS, dtype=jnp.int32)[None, :]).astype(jnp.int32)
    csum = jnp.cumsum(oh, axis=0)
    rank = jnp.sum(csum * oh, axis=1) - 1
    cnt = csum[-1]
    cnt_pad = (cnt + TM_E - 1) // TM_E * TM_E
    ends = jnp.cumsum(cnt_pad)
    offs = ends - cnt_pad
    slot = jnp.sum(oh * offs[None, :], axis=1) + rank
    n_active = (ends[-1] // TM_E).astype(jnp.int32)
    nt = n_slots // TM_E
    tile_start = jnp.arange(nt, dtype=jnp.int32) * TM_E
    te = jnp.sum((tile_start[:, None] >= ends[None, :]).astype(jnp.int32), axis=1)
    te = jnp.minimum(te, N_EXPERTS - 1)
    te = jnp.where(jnp.arange(nt) < n_active, te, te[jnp.maximum(n_active - 1, 0)])
    n_routed = n * TOP_K
    owner = jnp.full((n_slots,), -1, jnp.int32).at[slot].set(jnp.arange(n_routed, dtype=jnp.int32))
    is_pad = owner < 0
    spare = n_routed + TM_E + jnp.cumsum(is_pad.astype(jnp.int32)) - 1
    row_token = jnp.where(is_pad, 0, owner // TOP_K)
    routed = (owner % TOP_K) * n + owner // TOP_K
    row_dst = jnp.concatenate([n_routed + jnp.arange(TM_E, dtype=jnp.int32), jnp.where(is_pad, spare, routed)])
    n_y_rows = n_slots + TM_E
    return te.astype(jnp.int32), n_active.reshape(1), row_token, row_dst, n_y_rows


def kernel(x, c, ctx, c_ctx, ev_ada_w, ev_ada_b, ev_norm1, ev_norm2, ev_w_in, ev_q_gain, ev_k_gain, ev_dw_w, ev_dw_b, ev_ln_g, ev_ln_b, ev_w_o, ev_ff_w1, ev_ff_w3, ev_ff_w2, od_ada_w, od_ada_b, od_norm1, od_norm2, od_w_in, od_conv_w, od_conv_b, od_a_log_f, od_a_log_b, od_dt_bias_f, od_dt_bias_b, od_d_skip, od_gnorm, od_w_out, od_router, od_ex_w1, od_ex_w3, od_ex_w2, final_norm):
    bsz, seq_len, d = x.shape
    ctx_len = ctx.shape[1]
    assert ev_ada_w.shape[0] == 1 and od_ada_w.shape[0] == 1, "one even and one odd layer"
    assert ctx_len % TM == 0 and seq_len % TM == 0 and seq_len % GRID_W == 0
    n_ctx_tiles = ctx_len // TM
    mod = _mod_table(c, c_ctx, ev_ada_w[0], ev_ada_b[0])
    cc = ev_dw_w.shape[2]
    splits = [(0, ATTN_W), (ATTN_W, ATTN_W + KV_W), (ATTN_W + KV_W, ATTN_W + 2 * KV_W),
              (ATTN_W + 2 * KV_W, ATTN_W + 2 * KV_W + 2 * cc)]
    q, k, v, u = _inproj((ctx, x), mod, ev_norm1[0], ev_w_in[0].astype(BF16), splits, (F32, F32, BF16, F32),
                         n_ctx_tiles)
    cos2, sin2 = _rope_tables(ctx_len, seq_len)
    attn = _attention(q, k, v, cos2, sin2, ev_q_gain[0], ev_k_gain[0], ctx_len)
    conv = _conformer_conv(u, ev_dw_w[0], ev_dw_b[0], ev_ln_g[0], ev_ln_b[0], ctx_len)
    xa, (ex_w1, ex_w3, ex_w2) = _even_ffn(
        (ctx, x), attn, conv, mod, ev_norm2[0], ev_w_o[0].astype(BF16), ev_ff_w1[0].astype(BF16),
        ev_ff_w3[0].astype(BF16), ev_ff_w2[0].astype(BF16), n_ctx_tiles, (od_ex_w1[0], od_ex_w3[0], od_ex_w2[0]))

    mod = _mod_table(c, c_ctx, od_ada_w[0], od_ada_b[0])
    nh = od_a_log_f.shape[1]
    d_inner = nh * SSM_HEADDIM
    conv_dim = od_conv_w.shape[2]
    splits = [(0, d_inner), (d_inner, d_inner + conv_dim), (d_inner + conv_dim, d_inner + conv_dim + 2 * nh)]
    z, xbc, dt = _inproj((xa,), mod, od_norm1[0], od_w_in[0].astype(BF16), splits, (BF16, F32, F32), n_ctx_tiles)
    xbc = _ssm_conv(xbc, od_conv_w[0], od_conv_b[0], ctx_len)
    dtt = dt.transpose(0, 2, 1)
    y_b = _ssd_scan(xbc, dt, dtt, od_a_log_b[0], od_dt_bias_b[0], ctx_len, rev=True)
    y = _ssd_scan(xbc, dt, dtt, od_a_log_f[0], od_dt_bias_f[0], ctx_len, rev=False, y_other=y_b,
                  d_skip=od_d_skip[0])
    x1, h, topi, topw = _odd_out(xa, y, z, mod, od_gnorm[0], od_norm2[0], od_w_out[0].astype(BF16),
                                 od_router[0], ctx_len)

    n = bsz * seq_len
    n_slots = n * TOP_K + N_EXPERTS * TM_E
    tile_expert, n_active, row_token, row_dst, n_y_rows = _route(topi.reshape(n, TOP_K), n_slots)
    y_moe = _expert_ffn(h.reshape(n, d), tile_expert, n_active, row_token, row_dst, n_y_rows,
                        ex_w1, ex_w3, ex_w2)
    g2 = mod[:, 1, 5:6, :]
    out = _moe_combine(x1.reshape(n, d), topw.reshape(n, TOP_K), g2, final_norm, y_moe, seq_len // TM_C)
    return out.reshape(bsz, seq_len, d)
```

```python
import functools

import jax
import jax.numpy as jnp
from jax import lax
from jax.experimental import pallas as pl
from jax.experimental.pallas import tpu as pltpu

F32 = jnp.float32
BF16 = jnp.bfloat16
EPS = 1e-6
LOG2_E = 1.4426950408889634

GRID_W = 64
HEAD_DIM = 64
ATTN_HEADS = 8
KV_HEADS = 2
ATTN_W = ATTN_HEADS * HEAD_DIM
KV_W = KV_HEADS * HEAD_DIM
ROPE_THETA = 10000.0
SSM_HEADDIM = 64
SSM_GROUPS = 4
D_STATE = 128
CHUNK = 128
N_EXPERTS = 8
TOP_K = 2

LANES = 128
SUBLANES = 8
VMEM_LIMIT = 56 * 1024 * 1024

TM = 256
TM_E = 512
TF_E = 1792
TM_C = 256
SSD_CPS = 2
ATTN_ROWS = 256


def _params(*sem):
    return pltpu.CompilerParams(dimension_semantics=sem, vmem_limit_bytes=VMEM_LIMIT)


def _resident(shape):
    nd = len(shape)
    return pl.BlockSpec(shape, lambda *_: (0,) * nd, pipeline_mode=pl.Buffered(1))


def _silu(x):
    return x * jax.nn.sigmoid(x)


def _rms(x, g):
    return x * lax.rsqrt(jnp.mean(x * x, axis=-1, keepdims=True) + EPS) * g


def _dot(a, b):
    return jnp.dot(a, b, preferred_element_type=F32)


def _split3(a):
    a1 = a.astype(BF16)
    r1 = a - a1.astype(F32)
    a2 = r1.astype(BF16)
    a3 = (r1 - a2.astype(F32)).astype(BF16)
    return a1, a2, a3


def _ada_body(c_ref, w_ref, b_ref, o_ref):
    s = _silu(c_ref[...]).astype(BF16)
    o_ref[...] = _dot(s, w_ref[...].astype(BF16)) + b_ref[...]


def _adaln(cond, w, b):
    r, d = cond.shape
    n = w.shape[1]
    tn = n // 4
    return pl.pallas_call(
        _ada_body,
        grid=(n // tn,),
        in_specs=[pl.BlockSpec((r, d), lambda j: (0, 0)),
                  pl.BlockSpec((d, tn), lambda j: (0, j)),
                  pl.BlockSpec((1, tn), lambda j: (0, j))],
        out_specs=pl.BlockSpec((r, tn), lambda j: (0, j)),
        out_shape=jax.ShapeDtypeStruct((r, n), F32),
        compiler_params=_params("arbitrary"),
        name="adaln",
    )(cond, w, b.reshape(1, n))


def _mod_table(c, c_ctx, w, b):
    bsz, d = c.shape
    rows = -(-(bsz + 1) // SUBLANES) * SUBLANES
    cond = jnp.zeros((rows, d), F32).at[:bsz].set(c).at[bsz].set(c_ctx)
    m = _adaln(cond, w, b)
    lat = m[:bsz]
    ctx = jnp.broadcast_to(m[bsz][None], lat.shape)
    return jnp.stack([ctx, lat], axis=1).reshape(bsz, 2, 6, d)


def _token_specs(parts, n_ctx_tiles):
    d = parts[0].shape[2]
    if len(parts) == 1:
        return [pl.BlockSpec((1, TM, d), lambda b, i: (b, i, 0))]
    return [pl.BlockSpec((1, TM, d), lambda b, i: (b, jnp.minimum(i, n_ctx_tiles - 1), 0)),
            pl.BlockSpec((1, TM, d), lambda b, i: (b, jnp.maximum(i - n_ctx_tiles, 0), 0))]


def _token_tile(refs, n_ctx_tiles):
    if len(refs) == 1:
        return refs[0][0]
    return jnp.where(pl.program_id(1) < n_ctx_tiles, refs[0][0], refs[1][0])


def _inproj_body(*refs, splits, n_parts, n_ctx_tiles, also_transposed):
    x = _token_tile(refs[:n_parts], n_ctx_tiles)
    mod_ref, n_ref, w_ref = refs[n_parts:n_parts + 3]
    out_refs = refs[n_parts + 3:]
    h = _rms(x, n_ref[...]) * (1.0 + mod_ref[0, 0, 1:2, :]) + mod_ref[0, 0, 0:1, :]
    r = _dot(h.astype(BF16), w_ref[...])
    for o_ref, (lo, hi) in zip(out_refs, splits):
        o_ref[0] = r[:, lo:hi].astype(o_ref.dtype)
    if also_transposed:
        lo, hi = splits[-1]
        out_refs[-1][0] = r[:, lo:hi].T


def _inproj(parts, mod, norm, w_bf16, splits, dtypes, n_ctx_tiles, also_transposed=False):
    bsz, _, d = parts[0].shape
    t = sum(p.shape[1] for p in parts)
    n = w_bf16.shape[1]
    widths = [hi - lo for lo, hi in splits]
    out_specs = [pl.BlockSpec((1, TM, wd), lambda b, i: (b, i, 0)) for wd in widths]
    out_shape = [jax.ShapeDtypeStruct((bsz, t, wd), dt) for wd, dt in zip(widths, dtypes)]
    if also_transposed:
        out_specs.append(pl.BlockSpec((1, widths[-1], TM), lambda b, i: (b, 0, i)))
        out_shape.append(jax.ShapeDtypeStruct((bsz, widths[-1], t), F32))
    return pl.pallas_call(
        functools.partial(_inproj_body, splits=tuple(splits), n_parts=len(parts), n_ctx_tiles=n_ctx_tiles,
                          also_transposed=also_transposed),
        grid=(bsz, t // TM),
        in_specs=_token_specs(parts, n_ctx_tiles) + [
            pl.BlockSpec((1, 1, 6, d), lambda b, i: (b, (i >= n_ctx_tiles).astype(jnp.int32), 0, 0)),
            _resident((1, d)),
            _resident((d, n))],
        out_specs=out_specs,
        out_shape=out_shape,
        compiler_params=_params("arbitrary", "arbitrary"),
        name="inproj",
    )(*parts, mod, norm.reshape(1, d), w_bf16)


def _attn_body(q_ref, k_ref, v_ref, cq_ref, sq_ref, ck_ref, sk_ref, qg_ref, kg_ref, o_ref, kt_s, v2_s,
               *, n_ctx_tiles, ctx_len):
    i = pl.program_id(1)
    t = k_ref.shape[1]
    lane = lax.broadcasted_iota(jnp.int32, (1, LANES), 1)
    low = lane < HEAD_DIM
    even = (lane % 2) == 0

    def norm_rope(x, gain, cos, sin):
        x2 = x * x
        s_lo = jnp.sum(jnp.where(low, x2, 0.0), axis=-1, keepdims=True)
        s_hi = jnp.sum(jnp.where(low, 0.0, x2), axis=-1, keepdims=True)
        ms = jnp.where(low, s_lo, s_hi) * (1.0 / HEAD_DIM)
        xn = x * lax.rsqrt(ms + EPS) * gain
        swapped = jnp.where(even, pltpu.roll(xn, LANES - 1, 1), pltpu.roll(xn, 1, 1))
        return xn * cos + swapped * sin

    @pl.when(i == 0)
    def _():
        k = norm_rope(k_ref[0], kg_ref[...], ck_ref[...], sk_ref[...])
        kr = pltpu.roll(k, HEAD_DIM, 1)
        kt_s[0] = jnp.where(low, k, kr).T.astype(BF16)
        kt_s[1] = jnp.where(low, kr, k).T.astype(BF16)
        v = v_ref[0].astype(F32)
        vr = pltpu.roll(v, HEAD_DIM, 1)
        v2_s[0] = jnp.where(low, v, vr).astype(BF16)
        v2_s[1] = jnp.where(low, vr, v).astype(BF16)

    def run(tk):
        tq = q_ref.shape[1]
        for j in range(ATTN_HEADS // 2):
            g = (2 * j) // (ATTN_HEADS // KV_HEADS)
            qp = norm_rope(q_ref[0, :, j * LANES:(j + 1) * LANES], qg_ref[...], cq_ref[...], sq_ref[...])
            qp = qp * (HEAD_DIM ** -0.5 * LOG2_E)
            for r0 in range(0, tq, ATTN_ROWS):
                outs = []
                for hh in range(2):
                    qm = jnp.where(low if hh == 0 else jnp.logical_not(low), qp[r0:r0 + ATTN_ROWS], 0.0).astype(BF16)
                    s = _dot(qm, kt_s[g, :, 0:tk])
                    m = jnp.max(s, axis=-1, keepdims=True)
                    p = jnp.exp2(s - m)
                    l = jnp.sum(p, axis=-1, keepdims=True)
                    o = _dot(p.astype(BF16), v2_s[g, 0:tk, :])
                    outs.append(o / l)
                o_ref[0, r0:r0 + ATTN_ROWS, j * LANES:(j + 1) * LANES] = jnp.where(low, outs[0], outs[1]).astype(
                    o_ref.dtype)

    @pl.when(i < n_ctx_tiles)
    def _():
        run(ctx_len)

    @pl.when(i >= n_ctx_tiles)
    def _():
        run(t)


def _attention(q, k, v, cos2, sin2, q_gain, k_gain, ctx_len):
    bsz, t, _ = q.shape
    n_ctx_tiles = ctx_len // TM
    qg = jnp.tile(q_gain.reshape(1, HEAD_DIM), (1, 2))
    kg = jnp.tile(k_gain.reshape(1, HEAD_DIM), (1, 2))
    return pl.pallas_call(
        functools.partial(_attn_body, n_ctx_tiles=n_ctx_tiles, ctx_len=ctx_len),
        grid=(bsz, t // TM),
        in_specs=[pl.BlockSpec((1, TM, ATTN_W), lambda b, i: (b, i, 0)),
                  pl.BlockSpec((1, t, KV_W), lambda b, i: (b, 0, 0)),
                  pl.BlockSpec((1, t, KV_W), lambda b, i: (b, 0, 0)),
                  pl.BlockSpec((TM, LANES), lambda b, i: (i, 0)),
                  pl.BlockSpec((TM, LANES), lambda b, i: (i, 0)),
                  _resident((t, LANES)),
                  _resident((t, LANES)),
                  _resident((1, LANES)),
                  _resident((1, LANES))],
        out_specs=pl.BlockSpec((1, TM, ATTN_W), lambda b, i: (b, i, 0)),
        out_shape=jax.ShapeDtypeStruct((bsz, t, ATTN_W), BF16),
        scratch_shapes=[pltpu.VMEM((KV_HEADS, LANES, t), BF16), pltpu.VMEM((KV_HEADS, t, LANES), BF16)],
        compiler_params=_params("arbitrary", "arbitrary"),
        name="attention",
    )(q, k, v, cos2, sin2, cos2, sin2, qg, kg)


def _rope_tables(ctx_len, seq_len):
    rows = seq_len // GRID_W
    t_row = jnp.repeat(jnp.arange(rows, dtype=F32), GRID_W)
    t_col = jnp.tile(jnp.arange(GRID_W, dtype=F32), rows)
    axis_dim = HEAD_DIM // 2
    inv_freq = ROPE_THETA ** (-jnp.arange(0, axis_dim, 2, dtype=F32) / axis_dim)
    ang = jnp.concatenate([t_row[:, None] * inv_freq, t_col[:, None] * inv_freq], axis=-1)
    cos = jnp.repeat(jnp.cos(ang), 2, axis=-1)
    sin = jnp.repeat(jnp.sin(ang), 2, axis=-1) * jnp.tile(jnp.array([-1.0, 1.0], F32), axis_dim)
    cos = jnp.concatenate([jnp.ones((ctx_len, HEAD_DIM), F32), cos], axis=0)
    sin = jnp.concatenate([jnp.zeros((ctx_len, HEAD_DIM), F32), sin], axis=0)
    return jnp.tile(cos, (1, 2)), jnp.tile(sin, (1, 2))


def _fill_padded(src_ref, pad_s, width, ctx_len, seq_len, pad, rows, fn):
    t = ctx_len + seq_len
    z = jnp.zeros((pad + SUBLANES, width), F32)
    pad_s[0:pad, :] = z[0:pad]
    pad_s[pad + ctx_len:2 * pad + ctx_len, :] = z[0:pad]
    pad_s[2 * pad + t:3 * pad + t + SUBLANES, :] = z

    def seg(tok0, off, ntiles):
        def body(n, carry):
            r = pl.multiple_of(tok0 + n * rows, rows)
            pad_s[pl.ds(r + off, rows), :] = fn(src_ref, r, rows)
            return carry
        lax.fori_loop(0, ntiles, body, 0)

    seg(0, pad, ctx_len // rows)
    seg(ctx_len, 2 * pad, seq_len // rows)


def _conv_segments(pad_s, w_ref, taps, ctx_len, seq_len, pad, rows, emit):
    half = taps // 2
    span = rows + SUBLANES

    def seg(tok0, off, ntiles):
        def body(n, carry):
            r = pl.multiple_of(tok0 + n * rows, rows)
            base = r + off - pad
            acc = None
            for rho in range(SUBLANES):
                part = None
                for j in range(taps):
                    dj = pad - half + j
                    if dj % SUBLANES != rho:
                        continue
                    term = w_ref[j:j + 1, :] * pad_s[pl.ds(pl.multiple_of(base + (dj - rho), SUBLANES), span), :]
                    part = term if part is None else part + term
                if part is None:
                    continue
                if rho:
                    part = pltpu.roll(part, span - rho, 0)
                acc = part[0:rows] if acc is None else acc + part[0:rows]
            emit(r, acc)
            return carry
        lax.fori_loop(0, ntiles, body, 0)

    seg(0, pad, ctx_len // rows)
    seg(ctx_len, 2 * pad, seq_len // rows)


CC_PAD = 16
CC_ROWS = 32


def _cconv_body(u_ref, w_ref, b_ref, g_ref, bb_ref, o_ref, pad_s, *, ctx_len, seq_len):
    c = o_ref.shape[2]

    def glu(src_ref, r, rows):
        return src_ref[0, pl.ds(r, rows), 0:c] * jax.nn.sigmoid(src_ref[0, pl.ds(r, rows), c:2 * c])

    _fill_padded(u_ref, pad_s, c, ctx_len, seq_len, CC_PAD, TM, glu)

    def emit(r, acc):
        h = acc + b_ref[...]
        mu = jnp.mean(h, axis=-1, keepdims=True)
        hc = h - mu
        var = jnp.mean(hc * hc, axis=-1, keepdims=True)
        y = hc * lax.rsqrt(var + EPS) * g_ref[...] + bb_ref[...]
        o_ref[0, pl.ds(r, CC_ROWS), :] = _silu(y).astype(o_ref.dtype)

    _conv_segments(pad_s, w_ref, w_ref.shape[0], ctx_len, seq_len, CC_PAD, CC_ROWS, emit)


def _conformer_conv(u, dw_w, dw_b, ln_g, ln_b, ctx_len):
    bsz, t, c2 = u.shape
    c = c2 // 2
    taps = dw_w.shape[0]
    return pl.pallas_call(
        functools.partial(_cconv_body, ctx_len=ctx_len, seq_len=t - ctx_len),
        grid=(bsz,),
        in_specs=[pl.BlockSpec((1, t, c2), lambda b: (b, 0, 0)),
                  _resident((taps, c)), _resident((1, c)), _resident((1, c)), _resident((1, c))],
        out_specs=pl.BlockSpec((1, t, c), lambda b: (b, 0, 0)),
        out_shape=jax.ShapeDtypeStruct((bsz, t, c), BF16),
        scratch_shapes=[pltpu.VMEM((t + 3 * CC_PAD + SUBLANES, c), F32)],
        compiler_params=_params("arbitrary"),
        name="conformer_conv",
    )(u, dw_w, dw_b.reshape(1, c), ln_g.reshape(1, c), ln_b.reshape(1, c))


SC_PAD = 8
SC_ROWS = 64
SC_COLS = 512


def _sconv_body(x_ref, w_ref, b_ref, o_ref, pad_s, *, ctx_len, seq_len):
    def ident(src_ref, r, rows):
        return src_ref[0, pl.ds(r, rows), :]

    _fill_padded(x_ref, pad_s, SC_COLS, ctx_len, seq_len, SC_PAD, TM, ident)

    def emit(r, acc):
        o_ref[0, pl.ds(r, SC_ROWS), :] = _silu(acc + b_ref[...]).astype(o_ref.dtype)

    _conv_segments(pad_s, w_ref, w_ref.shape[0], ctx_len, seq_len, SC_PAD, SC_ROWS, emit)


def _ssm_conv(xbc, conv_w, conv_b, ctx_len):
    bsz, t, c = xbc.shape
    taps = conv_w.shape[0]
    return pl.pallas_call(
        functools.partial(_sconv_body, ctx_len=ctx_len, seq_len=t - ctx_len),
        grid=(bsz, c // SC_COLS),
        in_specs=[pl.BlockSpec((1, t, SC_COLS), lambda b, j: (b, 0, j)),
                  pl.BlockSpec((taps, SC_COLS), lambda b, j: (0, j)),
                  pl.BlockSpec((1, SC_COLS), lambda b, j: (0, j))],
        out_specs=pl.BlockSpec((1, t, SC_COLS), lambda b, j: (b, 0, j)),
        out_shape=jax.ShapeDtypeStruct((bsz, t, c), BF16),
        scratch_shapes=[pltpu.VMEM((t + 3 * SC_PAD + SUBLANES, SC_COLS), F32)],
        compiler_params=_params("arbitrary", "arbitrary"),
        name="ssm_conv",
    )(xbc, conv_w, conv_b.reshape(1, c))


def _even_ffn_body(*refs, n_parts, n_ctx_tiles, n_cast):
    x = _token_tile(refs[:n_parts], n_ctx_tiles)
    a_ref, c_ref, mod_ref, n2_ref, wo_ref, w1_ref, w3_ref, w2_ref = refs[n_parts:n_parts + 8]
    cast_in = refs[n_parts + 8:n_parts + 8 + n_cast]
    o_ref = refs[n_parts + 8 + n_cast]
    cast_out = refs[n_parts + 9 + n_cast:]
    for src, dst in zip(cast_in, cast_out):
        dst[...] = src[...].astype(dst.dtype)
    ca = a_ref.shape[2]
    mix = _dot(a_ref[0], wo_ref[0:ca, :]) + _dot(c_ref[0], wo_ref[ca:, :])
    x1 = x + mod_ref[0, 0, 2:3, :] * mix
    h = _rms(x1, n2_ref[...]) * (1.0 + mod_ref[0, 0, 4:5, :]) + mod_ref[0, 0, 3:4, :]
    hb = h.astype(BF16)
    u = _silu(_dot(hb, w1_ref[...])) * _dot(hb, w3_ref[...])
    o_ref[0] = x1 + mod_ref[0, 0, 5:6, :] * _dot(u.astype(BF16), w2_ref[...])


CAST_ROW_BLOCKS = 8


def _even_ffn(parts, attn, conv, mod, norm2, wo, w1, w3, w2, n_ctx_tiles, to_cast):
    bsz, t, ca = attn.shape
    d = parts[0].shape[2]
    cc = conv.shape[2]
    nt = t // TM
    row_blocks = max([r for r in (CAST_ROW_BLOCKS, 4, 2, 1) if to_cast[0].shape[0] * r <= bsz * nt], default=0)
    if row_blocks == 0:
        extra, to_cast = tuple(w.astype(BF16) for w in to_cast), ()
    else:
        extra = ()
    n_blocks = (to_cast[0].shape[0] if to_cast else 0) * row_blocks

    def cast_block(b, i):
        blk = jnp.minimum(b * nt + i, n_blocks - 1)
        return (blk // row_blocks, blk % row_blocks, 0)

    cast_specs = [pl.BlockSpec((1, w.shape[1] // row_blocks, w.shape[2]), cast_block) for w in to_cast]
    res = pl.pallas_call(
        functools.partial(_even_ffn_body, n_parts=len(parts), n_ctx_tiles=n_ctx_tiles, n_cast=len(to_cast)),
        grid=(bsz, nt),
        in_specs=_token_specs(parts, n_ctx_tiles) + [
                  pl.BlockSpec((1, TM, ca), lambda b, i: (b, i, 0)),
                  pl.BlockSpec((1, TM, cc), lambda b, i: (b, i, 0)),
                  pl.BlockSpec((1, 1, 6, d), lambda b, i: (b, (i >= n_ctx_tiles).astype(jnp.int32), 0, 0)),
                  _resident((1, d)), _resident(wo.shape), _resident(w1.shape), _resident(w3.shape),
                  _resident(w2.shape)] + cast_specs,
        out_specs=[pl.BlockSpec((1, TM, d), lambda b, i: (b, i, 0))] + cast_specs,
        out_shape=[jax.ShapeDtypeStruct((bsz, t, d), F32)] + [jax.ShapeDtypeStruct(w.shape, BF16) for w in to_cast],
        compiler_params=_params("arbitrary", "arbitrary"),
        name="even_ffn",
    )(*parts, attn, conv, mod, norm2.reshape(1, d), wo, w1, w3, w2, *to_cast)
    return res[0], tuple(res[1:]) + extra


def _softplus(x):
    return jnp.maximum(x, 0.0) + jnp.log(1.0 + jnp.exp(-jnp.abs(x)))


def _ssd_body(xs_ref, bm_ref, cm_ref, dt_ref, dtt_ref, alr_ref, alc_ref, bir_ref, bic_ref, exp_ref, sel_ref,
              *rest, rev, n_ctx_steps, combine):
    if combine:
        yb_ref, dsk_ref, y_ref, st_s = rest
    else:
        y_ref, st_s = rest
    c = pl.program_id(1)
    nh = alr_ref.shape[1]
    hpg = nh // SSM_GROUPS
    gw = hpg * SSM_HEADDIM
    end = 0 if rev else CHUNK - 1

    @pl.when(c == 0)
    def _():
        st_s[...] = jnp.zeros_like(st_s)

    off = nh if rev else 0
    row = lax.broadcasted_iota(jnp.int32, (CHUNK, CHUNK), 0)
    col = lax.broadcasted_iota(jnp.int32, (CHUNK, CHUNK), 1)
    tri = (col >= row) if rev else (col <= row)
    trit = (row >= col) if rev else (row <= col)
    tri_b = tri.astype(BF16)
    trit_b = trit.astype(BF16)
    lane = lax.broadcasted_iota(jnp.int32, (1, LANES), 1)
    low = lane < SSM_HEADDIM

    def cat3(v):
        return jnp.concatenate(_split3(v), axis=1)

    def step(with_y, k):
        rows = slice(k * CHUNK, (k + 1) * CHUNK)
        dt = _softplus(dt_ref[0, rows, off:off + nh] + bir_ref[...])
        dtt = _softplus(dtt_ref[0, off:off + nh, rows] + bic_ref[...])
        a = dt * (-jnp.exp(alr_ref[...]))
        at = dtt * (-jnp.exp(alc_ref[...]))
        acs = sum(_dot(tri_b, p) for p in _split3(a))
        acst = sum(_dot(p, trit_b) for p in _split3(at))
        spread = _dot(cat3(jnp.concatenate([dt, acs], axis=0)), exp_ref[...])
        dtx = spread[0:CHUNK]
        acx = spread[CHUNK:2 * CHUNK]
        totx = acx[end:end + 1, :]
        dec = jnp.exp(totx)
        xs = xs_ref[0, rows, :].astype(F32)
        xcd = (xs * (dtx * jnp.exp(totx - acx))).astype(BF16)
        if with_y:
            xc = (xs * dtx).astype(BF16)
            eax = jnp.exp(acx)
            acs_l = _dot(cat3(acs), sel_ref[...])
        for g in range(SSM_GROUPS):
            gs = slice(g * gw, (g + 1) * gw)
            bm = bm_ref[0, rows, g * D_STATE:(g + 1) * D_STATE]
            bmt = bm.astype(F32).T.astype(BF16)
            sg = st_s[g]
            st_s[g] = sg * dec[:, gs] + _dot(bmt, xcd[:, gs])

            if not with_y:
                continue
            cmb = cm_ref[0, rows, g * D_STATE:(g + 1) * D_STATE].astype(BF16)
            cb = _dot(cmb, bmt)
            yoff = _dot(cmb, sg.astype(BF16))
            for kp in range(hpg // 2):
                e0 = g * hpg + 2 * kp
                c0 = e0 * SSM_HEADDIM
                xcb = xc[:, c0:c0 + LANES]
                res = []
                for e in (e0, e0 + 1):
                    diff = acs_l[:, e * LANES:(e + 1) * LANES] - acst[e:e + 1, :]
                    m = (jnp.exp(jnp.where(tri, diff, -jnp.inf)) * cb).astype(BF16)
                    res.append(_dot(m, xcb))
                y = jnp.where(low, res[0], res[1]) + yoff[:, kp * LANES:(kp + 1) * LANES] * eax[:, c0:c0 + LANES]
                if combine:
                    y = (y + yb_ref[0, rows, c0:c0 + LANES].astype(F32)
                         + dsk_ref[:, c0:c0 + LANES] * xs[:, c0:c0 + LANES])
                y_ref[0, rows, c0:c0 + LANES] = y.astype(y_ref.dtype)

    order = range(SSD_CPS - 1, -1, -1) if rev else range(SSD_CPS)

    @pl.when(c >= n_ctx_steps)
    def _():
        for k in order:
            step(True, k)

    @pl.when(c < n_ctx_steps)
    def _():
        for k in order:
            step(False, k)


def _ssd_scan(xbc, dt, dtt, a_log, dt_bias, ctx_len, rev, y_other=None, d_skip=None):
    bsz, t, _ = xbc.shape
    nh = a_log.shape[0]
    d_inner = nh * SSM_HEADDIM
    gn = SSM_GROUPS * D_STATE
    rows = SSD_CPS * CHUNK
    assert t % rows == 0 and ctx_len % rows == 0
    nc = t // rows
    ncc = ctx_len // rows
    seq_len = t - ctx_len
    combine = y_other is not None

    if rev:
        def chunk(i):
            return jnp.where(i < ncc, ncc - 1 - i, nc - 1 + ncc - i)
    else:
        def chunk(i):
            return i

    def ychunk(i):
        return jnp.maximum(chunk(i), ncc) - ncc if not rev else jnp.where(i < ncc, nc - 1 - ncc, chunk(i) - ncc)

    bcol = d_inner // gn
    in_specs = [pl.BlockSpec((1, rows, d_inner), lambda b, i: (b, chunk(i), 0)),
                pl.BlockSpec((1, rows, gn), lambda b, i: (b, chunk(i), bcol)),
                pl.BlockSpec((1, rows, gn), lambda b, i: (b, chunk(i), bcol + 1)),
                pl.BlockSpec((1, rows, 2 * nh), lambda b, i: (b, chunk(i), 0)),
                pl.BlockSpec((1, 2 * nh, rows), lambda b, i: (b, 0, chunk(i))),
                _resident((1, nh)), _resident((nh, 1)), _resident((1, nh)), _resident((nh, 1)),
                _resident((3 * nh, d_inner)), _resident((3 * nh, nh * LANES))]
    heads = jnp.arange(nh, dtype=jnp.int32)[:, None]
    spread_p = jnp.tile((jnp.arange(d_inner, dtype=jnp.int32)[None, :] // SSM_HEADDIM == heads).astype(BF16), (3, 1))
    spread_l = jnp.tile((jnp.arange(nh * LANES, dtype=jnp.int32)[None, :] // LANES == heads).astype(BF16), (3, 1))
    args = [xbc, xbc, xbc, dt, dtt, a_log.reshape(1, nh), a_log.reshape(nh, 1),
            dt_bias.reshape(1, nh), dt_bias.reshape(nh, 1), spread_p, spread_l]
    if combine:
        in_specs += [pl.BlockSpec((1, rows, d_inner), lambda b, i: (b, ychunk(i), 0)), _resident((1, d_inner))]
        args += [y_other, jnp.repeat(d_skip, SSM_HEADDIM).reshape(1, d_inner)]
    return pl.pallas_call(
        functools.partial(_ssd_body, rev=rev, n_ctx_steps=ncc, combine=combine),
        grid=(bsz, nc),
        in_specs=in_specs,
        out_specs=pl.BlockSpec((1, rows, d_inner), lambda b, i: (b, ychunk(i), 0)),
        out_shape=jax.ShapeDtypeStruct((bsz, seq_len, d_inner), BF16),
        scratch_shapes=[pltpu.VMEM((SSM_GROUPS, D_STATE, d_inner // SSM_GROUPS), F32)],
        compiler_params=_params("arbitrary", "arbitrary"),
        name="ssd_bwd" if rev else "ssd_fwd",
    )(*args)


def _odd_out_body(x_ref, y_ref, z_ref, mod_ref, gn_ref, n2_ref, wout_ref, wr_ref, x1_ref, h_ref, ti_ref, tw_ref):
    z = z_ref[0].astype(F32)
    yn = _rms(y_ref[0].astype(F32) * _silu(z), gn_ref[...])
    x1 = x_ref[0] + mod_ref[0, 0, 2:3, :] * _dot(yn.astype(BF16), wout_ref[...])
    x1_ref[0] = x1
    h = _rms(x1, n2_ref[...]) * (1.0 + mod_ref[0, 0, 4:5, :]) + mod_ref[0, 0, 3:4, :]
    h_ref[0] = h
    ne = wr_ref.shape[1]
    h1, h2, _ = _split3(h)
    w1, w2, _ = _split3(wr_ref[...])
    hw = _dot(h1, jnp.concatenate([w1, w2], axis=1))
    lg = hw[:, 0:ne] + hw[:, ne:2 * ne] + _dot(h2, w1)
    idx = lax.broadcasted_iota(jnp.int32, lg.shape, 1)
    m1 = jnp.max(lg, axis=-1, keepdims=True)
    i1 = jnp.min(jnp.where(lg == m1, idx, ne), axis=-1, keepdims=True)
    lg2 = jnp.where(idx == i1, -jnp.inf, lg)
    m2 = jnp.max(lg2, axis=-1, keepdims=True)
    i2 = jnp.min(jnp.where(lg2 == m2, idx, ne), axis=-1, keepdims=True)
    e2 = jnp.exp(m2 - m1)
    den = 1.0 + e2
    ti_ref[0] = jnp.concatenate([i1, i2], axis=1)
    tw_ref[0] = jnp.concatenate([1.0 / den, e2 / den], axis=1)


def _odd_out(x, y, z, mod, gnorm, norm2, w_out, w_router, ctx_len):
    bsz, t, d = x.shape
    seq_len = t - ctx_len
    di = y.shape[2]
    ne = w_router.shape[1]
    nct = ctx_len // TM
    lat = lambda b, i: (b, i + nct, 0)
    own = lambda b, i: (b, i, 0)
    return pl.pallas_call(
        _odd_out_body,
        grid=(bsz, seq_len // TM),
        in_specs=[pl.BlockSpec((1, TM, d), lat),
                  pl.BlockSpec((1, TM, di), own),
                  pl.BlockSpec((1, TM, di), lat),
                  pl.BlockSpec((1, 1, 6, d), lambda b, i: (b, 1, 0, 0)),
                  _resident((1, di)), _resident((1, d)), _resident(w_out.shape), _resident((d, ne))],
        out_specs=[pl.BlockSpec((1, TM, d), own), pl.BlockSpec((1, TM, d), own),
                   pl.BlockSpec((1, TM, TOP_K), own), pl.BlockSpec((1, TM, TOP_K), own)],
        out_shape=[jax.ShapeDtypeStruct((bsz, seq_len, d), F32), jax.ShapeDtypeStruct((bsz, seq_len, d), F32),
                   jax.ShapeDtypeStruct((bsz, seq_len, TOP_K), jnp.int32),
                   jax.ShapeDtypeStruct((bsz, seq_len, TOP_K), F32)],
        compiler_params=_params("arbitrary", "arbitrary"),
        name="odd_out",
    )(x, y, z, mod, gnorm.reshape(1, di), norm2.reshape(1, d), w_out, w_router)


def _expert_body(te_ref, na_ref, rows_ref, next_rows_ref, dst_ref, h_hbm, w1_ref, w3_ref, w2_ref, y_hbm,
                 xf_s, xb_s, acc_s, out_s, gsem, ssem, *, nf, nt):
    ti = pl.program_id(0)
    f = pl.program_id(1)
    real = ti < nt
    active = ti < na_ref[0]
    part = TM_E // nf
    base = f * part
    last = f == nf - 1

    def gather_row(idx_ref, q, j, priority):
        pltpu.make_async_copy(h_hbm.at[pl.ds(idx_ref[0, 0, q * part + j], 1), :],
                              xf_s.at[q, pl.ds(j, 1), :], gsem).start(priority=priority)

    @pl.when(jnp.logical_and(ti == 0, f == 0))
    def _():
        out_s[...] = jnp.zeros_like(out_s)
        for q in range(nf):
            def body(j, carry):
                gather_row(rows_ref, q, 2 * j, 0)
                gather_row(rows_ref, q, 2 * j + 1, 1)
                return carry
            lax.fori_loop(0, part // 2, body, 0)

    @pl.when(f == 0)
    def _():
        for q in range(nf):
            pltpu.make_async_copy(h_hbm.at[pl.ds(0, part), :], xf_s.at[q], gsem).wait()

    @pl.when(jnp.logical_and(active, f == 0))
    def _():
        for q in range(nf):
            xb_s[q * part:(q + 1) * part, :] = xf_s[q].astype(BF16)
        acc_s[...] = jnp.zeros_like(acc_s)

    def scatter_part():
        for j in range(part):
            pltpu.make_async_copy(out_s.at[f, pl.ds(j, 1), :],
                                  y_hbm.at[pl.ds(dst_ref[0, 0, base + j], 1), :], ssem).start(priority=j % 2)

    def gather_part():
        for j in range(part):
            gather_row(next_rows_ref, f, j, 1)

    @pl.when(active)
    def _():
        scatter_part()
        gather_part()
        xb = xb_s[...]
        u = _silu(_dot(xb, w1_ref[0])) * _dot(xb, w3_ref[0])
        acc_s[...] += _dot(u.astype(BF16), w2_ref[0])

    @pl.when(jnp.logical_and(real, jnp.logical_not(active)))
    def _():
        scatter_part()
        gather_part()

    @pl.when(jnp.logical_not(real))
    def _():
        scatter_part()

    @pl.when(last)
    def _():
        for q in range(nf):
            pltpu.make_async_copy(out_s.at[q], y_hbm.at[pl.ds(0, part), :], ssem).wait()

    @pl.when(jnp.logical_and(active, last))
    def _():
        for q in range(nf):
            out_s[q] = acc_s[q * part:(q + 1) * part, :]


def _expert_ffn(h, tile_expert, n_active, row_token, row_dst, n_y_rows, w1, w3, w2):
    n, d = h.shape
    ns = row_token.shape[0]
    nt = ns // TM_E
    dff = w1.shape[2]
    nf = dff // TF_E

    def wcol(ti, f, te, na):
        return (te[jnp.minimum(ti, nt - 1)], 0, jnp.where(ti < na[0], f, nf - 1))

    def wrow(ti, f, te, na):
        return (te[jnp.minimum(ti, nt - 1)], jnp.where(ti < na[0], f, nf - 1), 0)

    def idx_block(shift):
        return pl.BlockSpec((1, 1, TM_E), lambda ti, f, te, na: (jnp.minimum(ti + shift, nt), 0, 0),
                            memory_space=pltpu.SMEM)

    grid_spec = pltpu.PrefetchScalarGridSpec(
        num_scalar_prefetch=2,
        grid=(nt + 1, nf),
        in_specs=[idx_block(0), idx_block(1), idx_block(0),
                  pl.BlockSpec(memory_space=pl.ANY),
                  pl.BlockSpec((1, d, TF_E), wcol),
                  pl.BlockSpec((1, d, TF_E), wcol),
                  pl.BlockSpec((1, TF_E, d), wrow)],
        out_specs=pl.BlockSpec(memory_space=pl.ANY),
        scratch_shapes=[pltpu.VMEM((nf, TM_E // nf, d), F32), pltpu.VMEM((TM_E, d), BF16),
                        pltpu.VMEM((TM_E, d), F32), pltpu.VMEM((nf, TM_E // nf, d), F32),
                        pltpu.SemaphoreType.DMA(()), pltpu.SemaphoreType.DMA(())],
    )
    rows = jnp.concatenate([row_token, jnp.zeros((TM_E,), jnp.int32)]).reshape(nt + 1, 1, TM_E)
    dst = row_dst.reshape(nt + 1, 1, TM_E)
    return pl.pallas_call(
        functools.partial(_expert_body, nf=nf, nt=nt),
        grid_spec=grid_spec,
        out_shape=jax.ShapeDtypeStruct((n_y_rows, d), F32),
        compiler_params=_params("arbitrary", "arbitrary"),
        name="expert_ffn",
    )(tile_expert, n_active, rows, rows, dst, h, w1, w3, w2)


def _combine_body(x_ref, y0_ref, y1_ref, tw_ref, g2_ref, fn_ref, o_ref):
    tw = tw_ref[...]
    moe = tw[:, 0:1] * y0_ref[...] + tw[:, 1:2] * y1_ref[...]
    o_ref[...] = _rms(x_ref[...] + g2_ref[0] * moe, fn_ref[...])


def _moe_combine(x1, topw, g2, final_norm, y, tiles_per_batch):
    n, d = x1.shape
    nt = n // TM_C
    return pl.pallas_call(
        _combine_body,
        grid=(nt,),
        in_specs=[pl.BlockSpec((TM_C, d), lambda i: (i, 0)),
                  pl.BlockSpec((TM_C, d), lambda i: (i, 0)),
                  pl.BlockSpec((TM_C, d), lambda i: (i + nt, 0)),
                  pl.BlockSpec((TM_C, TOP_K), lambda i: (i, 0)),
                  pl.BlockSpec((1, 1, d), lambda i: (i // tiles_per_batch, 0, 0)),
                  _resident((1, d))],
        out_specs=pl.BlockSpec((TM_C, d), lambda i: (i, 0)),
        out_shape=jax.ShapeDtypeStruct((n, d), F32),
        compiler_params=_params("arbitrary"),
        name="moe_combine",
    )(x1, y, y, topw, g2, final_norm.reshape(1, d))


def _route(topi, n_slots):
    n = topi.shape[0]
    e_flat = topi.reshape(n * TOP_K)
    oh = (e_flat[:, None] == jnp.arange(N_EXPERTS, dtype=jnp.int32)[None, :]).astype(jnp.int32)
    csum = jnp.cumsum(oh, axis=0)
    rank = jnp.sum(csum * oh, axis=1) - 1
    cnt = csum[-1]
    cnt_pad = (cnt + TM_E - 1) // TM_E * TM_E
    ends = jnp.cumsum(cnt_pad)
    offs = ends - cnt_pad
    slot = jnp.sum(oh * offs[None, :], axis=1) + rank
    n_active = (ends[-1] // TM_E).astype(jnp.int32)
    nt = n_slots // TM_E
    tile_start = jnp.arange(nt, dtype=jnp.int32) * TM_E
    te = jnp.sum((tile_start[:, None] >= ends[None, :]).astype(jnp.int32), axis=1)
    te = jnp.minimum(te, N_EXPERTS - 1)
    te = jnp.where(jnp.arange(nt) < n_active, te, te[jnp.maximum(n_active - 1, 0)])
    n_routed = n * TOP_K
    owner = jnp.full((n_slots,), -1, jnp.int32).at[slot].set(jnp.arange(n_routed, dtype=jnp.int32))
    is_pad = owner < 0
    spare = n_routed + TM_E + jnp.cumsum(is_pad.astype(jnp.int32)) - 1
    row_token = jnp.where(is_pad, 0, owner // TOP_K)
    routed = (owner % TOP_K) * n + owner // TOP_K
    row_dst = jnp.concatenate([n_routed + jnp.arange(TM_E, dtype=jnp.int32), jnp.where(is_pad, spare, routed)])
    n_y_rows = n_slots + TM_E
    return te.astype(jnp.int32), n_active.reshape(1), row_token, row_dst, n_y_rows


def kernel(x, c, ctx, c_ctx, ev_ada_w, ev_ada_b, ev_norm1, ev_norm2, ev_w_in, ev_q_gain, ev_k_gain, ev_dw_w, ev_dw_b, ev_ln_g, ev_ln_b, ev_w_o, ev_ff_w1, ev_ff_w3, ev_ff_w2, od_ada_w, od_ada_b, od_norm1, od_norm2, od_w_in, od_conv_w, od_conv_b, od_a_log_f, od_a_log_b, od_dt_bias_f, od_dt_bias_b, od_d_skip, od_gnorm, od_w_out, od_router, od_ex_w1, od_ex_w3, od_ex_w2, final_norm):
    bsz, seq_len, d = x.shape
    ctx_len = ctx.shape[1]
    assert ev_ada_w.shape[0] == 1 and od_ada_w.shape[0] == 1, "one even and one odd layer"
    assert ctx_len % TM == 0 and seq_len % TM == 0 and seq_len % GRID_W == 0
    n_ctx_tiles = ctx_len // TM
    mod = _mod_table(c, c_ctx, ev_ada_w[0], ev_ada_b[0])
    cc = ev_dw_w.shape[2]
    splits = [(0, ATTN_W), (ATTN_W, ATTN_W + KV_W), (ATTN_W + KV_W, ATTN_W + 2 * KV_W),
              (ATTN_W + 2 * KV_W, ATTN_W + 2 * KV_W + 2 * cc)]
    q, k, v, u = _inproj((ctx, x), mod, ev_norm1[0], ev_w_in[0].astype(BF16), splits, (F32, F32, BF16, F32),
                         n_ctx_tiles)
    cos2, sin2 = _rope_tables(ctx_len, seq_len)
    attn = _attention(q, k, v, cos2, sin2, ev_q_gain[0], ev_k_gain[0], ctx_len)
    conv = _conformer_conv(u, ev_dw_w[0], ev_dw_b[0], ev_ln_g[0], ev_ln_b[0], ctx_len)
    xa, (ex_w1, ex_w3, ex_w2) = _even_ffn(
        (ctx, x), attn, conv, mod, ev_norm2[0], ev_w_o[0].astype(BF16), ev_ff_w1[0].astype(BF16),
        ev_ff_w3[0].astype(BF16), ev_ff_w2[0].astype(BF16), n_ctx_tiles, (od_ex_w1[0], od_ex_w3[0], od_ex_w2[0]))

    mod = _mod_table(c, c_ctx, od_ada_w[0], od_ada_b[0])
    nh = od_a_log_f.shape[1]
    d_inner = nh * SSM_HEADDIM
    conv_dim = od_conv_w.shape[2]
    splits = [(0, d_inner), (d_inner, d_inner + conv_dim), (d_inner + conv_dim, d_inner + conv_dim + 2 * nh)]
    z, xbc, dt, dtt = _inproj((xa,), mod, od_norm1[0], od_w_in[0].astype(BF16), splits, (BF16, F32, F32),
                              n_ctx_tiles, also_transposed=True)
    xbc = _ssm_conv(xbc, od_conv_w[0], od_conv_b[0], ctx_len)
    y_b = _ssd_scan(xbc, dt, dtt, od_a_log_b[0], od_dt_bias_b[0], ctx_len, rev=True)
    y = _ssd_scan(xbc, dt, dtt, od_a_log_f[0], od_dt_bias_f[0], ctx_len, rev=False, y_other=y_b,
                  d_skip=od_d_skip[0])
    x1, h, topi, topw = _odd_out(xa, y, z, mod, od_gnorm[0], od_norm2[0], od_w_out[0].astype(BF16),
                                 od_router[0], ctx_len)

    n = bsz * seq_len
    n_slots = n * TOP_K + N_EXPERTS * TM_E
    tile_expert, n_active, row_token, row_dst, n_y_rows = _route(topi.reshape(n, TOP_K), n_slots)
    y_moe = _expert_ffn(h.reshape(n, d), tile_expert, n_active, row_token, row_dst, n_y_rows,
                        ex_w1, ex_w3, ex_w2)
    g2 = mod[:, 1, 5:6, :]
    out = _moe_combine(x1.reshape(n, d), topw.reshape(n, TOP_K), g2, final_norm, y_moe, seq_len // TM_C)
    return out.reshape(bsz, seq_len, d)
```

```python
import functools

import jax
import jax.numpy as jnp
from jax import lax
from jax.experimental import pallas as pl
from jax.experimental.pallas import tpu as pltpu

F32 = jnp.float32
BF16 = jnp.bfloat16
EPS = 1e-6
LOG2_E = 1.4426950408889634

GRID_W = 64
HEAD_DIM = 64
ATTN_HEADS = 8
KV_HEADS = 2
ATTN_W = ATTN_HEADS * HEAD_DIM
KV_W = KV_HEADS * HEAD_DIM
ROPE_THETA = 10000.0
SSM_HEADDIM = 64
SSM_GROUPS = 4
D_STATE = 128
CHUNK = 128
N_EXPERTS = 8
TOP_K = 2

LANES = 128
SUBLANES = 8
VMEM_LIMIT = 56 * 1024 * 1024

TM = 256
TM_E = 512
TF_E = 1792
TM_C = 256
SSD_CPS = 2


def _params(*sem):
    return pltpu.CompilerParams(dimension_semantics=sem, vmem_limit_bytes=VMEM_LIMIT)


def _resident(shape):
    nd = len(shape)
    return pl.BlockSpec(shape, lambda *_: (0,) * nd, pipeline_mode=pl.Buffered(1))


def _silu(x):
    return x * jax.nn.sigmoid(x)


def _rms(x, g):
    return x * lax.rsqrt(jnp.mean(x * x, axis=-1, keepdims=True) + EPS) * g


def _dot(a, b):
    return jnp.dot(a, b, preferred_element_type=F32)


def _split3(a):
    a1 = a.astype(BF16)
    r1 = a - a1.astype(F32)
    a2 = r1.astype(BF16)
    a3 = (r1 - a2.astype(F32)).astype(BF16)
    return a1, a2, a3


def _ada_body(c_ref, w_ref, b_ref, o_ref):
    s = _silu(c_ref[...]).astype(BF16)
    o_ref[...] = _dot(s, w_ref[...].astype(BF16)) + b_ref[...]


def _adaln(cond, w, b):
    r, d = cond.shape
    n = w.shape[1]
    tn = n // 4
    return pl.pallas_call(
        _ada_body,
        grid=(n // tn,),
        in_specs=[pl.BlockSpec((r, d), lambda j: (0, 0)),
                  pl.BlockSpec((d, tn), lambda j: (0, j)),
                  pl.BlockSpec((1, tn), lambda j: (0, j))],
        out_specs=pl.BlockSpec((r, tn), lambda j: (0, j)),
        out_shape=jax.ShapeDtypeStruct((r, n), F32),
        compiler_params=_params("arbitrary"),
        name="adaln",
    )(cond, w, b.reshape(1, n))


def _mod_table(c, c_ctx, w, b):
    bsz, d = c.shape
    rows = -(-(bsz + 1) // SUBLANES) * SUBLANES
    cond = jnp.zeros((rows, d), F32).at[:bsz].set(c).at[bsz].set(c_ctx)
    m = _adaln(cond, w, b)
    lat = m[:bsz]
    ctx = jnp.broadcast_to(m[bsz][None], lat.shape)
    return jnp.stack([ctx, lat], axis=1).reshape(bsz, 2, 6, d)


def _token_specs(parts, n_ctx_tiles):
    d = parts[0].shape[2]
    if len(parts) == 1:
        return [pl.BlockSpec((1, TM, d), lambda b, i: (b, i, 0))]
    return [pl.BlockSpec((1, TM, d), lambda b, i: (b, jnp.minimum(i, n_ctx_tiles - 1), 0)),
            pl.BlockSpec((1, TM, d), lambda b, i: (b, jnp.maximum(i - n_ctx_tiles, 0), 0))]


def _token_tile(refs, n_ctx_tiles):
    if len(refs) == 1:
        return refs[0][0]
    return jnp.where(pl.program_id(1) < n_ctx_tiles, refs[0][0], refs[1][0])


def _inproj_body(*refs, splits, n_parts, n_ctx_tiles):
    x = _token_tile(refs[:n_parts], n_ctx_tiles)
    mod_ref, n_ref, w_ref = refs[n_parts:n_parts + 3]
    out_refs = refs[n_parts + 3:]
    h = _rms(x, n_ref[...]) * (1.0 + mod_ref[0, 0, 1:2, :]) + mod_ref[0, 0, 0:1, :]
    r = _dot(h.astype(BF16), w_ref[...])
    for o_ref, (lo, hi) in zip(out_refs, splits):
        o_ref[0] = r[:, lo:hi].astype(o_ref.dtype)


def _inproj(parts, mod, norm, w_bf16, splits, dtypes, n_ctx_tiles):
    bsz, _, d = parts[0].shape
    t = sum(p.shape[1] for p in parts)
    n = w_bf16.shape[1]
    widths = [hi - lo for lo, hi in splits]
    return pl.pallas_call(
        functools.partial(_inproj_body, splits=tuple(splits), n_parts=len(parts), n_ctx_tiles=n_ctx_tiles),
        grid=(bsz, t // TM),
        in_specs=_token_specs(parts, n_ctx_tiles) + [
            pl.BlockSpec((1, 1, 6, d), lambda b, i: (b, (i >= n_ctx_tiles).astype(jnp.int32), 0, 0)),
            _resident((1, d)),
            _resident((d, n))],
        out_specs=[pl.BlockSpec((1, TM, wd), lambda b, i: (b, i, 0)) for wd in widths],
        out_shape=[jax.ShapeDtypeStruct((bsz, t, wd), dt) for wd, dt in zip(widths, dtypes)],
        compiler_params=_params("arbitrary", "arbitrary"),
        name="inproj",
    )(*parts, mod, norm.reshape(1, d), w_bf16)


def _attn_body(q_ref, k_ref, v_ref, cq_ref, sq_ref, ck_ref, sk_ref, qg_ref, kg_ref, o_ref, kt_s, v2_s,
               *, n_ctx_tiles, ctx_len):
    i = pl.program_id(1)
    t = k_ref.shape[1]
    lane = lax.broadcasted_iota(jnp.int32, (1, LANES), 1)
    low = lane < HEAD_DIM
    even = (lane % 2) == 0

    def norm_rope(x, gain, cos, sin):
        x2 = x * x
        s_lo = jnp.sum(jnp.where(low, x2, 0.0), axis=-1, keepdims=True)
        s_hi = jnp.sum(jnp.where(low, 0.0, x2), axis=-1, keepdims=True)
        ms = jnp.where(low, s_lo, s_hi) * (1.0 / HEAD_DIM)
        xn = x * lax.rsqrt(ms + EPS) * gain
        swapped = jnp.where(even, pltpu.roll(xn, LANES - 1, 1), pltpu.roll(xn, 1, 1))
        return xn * cos + swapped * sin

    @pl.when(i == 0)
    def _():
        k = norm_rope(k_ref[0], kg_ref[...], ck_ref[...], sk_ref[...])
        kr = pltpu.roll(k, HEAD_DIM, 1)
        kt_s[0] = jnp.where(low, k, kr).T.astype(BF16)
        kt_s[1] = jnp.where(low, kr, k).T.astype(BF16)
        v = v_ref[0].astype(F32)
        vr = pltpu.roll(v, HEAD_DIM, 1)
        v2_s[0] = jnp.where(low, v, vr).astype(BF16)
        v2_s[1] = jnp.where(low, vr, v).astype(BF16)

    def run(tk):
        for j in range(ATTN_HEADS // 2):
            g = (2 * j) // (ATTN_HEADS // KV_HEADS)
            qp = norm_rope(q_ref[0, :, j * LANES:(j + 1) * LANES], qg_ref[...], cq_ref[...], sq_ref[...])
            qp = qp * (HEAD_DIM ** -0.5 * LOG2_E)
            outs = []
            for hh in range(2):
                qm = jnp.where(low if hh == 0 else jnp.logical_not(low), qp, 0.0).astype(BF16)
                s = _dot(qm, kt_s[g, :, 0:tk])
                m = jnp.max(s, axis=-1, keepdims=True)
                p = jnp.exp2(s - m)
                l = jnp.sum(p, axis=-1, keepdims=True)
                o = _dot(p.astype(BF16), v2_s[g, 0:tk, :])
                outs.append(o / l)
            o_ref[0, :, j * LANES:(j + 1) * LANES] = jnp.where(low, outs[0], outs[1]).astype(o_ref.dtype)

    @pl.when(i < n_ctx_tiles)
    def _():
        run(ctx_len)

    @pl.when(i >= n_ctx_tiles)
    def _():
        run(t)


def _attention(q, k, v, cos2, sin2, q_gain, k_gain, ctx_len):
    bsz, t, _ = q.shape
    n_ctx_tiles = ctx_len // TM
    qg = jnp.tile(q_gain.reshape(1, HEAD_DIM), (1, 2))
    kg = jnp.tile(k_gain.reshape(1, HEAD_DIM), (1, 2))
    return pl.pallas_call(
        functools.partial(_attn_body, n_ctx_tiles=n_ctx_tiles, ctx_len=ctx_len),
        grid=(bsz, t // TM),
        in_specs=[pl.BlockSpec((1, TM, ATTN_W), lambda b, i: (b, i, 0)),
                  pl.BlockSpec((1, t, KV_W), lambda b, i: (b, 0, 0)),
                  pl.BlockSpec((1, t, KV_W), lambda b, i: (b, 0, 0)),
                  pl.BlockSpec((TM, LANES), lambda b, i: (i, 0)),
                  pl.BlockSpec((TM, LANES), lambda b, i: (i, 0)),
                  _resident((t, LANES)),
                  _resident((t, LANES)),
                  _resident((1, LANES)),
                  _resident((1, LANES))],
        out_specs=pl.BlockSpec((1, TM, ATTN_W), lambda b, i: (b, i, 0)),
        out_shape=jax.ShapeDtypeStruct((bsz, t, ATTN_W), BF16),
        scratch_shapes=[pltpu.VMEM((KV_HEADS, LANES, t), BF16), pltpu.VMEM((KV_HEADS, t, LANES), BF16)],
        compiler_params=_params("arbitrary", "arbitrary"),
        name="attention",
    )(q, k, v, cos2, sin2, cos2, sin2, qg, kg)


def _rope_tables(ctx_len, seq_len):
    rows = seq_len // GRID_W
    t_row = jnp.repeat(jnp.arange(rows, dtype=F32), GRID_W)
    t_col = jnp.tile(jnp.arange(GRID_W, dtype=F32), rows)
    axis_dim = HEAD_DIM // 2
    inv_freq = ROPE_THETA ** (-jnp.arange(0, axis_dim, 2, dtype=F32) / axis_dim)
    ang = jnp.concatenate([t_row[:, None] * inv_freq, t_col[:, None] * inv_freq], axis=-1)
    cos = jnp.repeat(jnp.cos(ang), 2, axis=-1)
    sin = jnp.repeat(jnp.sin(ang), 2, axis=-1) * jnp.tile(jnp.array([-1.0, 1.0], F32), axis_dim)
    cos = jnp.concatenate([jnp.ones((ctx_len, HEAD_DIM), F32), cos], axis=0)
    sin = jnp.concatenate([jnp.zeros((ctx_len, HEAD_DIM), F32), sin], axis=0)
    return jnp.tile(cos, (1, 2)), jnp.tile(sin, (1, 2))


def _fill_padded(src_ref, pad_s, width, ctx_len, seq_len, pad, rows, fn):
    t = ctx_len + seq_len
    z = jnp.zeros((pad + SUBLANES, width), F32)
    pad_s[0:pad, :] = z[0:pad]
    pad_s[pad + ctx_len:2 * pad + ctx_len, :] = z[0:pad]
    pad_s[2 * pad + t:3 * pad + t + SUBLANES, :] = z

    def seg(tok0, off, ntiles):
        def body(n, carry):
            r = pl.multiple_of(tok0 + n * rows, rows)
            pad_s[pl.ds(r + off, rows), :] = fn(src_ref, r, rows)
            return carry
        lax.fori_loop(0, ntiles, body, 0)

    seg(0, pad, ctx_len // rows)
    seg(ctx_len, 2 * pad, seq_len // rows)


def _conv_segments(pad_s, w_ref, taps, ctx_len, seq_len, pad, rows, emit):
    half = taps // 2
    span = rows + SUBLANES

    def seg(tok0, off, ntiles):
        def body(n, carry):
            r = pl.multiple_of(tok0 + n * rows, rows)
            base = r + off - pad
            acc = None
            for rho in range(SUBLANES):
                part = None
                for j in range(taps):
                    dj = pad - half + j
                    if dj % SUBLANES != rho:
                        continue
                    term = w_ref[j:j + 1, :] * pad_s[pl.ds(pl.multiple_of(base + (dj - rho), SUBLANES), span), :]
                    part = term if part is None else part + term
                if part is None:
                    continue
                if rho:
                    part = pltpu.roll(part, span - rho, 0)
                acc = part[0:rows] if acc is None else acc + part[0:rows]
            emit(r, acc)
            return carry
        lax.fori_loop(0, ntiles, body, 0)

    seg(0, pad, ctx_len // rows)
    seg(ctx_len, 2 * pad, seq_len // rows)


CC_PAD = 16
CC_ROWS = 32


def _cconv_body(u_ref, w_ref, b_ref, g_ref, bb_ref, o_ref, pad_s, *, ctx_len, seq_len):
    c = o_ref.shape[2]

    def glu(src_ref, r, rows):
        return src_ref[0, pl.ds(r, rows), 0:c] * jax.nn.sigmoid(src_ref[0, pl.ds(r, rows), c:2 * c])

    _fill_padded(u_ref, pad_s, c, ctx_len, seq_len, CC_PAD, TM, glu)

    def emit(r, acc):
        h = acc + b_ref[...]
        mu = jnp.mean(h, axis=-1, keepdims=True)
        hc = h - mu
        var = jnp.mean(hc * hc, axis=-1, keepdims=True)
        y = hc * lax.rsqrt(var + EPS) * g_ref[...] + bb_ref[...]
        o_ref[0, pl.ds(r, CC_ROWS), :] = _silu(y).astype(o_ref.dtype)

    _conv_segments(pad_s, w_ref, w_ref.shape[0], ctx_len, seq_len, CC_PAD, CC_ROWS, emit)


def _conformer_conv(u, dw_w, dw_b, ln_g, ln_b, ctx_len):
    bsz, t, c2 = u.shape
    c = c2 // 2
    taps = dw_w.shape[0]
    return pl.pallas_call(
        functools.partial(_cconv_body, ctx_len=ctx_len, seq_len=t - ctx_len),
        grid=(bsz,),
        in_specs=[pl.BlockSpec((1, t, c2), lambda b: (b, 0, 0)),
                  _resident((taps, c)), _resident((1, c)), _resident((1, c)), _resident((1, c))],
        out_specs=pl.BlockSpec((1, t, c), lambda b: (b, 0, 0)),
        out_shape=jax.ShapeDtypeStruct((bsz, t, c), BF16),
        scratch_shapes=[pltpu.VMEM((t + 3 * CC_PAD + SUBLANES, c), F32)],
        compiler_params=_params("arbitrary"),
        name="conformer_conv",
    )(u, dw_w, dw_b.reshape(1, c), ln_g.reshape(1, c), ln_b.reshape(1, c))


SC_PAD = 8
SC_ROWS = 64
SC_COLS = 512


def _sconv_body(x_ref, w_ref, b_ref, o_ref, pad_s, *, ctx_len, seq_len):
    def ident(src_ref, r, rows):
        return src_ref[0, pl.ds(r, rows), :]

    _fill_padded(x_ref, pad_s, SC_COLS, ctx_len, seq_len, SC_PAD, TM, ident)

    def emit(r, acc):
        o_ref[0, pl.ds(r, SC_ROWS), :] = _silu(acc + b_ref[...]).astype(o_ref.dtype)

    _conv_segments(pad_s, w_ref, w_ref.shape[0], ctx_len, seq_len, SC_PAD, SC_ROWS, emit)


def _ssm_conv(xbc, conv_w, conv_b, ctx_len):
    bsz, t, c = xbc.shape
    taps = conv_w.shape[0]
    return pl.pallas_call(
        functools.partial(_sconv_body, ctx_len=ctx_len, seq_len=t - ctx_len),
        grid=(bsz, c // SC_COLS),
        in_specs=[pl.BlockSpec((1, t, SC_COLS), lambda b, j: (b, 0, j)),
                  pl.BlockSpec((taps, SC_COLS), lambda b, j: (0, j)),
                  pl.BlockSpec((1, SC_COLS), lambda b, j: (0, j))],
        out_specs=pl.BlockSpec((1, t, SC_COLS), lambda b, j: (b, 0, j)),
        out_shape=jax.ShapeDtypeStruct((bsz, t, c), BF16),
        scratch_shapes=[pltpu.VMEM((t + 3 * SC_PAD + SUBLANES, SC_COLS), F32)],
        compiler_params=_params("arbitrary", "arbitrary"),
        name="ssm_conv",
    )(xbc, conv_w, conv_b.reshape(1, c))


def _even_ffn_body(*refs, n_parts, n_ctx_tiles, n_cast):
    x = _token_tile(refs[:n_parts], n_ctx_tiles)
    a_ref, c_ref, mod_ref, n2_ref, wo_ref, w1_ref, w3_ref, w2_ref = refs[n_parts:n_parts + 8]
    cast_in = refs[n_parts + 8:n_parts + 8 + n_cast]
    o_ref = refs[n_parts + 8 + n_cast]
    cast_out = refs[n_parts + 9 + n_cast:]
    for src, dst in zip(cast_in, cast_out):
        dst[...] = src[...].astype(dst.dtype)
    ca = a_ref.shape[2]
    mix = _dot(a_ref[0], wo_ref[0:ca, :]) + _dot(c_ref[0], wo_ref[ca:, :])
    x1 = x + mod_ref[0, 0, 2:3, :] * mix
    h = _rms(x1, n2_ref[...]) * (1.0 + mod_ref[0, 0, 4:5, :]) + mod_ref[0, 0, 3:4, :]
    hb = h.astype(BF16)
    u = _silu(_dot(hb, w1_ref[...])) * _dot(hb, w3_ref[...])
    o_ref[0] = x1 + mod_ref[0, 0, 5:6, :] * _dot(u.astype(BF16), w2_ref[...])


CAST_ROW_BLOCKS = 8


def _even_ffn(parts, attn, conv, mod, norm2, wo, w1, w3, w2, n_ctx_tiles, to_cast):
    bsz, t, ca = attn.shape
    d = parts[0].shape[2]
    cc = conv.shape[2]
    nt = t // TM
    row_blocks = max([r for r in (CAST_ROW_BLOCKS, 4, 2, 1) if to_cast[0].shape[0] * r <= bsz * nt], default=0)
    if row_blocks == 0:
        extra, to_cast = tuple(w.astype(BF16) for w in to_cast), ()
    else:
        extra = ()
    n_blocks = (to_cast[0].shape[0] if to_cast else 0) * row_blocks

    def cast_block(b, i):
        blk = jnp.minimum(b * nt + i, n_blocks - 1)
        return (blk // row_blocks, blk % row_blocks, 0)

    cast_specs = [pl.BlockSpec((1, w.shape[1] // row_blocks, w.shape[2]), cast_block) for w in to_cast]
    res = pl.pallas_call(
        functools.partial(_even_ffn_body, n_parts=len(parts), n_ctx_tiles=n_ctx_tiles, n_cast=len(to_cast)),
        grid=(bsz, nt),
        in_specs=_token_specs(parts, n_ctx_tiles) + [
                  pl.BlockSpec((1, TM, ca), lambda b, i: (b, i, 0)),
                  pl.BlockSpec((1, TM, cc), lambda b, i: (b, i, 0)),
                  pl.BlockSpec((1, 1, 6, d), lambda b, i: (b, (i >= n_ctx_tiles).astype(jnp.int32), 0, 0)),
                  _resident((1, d)), _resident(wo.shape), _resident(w1.shape), _resident(w3.shape),
                  _resident(w2.shape)] + cast_specs,
        out_specs=[pl.BlockSpec((1, TM, d), lambda b, i: (b, i, 0))] + cast_specs,
        out_shape=[jax.ShapeDtypeStruct((bsz, t, d), F32)] + [jax.ShapeDtypeStruct(w.shape, BF16) for w in to_cast],
        compiler_params=_params("arbitrary", "arbitrary"),
        name="even_ffn",
    )(*parts, attn, conv, mod, norm2.reshape(1, d), wo, w1, w3, w2, *to_cast)
    return res[0], tuple(res[1:]) + extra


def _softplus(x):
    return jnp.maximum(x, 0.0) + jnp.log(1.0 + jnp.exp(-jnp.abs(x)))


def _ssd_body(xs_ref, bm_ref, cm_ref, dt_ref, dtt_ref, alr_ref, alc_ref, bir_ref, bic_ref, exp_ref, sel_ref,
              *rest, rev, n_ctx_steps, combine):
    if combine:
        yb_ref, dsk_ref, y_ref, st_s = rest
    else:
        y_ref, st_s = rest
    c = pl.program_id(1)
    nh = alr_ref.shape[1]
    hpg = nh // SSM_GROUPS
    gw = hpg * SSM_HEADDIM
    end = 0 if rev else CHUNK - 1

    @pl.when(c == 0)
    def _():
        st_s[...] = jnp.zeros_like(st_s)

    off = nh if rev else 0
    row = lax.broadcasted_iota(jnp.int32, (CHUNK, CHUNK), 0)
    col = lax.broadcasted_iota(jnp.int32, (CHUNK, CHUNK), 1)
    tri = (col >= row) if rev else (col <= row)
    trit = (row >= col) if rev else (row <= col)
    tri_b = tri.astype(BF16)
    trit_b = trit.astype(BF16)
    lane = lax.broadcasted_iota(jnp.int32, (1, LANES), 1)
    low = lane < SSM_HEADDIM

    def cat3(v):
        return jnp.concatenate(_split3(v), axis=1)

    def step(with_y, k):
        rows = slice(k * CHUNK, (k + 1) * CHUNK)
        dt = _softplus(dt_ref[0, rows, off:off + nh] + bir_ref[...])
        dtt = _softplus(dtt_ref[0, off:off + nh, rows] + bic_ref[...])
        a = dt * (-jnp.exp(alr_ref[...]))
        at = dtt * (-jnp.exp(alc_ref[...]))
        acs = sum(_dot(tri_b, p) for p in _split3(a))
        acst = sum(_dot(p, trit_b) for p in _split3(at))
        tot = acs[end:end + 1, :]
        w_end = dt * jnp.exp(tot - acs)
        eacs = jnp.exp(acs)
        cdec = jnp.exp(tot)
        pieces = [w_end, jnp.broadcast_to(cdec, (SUBLANES, nh))]
        if with_y:
            pieces = [dt, eacs] + pieces
        spread = _dot(cat3(jnp.concatenate(pieces, axis=0)), exp_ref[...])
        r0 = 2 * CHUNK if with_y else 0
        wex = spread[r0:r0 + CHUNK]
        dec = spread[r0 + CHUNK:r0 + CHUNK + 1]
        xs = xs_ref[0, rows, :].astype(F32)
        xcd = (xs * wex).astype(BF16)
        if with_y:
            xc = (xs * spread[0:CHUNK]).astype(BF16)
            eax = spread[CHUNK:2 * CHUNK]
            acs_l = _dot(cat3(acs), sel_ref[...])
        for g in range(SSM_GROUPS):
            gs = slice(g * gw, (g + 1) * gw)
            bm = bm_ref[0, rows, g * D_STATE:(g + 1) * D_STATE]
            bmt = bm.astype(F32).T.astype(BF16)
            sg = st_s[g]
            st_s[g] = sg * dec[:, gs] + _dot(bmt, xcd[:, gs])

            if not with_y:
                continue
            cmb = cm_ref[0, rows, g * D_STATE:(g + 1) * D_STATE].astype(BF16)
            cb = _dot(cmb, bmt)
            yoff = _dot(cmb, sg.astype(BF16))
            for kp in range(hpg // 2):
                e0 = g * hpg + 2 * kp
                c0 = e0 * SSM_HEADDIM
                xcb = xc[:, c0:c0 + LANES]
                res = []
                for e in (e0, e0 + 1):
                    diff = acs_l[:, e * LANES:(e + 1) * LANES] - acst[e:e + 1, :]
                    m = (jnp.exp(jnp.where(tri, diff, -jnp.inf)) * cb).astype(BF16)
                    res.append(_dot(m, xcb))
                y = jnp.where(low, res[0], res[1]) + yoff[:, kp * LANES:(kp + 1) * LANES] * eax[:, c0:c0 + LANES]
                if combine:
                    y = (y + yb_ref[0, rows, c0:c0 + LANES].astype(F32)
                         + dsk_ref[:, c0:c0 + LANES] * xs[:, c0:c0 + LANES])
                y_ref[0, rows, c0:c0 + LANES] = y.astype(y_ref.dtype)

    order = range(SSD_CPS - 1, -1, -1) if rev else range(SSD_CPS)

    @pl.when(c >= n_ctx_steps)
    def _():
        for k in order:
            step(True, k)

    @pl.when(c < n_ctx_steps)
    def _():
        for k in order:
            step(False, k)


def _ssd_scan(xbc, dt, dtt, a_log, dt_bias, ctx_len, rev, y_other=None, d_skip=None):
    bsz, t, _ = xbc.shape
    nh = a_log.shape[0]
    d_inner = nh * SSM_HEADDIM
    gn = SSM_GROUPS * D_STATE
    rows = SSD_CPS * CHUNK
    assert t % rows == 0 and ctx_len % rows == 0
    nc = t // rows
    ncc = ctx_len // rows
    seq_len = t - ctx_len
    combine = y_other is not None

    if rev:
        def chunk(i):
            return jnp.where(i < ncc, ncc - 1 - i, nc - 1 + ncc - i)
    else:
        def chunk(i):
            return i

    def ychunk(i):
        return jnp.maximum(chunk(i), ncc) - ncc if not rev else jnp.where(i < ncc, nc - 1 - ncc, chunk(i) - ncc)

    bcol = d_inner // gn
    in_specs = [pl.BlockSpec((1, rows, d_inner), lambda b, i: (b, chunk(i), 0)),
                pl.BlockSpec((1, rows, gn), lambda b, i: (b, chunk(i), bcol)),
                pl.BlockSpec((1, rows, gn), lambda b, i: (b, chunk(i), bcol + 1)),
                pl.BlockSpec((1, rows, 2 * nh), lambda b, i: (b, chunk(i), 0)),
                pl.BlockSpec((1, 2 * nh, rows), lambda b, i: (b, 0, chunk(i))),
                _resident((1, nh)), _resident((nh, 1)), _resident((1, nh)), _resident((nh, 1)),
                _resident((3 * nh, d_inner)), _resident((3 * nh, nh * LANES))]
    heads = jnp.arange(nh, dtype=jnp.int32)[:, None]
    spread_p = jnp.tile((jnp.arange(d_inner, dtype=jnp.int32)[None, :] // SSM_HEADDIM == heads).astype(BF16), (3, 1))
    spread_l = jnp.tile((jnp.arange(nh * LANES, dtype=jnp.int32)[None, :] // LANES == heads).astype(BF16), (3, 1))
    args = [xbc, xbc, xbc, dt, dtt, a_log.reshape(1, nh), a_log.reshape(nh, 1),
            dt_bias.reshape(1, nh), dt_bias.reshape(nh, 1), spread_p, spread_l]
    if combine:
        in_specs += [pl.BlockSpec((1, rows, d_inner), lambda b, i: (b, ychunk(i), 0)), _resident((1, d_inner))]
        args += [y_other, jnp.repeat(d_skip, SSM_HEADDIM).reshape(1, d_inner)]
    return pl.pallas_call(
        functools.partial(_ssd_body, rev=rev, n_ctx_steps=ncc, combine=combine),
        grid=(bsz, nc),
        in_specs=in_specs,
        out_specs=pl.BlockSpec((1, rows, d_inner), lambda b, i: (b, ychunk(i), 0)),
        out_shape=jax.ShapeDtypeStruct((bsz, seq_len, d_inner), BF16),
        scratch_shapes=[pltpu.VMEM((SSM_GROUPS, D_STATE, d_inner // SSM_GROUPS), F32)],
        compiler_params=_params("arbitrary", "arbitrary"),
        name="ssd_bwd" if rev else "ssd_fwd",
    )(*args)


def _odd_out_body(x_ref, y_ref, z_ref, mod_ref, gn_ref, n2_ref, wout_ref, wr_ref, x1_ref, h_ref, ti_ref, tw_ref):
    z = z_ref[0].astype(F32)
    yn = _rms(y_ref[0].astype(F32) * _silu(z), gn_ref[...])
    x1 = x_ref[0] + mod_ref[0, 0, 2:3, :] * _dot(yn.astype(BF16), wout_ref[...])
    x1_ref[0] = x1
    h = _rms(x1, n2_ref[...]) * (1.0 + mod_ref[0, 0, 4:5, :]) + mod_ref[0, 0, 3:4, :]
    h_ref[0] = h
    ne = wr_ref.shape[1]
    h1, h2, _ = _split3(h)
    w1, w2, _ = _split3(wr_ref[...])
    hw = _dot(h1, jnp.concatenate([w1, w2], axis=1))
    lg = hw[:, 0:ne] + hw[:, ne:2 * ne] + _dot(h2, w1)
    idx = lax.broadcasted_iota(jnp.int32, lg.shape, 1)
    m1 = jnp.max(lg, axis=-1, keepdims=True)
    i1 = jnp.min(jnp.where(lg == m1, idx, ne), axis=-1, keepdims=True)
    lg2 = jnp.where(idx == i1, -jnp.inf, lg)
    m2 = jnp.max(lg2, axis=-1, keepdims=True)
    i2 = jnp.min(jnp.where(lg2 == m2, idx, ne), axis=-1, keepdims=True)
    e2 = jnp.exp(m2 - m1)
    den = 1.0 + e2
    ti_ref[0] = jnp.concatenate([i1, i2], axis=1)
    tw_ref[0] = jnp.concatenate([1.0 / den, e2 / den], axis=1)


def _odd_out(x, y, z, mod, gnorm, norm2, w_out, w_router, ctx_len):
    bsz, t, d = x.shape
    seq_len = t - ctx_len
    di = y.shape[2]
    ne = w_router.shape[1]
    nct = ctx_len // TM
    lat = lambda b, i: (b, i + nct, 0)
    own = lambda b, i: (b, i, 0)
    return pl.pallas_call(
        _odd_out_body,
        grid=(bsz, seq_len // TM),
        in_specs=[pl.BlockSpec((1, TM, d), lat),
                  pl.BlockSpec((1, TM, di), own),
                  pl.BlockSpec((1, TM, di), lat),
                  pl.BlockSpec((1, 1, 6, d), lambda b, i: (b, 1, 0, 0)),
                  _resident((1, di)), _resident((1, d)), _resident(w_out.shape), _resident((d, ne))],
        out_specs=[pl.BlockSpec((1, TM, d), own), pl.BlockSpec((1, TM, d), own),
                   pl.BlockSpec((1, TM, TOP_K), own), pl.BlockSpec((1, TM, TOP_K), own)],
        out_shape=[jax.ShapeDtypeStruct((bsz, seq_len, d), F32), jax.ShapeDtypeStruct((bsz, seq_len, d), F32),
                   jax.ShapeDtypeStruct((bsz, seq_len, TOP_K), jnp.int32),
                   jax.ShapeDtypeStruct((bsz, seq_len, TOP_K), F32)],
        compiler_params=_params("arbitrary", "arbitrary"),
        name="odd_out",
    )(x, y, z, mod, gnorm.reshape(1, di), norm2.reshape(1, d), w_out, w_router)


def _expert_body(te_ref, na_ref, rows_ref, next_rows_ref, dst_ref, h_hbm, w1_ref, w3_ref, w2_ref, y_hbm,
                 xf_s, xb_s, acc_s, out_s, gsem, ssem, *, nf, nt):
    ti = pl.program_id(0)
    f = pl.program_id(1)
    real = ti < nt
    active = ti < na_ref[0]
    part = TM_E // nf
    base = f * part
    last = f == nf - 1

    def gather_row(idx_ref, q, j, priority):
        pltpu.make_async_copy(h_hbm.at[pl.ds(idx_ref[0, 0, q * part + j], 1), :],
                              xf_s.at[q, pl.ds(j, 1), :], gsem).start(priority=priority)

    @pl.when(jnp.logical_and(ti == 0, f == 0))
    def _():
        out_s[...] = jnp.zeros_like(out_s)
        for q in range(nf):
            def body(j, carry):
                gather_row(rows_ref, q, 2 * j, 0)
                gather_row(rows_ref, q, 2 * j + 1, 1)
                return carry
            lax.fori_loop(0, part // 2, body, 0)

    @pl.when(f == 0)
    def _():
        for q in range(nf):
            pltpu.make_async_copy(h_hbm.at[pl.ds(0, part), :], xf_s.at[q], gsem).wait()

    @pl.when(jnp.logical_and(active, f == 0))
    def _():
        for q in range(nf):
            xb_s[q * part:(q + 1) * part, :] = xf_s[q].astype(BF16)
        acc_s[...] = jnp.zeros_like(acc_s)

    def scatter_part():
        for j in range(part):
            pltpu.make_async_copy(out_s.at[f, pl.ds(j, 1), :],
                                  y_hbm.at[pl.ds(dst_ref[0, 0, base + j], 1), :], ssem).start(priority=1)

    def gather_part():
        for j in range(part):
            gather_row(next_rows_ref, f, j, 1)

    @pl.when(active)
    def _():
        scatter_part()
        gather_part()
        xb = xb_s[...]
        u = _silu(_dot(xb, w1_ref[0])) * _dot(xb, w3_ref[0])
        acc_s[...] += _dot(u.astype(BF16), w2_ref[0])

    @pl.when(jnp.logical_and(real, jnp.logical_not(active)))
    def _():
        scatter_part()
        gather_part()

    @pl.when(jnp.logical_not(real))
    def _():
        scatter_part()

    @pl.when(last)
    def _():
        for q in range(nf):
            pltpu.make_async_copy(out_s.at[q], y_hbm.at[pl.ds(0, part), :], ssem).wait()

    @pl.when(jnp.logical_and(active, last))
    def _():
        for q in range(nf):
            out_s[q] = acc_s[q * part:(q + 1) * part, :]


def _expert_ffn(h, tile_expert, n_active, row_token, row_dst, n_y_rows, w1, w3, w2):
    n, d = h.shape
    ns = row_token.shape[0]
    nt = ns // TM_E
    dff = w1.shape[2]
    nf = dff // TF_E

    def wcol(ti, f, te, na):
        return (te[jnp.minimum(ti, nt - 1)], 0, jnp.where(ti < na[0], f, nf - 1))

    def wrow(ti, f, te, na):
        return (te[jnp.minimum(ti, nt - 1)], jnp.where(ti < na[0], f, nf - 1), 0)

    def idx_block(shift):
        return pl.BlockSpec((1, 1, TM_E), lambda ti, f, te, na: (jnp.minimum(ti + shift, nt), 0, 0),
                            memory_space=pltpu.SMEM)

    grid_spec = pltpu.PrefetchScalarGridSpec(
        num_scalar_prefetch=2,
        grid=(nt + 1, nf),
        in_specs=[idx_block(0), idx_block(1), idx_block(0),
                  pl.BlockSpec(memory_space=pl.ANY),
                  pl.BlockSpec((1, d, TF_E), wcol),
                  pl.BlockSpec((1, d, TF_E), wcol),
                  pl.BlockSpec((1, TF_E, d), wrow)],
        out_specs=pl.BlockSpec(memory_space=pl.ANY),
        scratch_shapes=[pltpu.VMEM((nf, TM_E // nf, d), F32), pltpu.VMEM((TM_E, d), BF16),
                        pltpu.VMEM((TM_E, d), F32), pltpu.VMEM((nf, TM_E // nf, d), F32),
                        pltpu.SemaphoreType.DMA(()), pltpu.SemaphoreType.DMA(())],
    )
    rows = jnp.concatenate([row_token, jnp.zeros((TM_E,), jnp.int32)]).reshape(nt + 1, 1, TM_E)
    dst = row_dst.reshape(nt + 1, 1, TM_E)
    return pl.pallas_call(
        functools.partial(_expert_body, nf=nf, nt=nt),
        grid_spec=grid_spec,
        out_shape=jax.ShapeDtypeStruct((n_y_rows, d), F32),
        compiler_params=_params("arbitrary", "arbitrary"),
        name="expert_ffn",
    )(tile_expert, n_active, rows, rows, dst, h, w1, w3, w2)


def _combine_body(x_ref, y0_ref, y1_ref, tw_ref, g2_ref, fn_ref, o_ref):
    tw = tw_ref[...]
    moe = tw[:, 0:1] * y0_ref[...] + tw[:, 1:2] * y1_ref[...]
    o_ref[...] = _rms(x_ref[...] + g2_ref[0] * moe, fn_ref[...])


def _moe_combine(x1, topw, g2, final_norm, y, tiles_per_batch):
    n, d = x1.shape
    nt = n // TM_C
    return pl.pallas_call(
        _combine_body,
        grid=(nt,),
        in_specs=[pl.BlockSpec((TM_C, d), lambda i: (i, 0)),
                  pl.BlockSpec((TM_C, d), lambda i: (i, 0)),
                  pl.BlockSpec((TM_C, d), lambda i: (i + nt, 0)),
                  pl.BlockSpec((TM_C, TOP_K), lambda i: (i, 0)),
                  pl.BlockSpec((1, 1, d), lambda i: (i // tiles_per_batch, 0, 0)),
                  _resident((1, d))],
        out_specs=pl.BlockSpec((TM_C, d), lambda i: (i, 0)),
        out_shape=jax.ShapeDtypeStruct((n, d), F32),
        compiler_params=_params("arbitrary"),
        name="moe_combine",
    )(x1, y, y, topw, g2, final_norm.reshape(1, d))


def _route(topi, n_slots):
    n = topi.shape[0]
    e_flat = topi.reshape(n * TOP_K)
    oh = (e_flat[:, None] == jnp.arange(N_EXPERTS, dtype=jnp.int32)[None, :]).astype(jnp.int32)
    csum = jnp.cumsum(oh, axis=0)
    rank = jnp.sum(csum * oh, axis=1) - 1
    cnt = csum[-1]
    cnt_pad = (cnt + TM_E - 1) // TM_E * TM_E
    ends = jnp.cumsum(cnt_pad)
    offs = ends - cnt_pad
    slot = jnp.sum(oh * offs[None, :], axis=1) + rank
    n_active = (ends[-1] // TM_E).astype(jnp.int32)
    nt = n_slots // TM_E
    tile_start = jnp.arange(nt, dtype=jnp.int32) * TM_E
    te = jnp.sum((tile_start[:, None] >= ends[None, :]).astype(jnp.int32), axis=1)
    te = jnp.minimum(te, N_EXPERTS - 1)
    te = jnp.where(jnp.arange(nt) < n_active, te, te[jnp.maximum(n_active - 1, 0)])
    n_routed = n * TOP_K
    owner = jnp.full((n_slots,), -1, jnp.int32).at[slot].set(jnp.arange(n_routed, dtype=jnp.int32))
    is_pad = owner < 0
    spare = n_routed + TM_E + jnp.cumsum(is_pad.astype(jnp.int32)) - 1
    row_token = jnp.where(is_pad, 0, owner // TOP_K)
    routed = (owner % TOP_K) * n + owner // TOP_K
    row_dst = jnp.concatenate([n_routed + jnp.arange(TM_E, dtype=jnp.int32), jnp.where(is_pad, spare, routed)])
    n_y_rows = n_slots + TM_E
    return te.astype(jnp.int32), n_active.reshape(1), row_token, row_dst, n_y_rows


def kernel(x, c, ctx, c_ctx, ev_ada_w, ev_ada_b, ev_norm1, ev_norm2, ev_w_in, ev_q_gain, ev_k_gain, ev_dw_w, ev_dw_b, ev_ln_g, ev_ln_b, ev_w_o, ev_ff_w1, ev_ff_w3, ev_ff_w2, od_ada_w, od_ada_b, od_norm1, od_norm2, od_w_in, od_conv_w, od_conv_b, od_a_log_f, od_a_log_b, od_dt_bias_f, od_dt_bias_b, od_d_skip, od_gnorm, od_w_out, od_router, od_ex_w1, od_ex_w3, od_ex_w2, final_norm):
    bsz, seq_len, d = x.shape
    ctx_len = ctx.shape[1]
    assert ev_ada_w.shape[0] == 1 and od_ada_w.shape[0] == 1, "one even and one odd layer"
    assert ctx_len % TM == 0 and seq_len % TM == 0 and seq_len % GRID_W == 0
    n_ctx_tiles = ctx_len // TM
    mod = _mod_table(c, c_ctx, ev_ada_w[0], ev_ada_b[0])
    cc = ev_dw_w.shape[2]
    splits = [(0, ATTN_W), (ATTN_W, ATTN_W + KV_W), (ATTN_W + KV_W, ATTN_W + 2 * KV_W),
              (ATTN_W + 2 * KV_W, ATTN_W + 2 * KV_W + 2 * cc)]
    q, k, v, u = _inproj((ctx, x), mod, ev_norm1[0], ev_w_in[0].astype(BF16), splits, (F32, F32, BF16, F32),
                         n_ctx_tiles)
    cos2, sin2 = _rope_tables(ctx_len, seq_len)
    attn = _attention(q, k, v, cos2, sin2, ev_q_gain[0], ev_k_gain[0], ctx_len)
    conv = _conformer_conv(u, ev_dw_w[0], ev_dw_b[0], ev_ln_g[0], ev_ln_b[0], ctx_len)
    xa, (ex_w1, ex_w3, ex_w2) = _even_ffn(
        (ctx, x), attn, conv, mod, ev_norm2[0], ev_w_o[0].astype(BF16), ev_ff_w1[0].astype(BF16),
        ev_ff_w3[0].astype(BF16), ev_ff_w2[0].astype(BF16), n_ctx_tiles, (od_ex_w1[0], od_ex_w3[0], od_ex_w2[0]))

    mod = _mod_table(c, c_ctx, od_ada_w[0], od_ada_b[0])
    nh = od_a_log_f.shape[1]
    d_inner = nh * SSM_HEADDIM
    conv_dim = od_conv_w.shape[2]
    splits = [(0, d_inner), (d_inner, d_inner + conv_dim), (d_inner + conv_dim, d_inner + conv_dim + 2 * nh)]
    z, xbc, dt = _inproj((xa,), mod, od_norm1[0], od_w_in[0].astype(BF16), splits, (BF16, F32, F32), n_ctx_tiles)
    dtt = dt.transpose(0, 2, 1)
    xbc = _ssm_conv(xbc, od_conv_w[0], od_conv_b[0], ctx_len)
    y_b = _ssd_scan(xbc, dt, dtt, od_a_log_b[0], od_dt_bias_b[0], ctx_len, rev=True)
    y = _ssd_scan(xbc, dt, dtt, od_a_log_f[0], od_dt_bias_f[0], ctx_len, rev=False, y_other=y_b,
                  d_skip=od_d_skip[0])
    x1, h, topi, topw = _odd_out(xa, y, z, mod, od_gnorm[0], od_norm2[0], od_w_out[0].astype(BF16),
                                 od_router[0], ctx_len)

    n = bsz * seq_len
    n_slots = n * TOP_K + N_EXPERTS * TM_E
    tile_expert, n_active, row_token, row_dst, n_y_rows = _route(topi.reshape(n, TOP_K), n_slots)
    y_moe = _expert_ffn(h.reshape(n, d), tile_expert, n_active, row_token, row_dst, n_y_rows,
                        ex_w1, ex_w3, ex_w2)
    g2 = mod[:, 1, 5:6, :]
    out = _moe_combine(x1.reshape(n, d), topw.reshape(n, TOP_K), g2, final_norm, y_moe, seq_len // TM_C)
    return out.reshape(bsz, seq_len, d)
```

```python
import functools

import jax
import jax.numpy as jnp
import numpy as np
from jax import lax
from jax.experimental import pallas as pl
from jax.experimental.pallas import tpu as pltpu

F32 = jnp.float32
BF16 = jnp.bfloat16
EPS = 1e-6
LOG2_E = 1.4426950408889634

GRID_W = 64
HEAD_DIM = 64
ATTN_HEADS = 8
KV_HEADS = 2
ATTN_W = ATTN_HEADS * HEAD_DIM
KV_W = KV_HEADS * HEAD_DIM
ROPE_THETA = 10000.0
SSM_HEADDIM = 64
SSM_GROUPS = 4
D_STATE = 128
CHUNK = 128
N_EXPERTS = 8
TOP_K = 2

LANES = 128
SUBLANES = 8
VMEM_LIMIT = 56 * 1024 * 1024

TM = 256
TM_E = 512
TF_E = 1792
TM_C = 256
SSD_CPS = 2


def _params(*sem):
    return pltpu.CompilerParams(dimension_semantics=sem, vmem_limit_bytes=VMEM_LIMIT)


def _resident(shape):
    nd = len(shape)
    return pl.BlockSpec(shape, lambda *_: (0,) * nd, pipeline_mode=pl.Buffered(1))


def _silu(x):
    return x * jax.nn.sigmoid(x)


def _rms(x, g):
    return x * lax.rsqrt(jnp.mean(x * x, axis=-1, keepdims=True) + EPS) * g


def _dot(a, b):
    return jnp.dot(a, b, preferred_element_type=F32)


def _split3(a):
    a1 = a.astype(BF16)
    r1 = a - a1.astype(F32)
    a2 = r1.astype(BF16)
    a3 = (r1 - a2.astype(F32)).astype(BF16)
    return a1, a2, a3


def _ada_body(c_ref, w_ref, b_ref, o_ref):
    s = _silu(c_ref[...]).astype(BF16)
    o_ref[...] = _dot(s, w_ref[...].astype(BF16)) + b_ref[...]


def _adaln(cond, w, b):
    r, d = cond.shape
    n = w.shape[1]
    tn = n // 4
    return pl.pallas_call(
        _ada_body,
        grid=(n // tn,),
        in_specs=[pl.BlockSpec((r, d), lambda j: (0, 0)),
                  pl.BlockSpec((d, tn), lambda j: (0, j)),
                  pl.BlockSpec((1, tn), lambda j: (0, j))],
        out_specs=pl.BlockSpec((r, tn), lambda j: (0, j)),
        out_shape=jax.ShapeDtypeStruct((r, n), F32),
        compiler_params=_params("arbitrary"),
        name="adaln",
    )(cond, w, b.reshape(1, n))


def _mod_table(c, c_ctx, w, b):
    bsz, d = c.shape
    rows = -(-(bsz + 1) // SUBLANES) * SUBLANES
    cond = jnp.zeros((rows, d), F32).at[:bsz].set(c).at[bsz].set(c_ctx)
    m = _adaln(cond, w, b)
    lat = m[:bsz]
    ctx = jnp.broadcast_to(m[bsz][None], lat.shape)
    return jnp.stack([ctx, lat], axis=1).reshape(bsz, 2, 6, d)


def _token_specs(parts, n_ctx_tiles):
    d = parts[0].shape[2]
    if len(parts) == 1:
        return [pl.BlockSpec((1, TM, d), lambda b, i: (b, i, 0))]
    return [pl.BlockSpec((1, TM, d), lambda b, i: (b, jnp.minimum(i, n_ctx_tiles - 1), 0)),
            pl.BlockSpec((1, TM, d), lambda b, i: (b, jnp.maximum(i - n_ctx_tiles, 0), 0))]


def _token_tile(refs, n_ctx_tiles):
    if len(refs) == 1:
        return refs[0][0]
    return jnp.where(pl.program_id(1) < n_ctx_tiles, refs[0][0], refs[1][0])


def _inproj_body(*refs, splits, n_parts, n_ctx_tiles):
    x = _token_tile(refs[:n_parts], n_ctx_tiles)
    mod_ref, n_ref, w_ref = refs[n_parts:n_parts + 3]
    out_refs = refs[n_parts + 3:]
    h = _rms(x, n_ref[...]) * (1.0 + mod_ref[0, 0, 1:2, :]) + mod_ref[0, 0, 0:1, :]
    r = _dot(h.astype(BF16), w_ref[...])
    for o_ref, (lo, hi) in zip(out_refs, splits):
        o_ref[0] = r[:, lo:hi].astype(o_ref.dtype)


def _inproj(parts, mod, norm, w_bf16, splits, dtypes, n_ctx_tiles):
    bsz, _, d = parts[0].shape
    t = sum(p.shape[1] for p in parts)
    n = w_bf16.shape[1]
    widths = [hi - lo for lo, hi in splits]
    return pl.pallas_call(
        functools.partial(_inproj_body, splits=tuple(splits), n_parts=len(parts), n_ctx_tiles=n_ctx_tiles),
        grid=(bsz, t // TM),
        in_specs=_token_specs(parts, n_ctx_tiles) + [
            pl.BlockSpec((1, 1, 6, d), lambda b, i: (b, (i >= n_ctx_tiles).astype(jnp.int32), 0, 0)),
            _resident((1, d)),
            _resident((d, n))],
        out_specs=[pl.BlockSpec((1, TM, wd), lambda b, i: (b, i, 0)) for wd in widths],
        out_shape=[jax.ShapeDtypeStruct((bsz, t, wd), dt) for wd, dt in zip(widths, dtypes)],
        compiler_params=_params("arbitrary", "arbitrary"),
        name="inproj",
    )(*parts, mod, norm.reshape(1, d), w_bf16)


def _attn_body(q_ref, k_ref, v_ref, cq_ref, sq_ref, ck_ref, sk_ref, qg_ref, kg_ref, o_ref, kt_s, v2_s,
               *, n_ctx_tiles, ctx_len):
    i = pl.program_id(1)
    t = k_ref.shape[1]
    lane = lax.broadcasted_iota(jnp.int32, (1, LANES), 1)
    low = lane < HEAD_DIM
    even = (lane % 2) == 0

    def norm_rope(x, gain, cos, sin):
        x2 = x * x
        s_lo = jnp.sum(jnp.where(low, x2, 0.0), axis=-1, keepdims=True)
        s_hi = jnp.sum(jnp.where(low, 0.0, x2), axis=-1, keepdims=True)
        ms = jnp.where(low, s_lo, s_hi) * (1.0 / HEAD_DIM)
        xn = x * lax.rsqrt(ms + EPS) * gain
        swapped = jnp.where(even, pltpu.roll(xn, LANES - 1, 1), pltpu.roll(xn, 1, 1))
        return xn * cos + swapped * sin

    @pl.when(i == 0)
    def _():
        k = norm_rope(k_ref[0], kg_ref[...], ck_ref[...], sk_ref[...])
        kr = pltpu.roll(k, HEAD_DIM, 1)
        kt_s[0] = jnp.where(low, k, kr).T.astype(BF16)
        kt_s[1] = jnp.where(low, kr, k).T.astype(BF16)
        v = v_ref[0].astype(F32)
        vr = pltpu.roll(v, HEAD_DIM, 1)
        v2_s[0] = jnp.where(low, v, vr).astype(BF16)
        v2_s[1] = jnp.where(low, vr, v).astype(BF16)

    def run(tk):
        for j in range(ATTN_HEADS // 2):
            g = (2 * j) // (ATTN_HEADS // KV_HEADS)
            qp = norm_rope(q_ref[0, :, j * LANES:(j + 1) * LANES], qg_ref[...], cq_ref[...], sq_ref[...])
            qp = qp * (HEAD_DIM ** -0.5 * LOG2_E)
            outs = []
            for hh in range(2):
                qm = jnp.where(low if hh == 0 else jnp.logical_not(low), qp, 0.0).astype(BF16)
                s = _dot(qm, kt_s[g, :, 0:tk])
                m = jnp.max(s, axis=-1, keepdims=True)
                p = jnp.exp2(s - m)
                l = jnp.sum(p, axis=-1, keepdims=True)
                o = _dot(p.astype(BF16), v2_s[g, 0:tk, :])
                outs.append(o / l)
            o_ref[0, :, j * LANES:(j + 1) * LANES] = jnp.where(low, outs[0], outs[1]).astype(o_ref.dtype)

    @pl.when(i < n_ctx_tiles)
    def _():
        run(ctx_len)

    @pl.when(i >= n_ctx_tiles)
    def _():
        run(t)


def _attention(q, k, v, cos2, sin2, q_gain, k_gain, ctx_len):
    bsz, t, _ = q.shape
    n_ctx_tiles = ctx_len // TM
    qg = jnp.tile(q_gain.reshape(1, HEAD_DIM), (1, 2))
    kg = jnp.tile(k_gain.reshape(1, HEAD_DIM), (1, 2))
    return pl.pallas_call(
        functools.partial(_attn_body, n_ctx_tiles=n_ctx_tiles, ctx_len=ctx_len),
        grid=(bsz, t // TM),
        in_specs=[pl.BlockSpec((1, TM, ATTN_W), lambda b, i: (b, i, 0)),
                  pl.BlockSpec((1, t, KV_W), lambda b, i: (b, 0, 0)),
                  pl.BlockSpec((1, t, KV_W), lambda b, i: (b, 0, 0)),
                  pl.BlockSpec((TM, LANES), lambda b, i: (i, 0)),
                  pl.BlockSpec((TM, LANES), lambda b, i: (i, 0)),
                  _resident((t, LANES)),
                  _resident((t, LANES)),
                  _resident((1, LANES)),
                  _resident((1, LANES))],
        out_specs=pl.BlockSpec((1, TM, ATTN_W), lambda b, i: (b, i, 0)),
        out_shape=jax.ShapeDtypeStruct((bsz, t, ATTN_W), BF16),
        scratch_shapes=[pltpu.VMEM((KV_HEADS, LANES, t), BF16), pltpu.VMEM((KV_HEADS, t, LANES), BF16)],
        compiler_params=_params("arbitrary", "arbitrary"),
        name="attention",
    )(q, k, v, cos2, sin2, cos2, sin2, qg, kg)


def _rope_tables(ctx_len, seq_len):
    f32 = np.float32
    rows = seq_len // GRID_W
    t_row = np.repeat(np.arange(rows, dtype=f32), GRID_W)
    t_col = np.tile(np.arange(GRID_W, dtype=f32), rows)
    axis_dim = HEAD_DIM // 2
    inv_freq = (f32(ROPE_THETA) ** (-np.arange(0, axis_dim, 2, dtype=f32) / f32(axis_dim))).astype(f32)
    ang = np.concatenate([t_row[:, None] * inv_freq, t_col[:, None] * inv_freq], axis=-1)
    cos = np.repeat(np.cos(ang), 2, axis=-1)
    sin = np.repeat(np.sin(ang), 2, axis=-1) * np.tile(np.array([-1.0, 1.0], f32), axis_dim)
    cos = np.concatenate([np.ones((ctx_len, HEAD_DIM), f32), cos], axis=0)
    sin = np.concatenate([np.zeros((ctx_len, HEAD_DIM), f32), sin], axis=0)
    return jnp.asarray(np.tile(cos, (1, 2)), F32), jnp.asarray(np.tile(sin, (1, 2)), F32)


def _fill_padded(src_ref, pad_s, width, ctx_len, seq_len, pad, rows, fn):
    t = ctx_len + seq_len
    z = jnp.zeros((pad + SUBLANES, width), F32)
    pad_s[0:pad, :] = z[0:pad]
    pad_s[pad + ctx_len:2 * pad + ctx_len, :] = z[0:pad]
    pad_s[2 * pad + t:3 * pad + t + SUBLANES, :] = z

    def seg(tok0, off, ntiles):
        def body(n, carry):
            r = pl.multiple_of(tok0 + n * rows, rows)
            pad_s[pl.ds(r + off, rows), :] = fn(src_ref, r, rows)
            return carry
        lax.fori_loop(0, ntiles, body, 0)

    seg(0, pad, ctx_len // rows)
    seg(ctx_len, 2 * pad, seq_len // rows)


def _conv_segments(pad_s, w_ref, taps, ctx_len, seq_len, pad, rows, emit):
    half = taps // 2
    span = rows + SUBLANES

    def seg(tok0, off, ntiles):
        def body(n, carry):
            r = pl.multiple_of(tok0 + n * rows, rows)
            base = r + off - pad
            acc = None
            for rho in range(SUBLANES):
                part = None
                for j in range(taps):
                    dj = pad - half + j
                    if dj % SUBLANES != rho:
                        continue
                    term = w_ref[j:j + 1, :] * pad_s[pl.ds(pl.multiple_of(base + (dj - rho), SUBLANES), span), :]
                    part = term if part is None else part + term
                if part is None:
                    continue
                if rho:
                    part = pltpu.roll(part, span - rho, 0)
                acc = part[0:rows] if acc is None else acc + part[0:rows]
            emit(r, acc)
            return carry
        lax.fori_loop(0, ntiles, body, 0)

    seg(0, pad, ctx_len // rows)
    seg(ctx_len, 2 * pad, seq_len // rows)


CC_PAD = 16
CC_ROWS = 32


def _cconv_body(u_ref, w_ref, b_ref, g_ref, bb_ref, o_ref, pad_s, *, ctx_len, seq_len):
    c = o_ref.shape[2]

    def glu(src_ref, r, rows):
        return src_ref[0, pl.ds(r, rows), 0:c] * jax.nn.sigmoid(src_ref[0, pl.ds(r, rows), c:2 * c])

    _fill_padded(u_ref, pad_s, c, ctx_len, seq_len, CC_PAD, TM, glu)

    def emit(r, acc):
        h = acc + b_ref[...]
        mu = jnp.mean(h, axis=-1, keepdims=True)
        hc = h - mu
        var = jnp.mean(hc * hc, axis=-1, keepdims=True)
        y = hc * lax.rsqrt(var + EPS) * g_ref[...] + bb_ref[...]
        o_ref[0, pl.ds(r, CC_ROWS), :] = _silu(y).astype(o_ref.dtype)

    _conv_segments(pad_s, w_ref, w_ref.shape[0], ctx_len, seq_len, CC_PAD, CC_ROWS, emit)


def _conformer_conv(u, dw_w, dw_b, ln_g, ln_b, ctx_len):
    bsz, t, c2 = u.shape
    c = c2 // 2
    taps = dw_w.shape[0]
    return pl.pallas_call(
        functools.partial(_cconv_body, ctx_len=ctx_len, seq_len=t - ctx_len),
        grid=(bsz,),
        in_specs=[pl.BlockSpec((1, t, c2), lambda b: (b, 0, 0)),
                  _resident((taps, c)), _resident((1, c)), _resident((1, c)), _resident((1, c))],
        out_specs=pl.BlockSpec((1, t, c), lambda b: (b, 0, 0)),
        out_shape=jax.ShapeDtypeStruct((bsz, t, c), BF16),
        scratch_shapes=[pltpu.VMEM((t + 3 * CC_PAD + SUBLANES, c), F32)],
        compiler_params=_params("arbitrary"),
        name="conformer_conv",
    )(u, dw_w, dw_b.reshape(1, c), ln_g.reshape(1, c), ln_b.reshape(1, c))


SC_PAD = 8
SC_ROWS = 64
SC_COLS = 512


def _sconv_body(x_ref, w_ref, b_ref, o_ref, pad_s, *, ctx_len, seq_len):
    def ident(src_ref, r, rows):
        return src_ref[0, pl.ds(r, rows), :]

    _fill_padded(x_ref, pad_s, SC_COLS, ctx_len, seq_len, SC_PAD, TM, ident)

    def emit(r, acc):
        o_ref[0, pl.ds(r, SC_ROWS), :] = _silu(acc + b_ref[...]).astype(o_ref.dtype)

    _conv_segments(pad_s, w_ref, w_ref.shape[0], ctx_len, seq_len, SC_PAD, SC_ROWS, emit)


def _ssm_conv(xbc, conv_w, conv_b, ctx_len):
    bsz, t, c = xbc.shape
    taps = conv_w.shape[0]
    return pl.pallas_call(
        functools.partial(_sconv_body, ctx_len=ctx_len, seq_len=t - ctx_len),
        grid=(bsz, c // SC_COLS),
        in_specs=[pl.BlockSpec((1, t, SC_COLS), lambda b, j: (b, 0, j)),
                  pl.BlockSpec((taps, SC_COLS), lambda b, j: (0, j)),
                  pl.BlockSpec((1, SC_COLS), lambda b, j: (0, j))],
        out_specs=pl.BlockSpec((1, t, SC_COLS), lambda b, j: (b, 0, j)),
        out_shape=jax.ShapeDtypeStruct((bsz, t, c), BF16),
        scratch_shapes=[pltpu.VMEM((t + 3 * SC_PAD + SUBLANES, SC_COLS), F32)],
        compiler_params=_params("arbitrary", "arbitrary"),
        name="ssm_conv",
    )(xbc, conv_w, conv_b.reshape(1, c))


def _even_ffn_body(*refs, n_parts, n_ctx_tiles, n_cast):
    x = _token_tile(refs[:n_parts], n_ctx_tiles)
    a_ref, c_ref, mod_ref, n2_ref, wo_ref, w1_ref, w3_ref, w2_ref = refs[n_parts:n_parts + 8]
    cast_in = refs[n_parts + 8:n_parts + 8 + n_cast]
    o_ref = refs[n_parts + 8 + n_cast]
    cast_out = refs[n_parts + 9 + n_cast:]
    for src, dst in zip(cast_in, cast_out):
        dst[...] = src[...].astype(dst.dtype)
    ca = a_ref.shape[2]
    mix = _dot(a_ref[0], wo_ref[0:ca, :]) + _dot(c_ref[0], wo_ref[ca:, :])
    x1 = x + mod_ref[0, 0, 2:3, :] * mix
    h = _rms(x1, n2_ref[...]) * (1.0 + mod_ref[0, 0, 4:5, :]) + mod_ref[0, 0, 3:4, :]
    hb = h.astype(BF16)
    u = _silu(_dot(hb, w1_ref[...])) * _dot(hb, w3_ref[...])
    o_ref[0] = x1 + mod_ref[0, 0, 5:6, :] * _dot(u.astype(BF16), w2_ref[...])


CAST_ROW_BLOCKS = 8


def _even_ffn(parts, attn, conv, mod, norm2, wo, w1, w3, w2, n_ctx_tiles, to_cast):
    bsz, t, ca = attn.shape
    d = parts[0].shape[2]
    cc = conv.shape[2]
    nt = t // TM
    row_blocks = max([r for r in (CAST_ROW_BLOCKS, 4, 2, 1) if to_cast[0].shape[0] * r <= bsz * nt], default=0)
    if row_blocks == 0:
        extra, to_cast = tuple(w.astype(BF16) for w in to_cast), ()
    else:
        extra = ()
    n_blocks = (to_cast[0].shape[0] if to_cast else 0) * row_blocks

    def cast_block(b, i):
        blk = jnp.minimum(b * nt + i, n_blocks - 1)
        return (blk // row_blocks, blk % row_blocks, 0)

    cast_specs = [pl.BlockSpec((1, w.shape[1] // row_blocks, w.shape[2]), cast_block) for w in to_cast]
    res = pl.pallas_call(
        functools.partial(_even_ffn_body, n_parts=len(parts), n_ctx_tiles=n_ctx_tiles, n_cast=len(to_cast)),
        grid=(bsz, nt),
        in_specs=_token_specs(parts, n_ctx_tiles) + [
                  pl.BlockSpec((1, TM, ca), lambda b, i: (b, i, 0)),
                  pl.BlockSpec((1, TM, cc), lambda b, i: (b, i, 0)),
                  pl.BlockSpec((1, 1, 6, d), lambda b, i: (b, (i >= n_ctx_tiles).astype(jnp.int32), 0, 0)),
                  _resident((1, d)), _resident(wo.shape), _resident(w1.shape), _resident(w3.shape),
                  _resident(w2.shape)] + cast_specs,
        out_specs=[pl.BlockSpec((1, TM, d), lambda b, i: (b, i, 0))] + cast_specs,
        out_shape=[jax.ShapeDtypeStruct((bsz, t, d), F32)] + [jax.ShapeDtypeStruct(w.shape, BF16) for w in to_cast],
        compiler_params=_params("arbitrary", "arbitrary"),
        name="even_ffn",
    )(*parts, attn, conv, mod, norm2.reshape(1, d), wo, w1, w3, w2, *to_cast)
    return res[0], tuple(res[1:]) + extra


def _softplus(x):
    return jnp.maximum(x, 0.0) + jnp.log(1.0 + jnp.exp(-jnp.abs(x)))


def _ssd_body(xs_ref, bm_ref, cm_ref, dt_ref, dtt_ref, alr_ref, alc_ref, bir_ref, bic_ref, exp_ref, sel_ref,
              *rest, rev, n_ctx_steps, combine):
    if combine:
        yb_ref, dsk_ref, y_ref, st_s = rest
    else:
        y_ref, st_s = rest
    c = pl.program_id(1)
    nh = alr_ref.shape[1]
    hpg = nh // SSM_GROUPS
    gw = hpg * SSM_HEADDIM
    end = 0 if rev else CHUNK - 1

    @pl.when(c == 0)
    def _():
        st_s[...] = jnp.zeros_like(st_s)

    off = nh if rev else 0
    row = lax.broadcasted_iota(jnp.int32, (CHUNK, CHUNK), 0)
    col = lax.broadcasted_iota(jnp.int32, (CHUNK, CHUNK), 1)
    tri = (col >= row) if rev else (col <= row)
    trit = (row >= col) if rev else (row <= col)
    tri_b = tri.astype(BF16)
    trit_b = trit.astype(BF16)
    lane = lax.broadcasted_iota(jnp.int32, (1, LANES), 1)
    low = lane < SSM_HEADDIM

    def cat3(v):
        return jnp.concatenate(_split3(v), axis=1)

    def step(with_y, k):
        rows = slice(k * CHUNK, (k + 1) * CHUNK)
        dt = _softplus(dt_ref[0, rows, off:off + nh] + bir_ref[...])
        dtt = _softplus(dtt_ref[0, off:off + nh, rows] + bic_ref[...])
        a = dt * (-jnp.exp(alr_ref[...]))
        at = dtt * (-jnp.exp(alc_ref[...]))
        acs = sum(_dot(tri_b, p) for p in _split3(a))
        acst = sum(_dot(p, trit_b) for p in _split3(at))
        tot = acs[end:end + 1, :]
        w_end = dt * jnp.exp(tot - acs)
        eacs = jnp.exp(acs)
        cdec = jnp.exp(tot)
        pieces = [w_end, jnp.broadcast_to(cdec, (SUBLANES, nh))]
        if with_y:
            pieces = [dt, eacs] + pieces
        spread = _dot(cat3(jnp.concatenate(pieces, axis=0)), exp_ref[...])
        r0 = 2 * CHUNK if with_y else 0
        wex = spread[r0:r0 + CHUNK]
        dec = spread[r0 + CHUNK:r0 + CHUNK + 1]
        xs = xs_ref[0, rows, :].astype(F32)
        xcd = (xs * wex).astype(BF16)
        if with_y:
            xc = (xs * spread[0:CHUNK]).astype(BF16)
            eax = spread[CHUNK:2 * CHUNK]
            acs_l = _dot(cat3(acs), sel_ref[...])
        for g in range(SSM_GROUPS):
            gs = slice(g * gw, (g + 1) * gw)
            bm = bm_ref[0, rows, g * D_STATE:(g + 1) * D_STATE]
            bmt = bm.astype(F32).T.astype(BF16)
            sg = st_s[g]
            st_s[g] = sg * dec[:, gs] + _dot(bmt, xcd[:, gs])

            if not with_y:
                continue
            cmb = cm_ref[0, rows, g * D_STATE:(g + 1) * D_STATE].astype(BF16)
            cb = _dot(cmb, bmt)
            yoff = _dot(cmb, sg.astype(BF16))
            for kp in range(hpg // 2):
                e0 = g * hpg + 2 * kp
                c0 = e0 * SSM_HEADDIM
                xcb = xc[:, c0:c0 + LANES]
                res = []
                for e in (e0, e0 + 1):
                    diff = acs_l[:, e * LANES:(e + 1) * LANES] - acst[e:e + 1, :]
                    m = (jnp.exp(jnp.where(tri, diff, -jnp.inf)) * cb).astype(BF16)
                    res.append(_dot(m, xcb))
                y = jnp.where(low, res[0], res[1]) + yoff[:, kp * LANES:(kp + 1) * LANES] * eax[:, c0:c0 + LANES]
                if combine:
                    y = (y + yb_ref[0, rows, c0:c0 + LANES].astype(F32)
                         + dsk_ref[:, c0:c0 + LANES] * xs[:, c0:c0 + LANES])
                y_ref[0, rows, c0:c0 + LANES] = y.astype(y_ref.dtype)

    order = range(SSD_CPS - 1, -1, -1) if rev else range(SSD_CPS)

    @pl.when(c >= n_ctx_steps)
    def _():
        for k in order:
            step(True, k)

    @pl.when(c < n_ctx_steps)
    def _():
        for k in order:
            step(False, k)


def _ssd_scan(xbc, dt, dtt, a_log, dt_bias, ctx_len, rev, y_other=None, d_skip=None):
    bsz, t, _ = xbc.shape
    nh = a_log.shape[0]
    d_inner = nh * SSM_HEADDIM
    gn = SSM_GROUPS * D_STATE
    rows = SSD_CPS * CHUNK
    assert t % rows == 0 and ctx_len % rows == 0
    nc = t // rows
    ncc = ctx_len // rows
    seq_len = t - ctx_len
    combine = y_other is not None

    if rev:
        def chunk(i):
            return jnp.where(i < ncc, ncc - 1 - i, nc - 1 + ncc - i)
    else:
        def chunk(i):
            return i

    def ychunk(i):
        return jnp.maximum(chunk(i), ncc) - ncc if not rev else jnp.where(i < ncc, nc - 1 - ncc, chunk(i) - ncc)

    bcol = d_inner // gn
    in_specs = [pl.BlockSpec((1, rows, d_inner), lambda b, i: (b, chunk(i), 0)),
                pl.BlockSpec((1, rows, gn), lambda b, i: (b, chunk(i), bcol)),
                pl.BlockSpec((1, rows, gn), lambda b, i: (b, chunk(i), bcol + 1)),
                pl.BlockSpec((1, rows, 2 * nh), lambda b, i: (b, chunk(i), 0)),
                pl.BlockSpec((1, 2 * nh, rows), lambda b, i: (b, 0, chunk(i))),
                _resident((1, nh)), _resident((nh, 1)), _resident((1, nh)), _resident((nh, 1)),
                _resident((3 * nh, d_inner)), _resident((3 * nh, nh * LANES))]
    heads = np.arange(nh)[:, None]
    spread_p = jnp.asarray(np.tile(np.arange(d_inner)[None, :] // SSM_HEADDIM == heads, (3, 1)), BF16)
    spread_l = jnp.asarray(np.tile(np.arange(nh * LANES)[None, :] // LANES == heads, (3, 1)), BF16)
    args = [xbc, xbc, xbc, dt, dtt, a_log.reshape(1, nh), a_log.reshape(nh, 1),
            dt_bias.reshape(1, nh), dt_bias.reshape(nh, 1), spread_p, spread_l]
    if combine:
        in_specs += [pl.BlockSpec((1, rows, d_inner), lambda b, i: (b, ychunk(i), 0)), _resident((1, d_inner))]
        args += [y_other, jnp.repeat(d_skip, SSM_HEADDIM).reshape(1, d_inner)]
    return pl.pallas_call(
        functools.partial(_ssd_body, rev=rev, n_ctx_steps=ncc, combine=combine),
        grid=(bsz, nc),
        in_specs=in_specs,
        out_specs=pl.BlockSpec((1, rows, d_inner), lambda b, i: (b, ychunk(i), 0)),
        out_shape=jax.ShapeDtypeStruct((bsz, seq_len, d_inner), BF16),
        scratch_shapes=[pltpu.VMEM((SSM_GROUPS, D_STATE, d_inner // SSM_GROUPS), F32)],
        compiler_params=_params("arbitrary", "arbitrary"),
        name="ssd_bwd" if rev else "ssd_fwd",
    )(*args)


def _odd_out_body(x_ref, y_ref, z_ref, mod_ref, gn_ref, n2_ref, wout_ref, wr_ref, x1_ref, h_ref, ti_ref, tw_ref):
    z = z_ref[0].astype(F32)
    yn = _rms(y_ref[0].astype(F32) * _silu(z), gn_ref[...])
    x1 = x_ref[0] + mod_ref[0, 0, 2:3, :] * _dot(yn.astype(BF16), wout_ref[...])
    x1_ref[0] = x1
    h = _rms(x1, n2_ref[...]) * (1.0 + mod_ref[0, 0, 4:5, :]) + mod_ref[0, 0, 3:4, :]
    h_ref[0] = h
    ne = wr_ref.shape[1]
    h1, h2, _ = _split3(h)
    w1, w2, _ = _split3(wr_ref[...])
    hw = _dot(h1, jnp.concatenate([w1, w2], axis=1))
    lg = hw[:, 0:ne] + hw[:, ne:2 * ne] + _dot(h2, w1)
    idx = lax.broadcasted_iota(jnp.int32, lg.shape, 1)
    m1 = jnp.max(lg, axis=-1, keepdims=True)
    i1 = jnp.min(jnp.where(lg == m1, idx, ne), axis=-1, keepdims=True)
    lg2 = jnp.where(idx == i1, -jnp.inf, lg)
    m2 = jnp.max(lg2, axis=-1, keepdims=True)
    i2 = jnp.min(jnp.where(lg2 == m2, idx, ne), axis=-1, keepdims=True)
    e2 = jnp.exp(m2 - m1)
    den = 1.0 + e2
    ti_ref[0] = jnp.concatenate([i1, i2], axis=1)
    tw_ref[0] = jnp.concatenate([1.0 / den, e2 / den], axis=1)


def _odd_out(x, y, z, mod, gnorm, norm2, w_out, w_router, ctx_len):
    bsz, t, d = x.shape
    seq_len = t - ctx_len
    di = y.shape[2]
    ne = w_router.shape[1]
    nct = ctx_len // TM
    lat = lambda b, i: (b, i + nct, 0)
    own = lambda b, i: (b, i, 0)
    return pl.pallas_call(
        _odd_out_body,
        grid=(bsz, seq_len // TM),
        in_specs=[pl.BlockSpec((1, TM, d), lat),
                  pl.BlockSpec((1, TM, di), own),
                  pl.BlockSpec((1, TM, di), lat),
                  pl.BlockSpec((1, 1, 6, d), lambda b, i: (b, 1, 0, 0)),
                  _resident((1, di)), _resident((1, d)), _resident(w_out.shape), _resident((d, ne))],
        out_specs=[pl.BlockSpec((1, TM, d), own), pl.BlockSpec((1, TM, d), own),
                   pl.BlockSpec((1, TM, TOP_K), own), pl.BlockSpec((1, TM, TOP_K), own)],
        out_shape=[jax.ShapeDtypeStruct((bsz, seq_len, d), F32), jax.ShapeDtypeStruct((bsz, seq_len, d), F32),
                   jax.ShapeDtypeStruct((bsz, seq_len, TOP_K), jnp.int32),
                   jax.ShapeDtypeStruct((bsz, seq_len, TOP_K), F32)],
        compiler_params=_params("arbitrary", "arbitrary"),
        name="odd_out",
    )(x, y, z, mod, gnorm.reshape(1, di), norm2.reshape(1, d), w_out, w_router)


def _expert_body(te_ref, na_ref, rows_ref, next_rows_ref, dst_ref, h_hbm, w1_ref, w3_ref, w2_ref, y_hbm,
                 xf_s, xb_s, acc_s, out_s, gsem, ssem, *, nf, nt):
    ti = pl.program_id(0)
    f = pl.program_id(1)
    real = ti < nt
    active = ti < na_ref[0]
    part = TM_E // nf
    base = f * part
    last = f == nf - 1

    def gather_row(idx_ref, q, j, priority):
        pltpu.make_async_copy(h_hbm.at[pl.ds(idx_ref[0, 0, q * part + j], 1), :],
                              xf_s.at[q, pl.ds(j, 1), :], gsem).start(priority=priority)

    @pl.when(jnp.logical_and(ti == 0, f == 0))
    def _():
        out_s[...] = jnp.zeros_like(out_s)
        for q in range(nf):
            def body(j, carry):
                gather_row(rows_ref, q, 2 * j, 0)
                gather_row(rows_ref, q, 2 * j + 1, 1)
                return carry
            lax.fori_loop(0, part // 2, body, 0)

    @pl.when(f == 0)
    def _():
        for q in range(nf):
            pltpu.make_async_copy(h_hbm.at[pl.ds(0, part), :], xf_s.at[q], gsem).wait()

    @pl.when(jnp.logical_and(active, f == 0))
    def _():
        for q in range(nf):
            xb_s[q * part:(q + 1) * part, :] = xf_s[q].astype(BF16)
        acc_s[...] = jnp.zeros_like(acc_s)

    def scatter_part():
        for j in range(part):
            pltpu.make_async_copy(out_s.at[f, pl.ds(j, 1), :],
                                  y_hbm.at[pl.ds(dst_ref[0, 0, base + j], 1), :], ssem).start(priority=1)

    def gather_part():
        for j in range(part):
            gather_row(next_rows_ref, f, j, 1)

    @pl.when(active)
    def _():
        scatter_part()
        gather_part()
        xb = xb_s[...]
        u = _silu(_dot(xb, w1_ref[0])) * _dot(xb, w3_ref[0])
        acc_s[...] += _dot(u.astype(BF16), w2_ref[0])

    @pl.when(jnp.logical_and(real, jnp.logical_not(active)))
    def _():
        scatter_part()
        gather_part()

    @pl.when(jnp.logical_not(real))
    def _():
        scatter_part()

    @pl.when(last)
    def _():
        for q in range(nf):
            pltpu.make_async_copy(out_s.at[q], y_hbm.at[pl.ds(0, part), :], ssem).wait()

    @pl.when(jnp.logical_and(active, last))
    def _():
        for q in range(nf):
            out_s[q] = acc_s[q * part:(q + 1) * part, :]


def _expert_ffn(h, tile_expert, n_active, row_token, row_dst, n_y_rows, w1, w3, w2):
    n, d = h.shape
    ns = row_token.shape[0]
    nt = ns // TM_E
    dff = w1.shape[2]
    nf = dff // TF_E

    def wcol(ti, f, te, na):
        return (te[jnp.minimum(ti, nt - 1)], 0, jnp.where(ti < na[0], f, nf - 1))

    def wrow(ti, f, te, na):
        return (te[jnp.minimum(ti, nt - 1)], jnp.where(ti < na[0], f, nf - 1), 0)

    def idx_block(shift):
        return pl.BlockSpec((1, 1, TM_E), lambda ti, f, te, na: (jnp.minimum(ti + shift, nt), 0, 0),
                            memory_space=pltpu.SMEM)

    grid_spec = pltpu.PrefetchScalarGridSpec(
        num_scalar_prefetch=2,
        grid=(nt + 1, nf),
        in_specs=[idx_block(0), idx_block(1), idx_block(0),
                  pl.BlockSpec(memory_space=pl.ANY),
                  pl.BlockSpec((1, d, TF_E), wcol),
                  pl.BlockSpec((1, d, TF_E), wcol),
                  pl.BlockSpec((1, TF_E, d), wrow)],
        out_specs=pl.BlockSpec(memory_space=pl.ANY),
        scratch_shapes=[pltpu.VMEM((nf, TM_E // nf, d), F32), pltpu.VMEM((TM_E, d), BF16),
                        pltpu.VMEM((TM_E, d), F32), pltpu.VMEM((nf, TM_E // nf, d), F32),
                        pltpu.SemaphoreType.DMA(()), pltpu.SemaphoreType.DMA(())],
    )
    rows = jnp.concatenate([row_token, jnp.zeros((TM_E,), jnp.int32)]).reshape(nt + 1, 1, TM_E)
    dst = row_dst.reshape(nt + 1, 1, TM_E)
    return pl.pallas_call(
        functools.partial(_expert_body, nf=nf, nt=nt),
        grid_spec=grid_spec,
        out_shape=jax.ShapeDtypeStruct((n_y_rows, d), F32),
        compiler_params=_params("arbitrary", "arbitrary"),
        name="expert_ffn",
    )(tile_expert, n_active, rows, rows, dst, h, w1, w3, w2)


def _combine_body(x_ref, y0_ref, y1_ref, tw_ref, g2_ref, fn_ref, o_ref):
    tw = tw_ref[...]
    moe = tw[:, 0:1] * y0_ref[...] + tw[:, 1:2] * y1_ref[...]
    o_ref[...] = _rms(x_ref[...] + g2_ref[0] * moe, fn_ref[...])


def _moe_combine(x1, topw, g2, final_norm, y, tiles_per_batch):
    n, d = x1.shape
    nt = n // TM_C
    return pl.pallas_call(
        _combine_body,
        grid=(nt,),
        in_specs=[pl.BlockSpec((TM_C, d), lambda i: (i, 0)),
                  pl.BlockSpec((TM_C, d), lambda i: (i, 0)),
                  pl.BlockSpec((TM_C, d), lambda i: (i + nt, 0)),
                  pl.BlockSpec((TM_C, TOP_K), lambda i: (i, 0)),
                  pl.BlockSpec((1, 1, d), lambda i: (i // tiles_per_batch, 0, 0)),
                  _resident((1, d))],
        out_specs=pl.BlockSpec((TM_C, d), lambda i: (i, 0)),
        out_shape=jax.ShapeDtypeStruct((n, d), F32),
        compiler_params=_params("arbitrary"),
        name="moe_combine",
    )(x1, y, y, topw, g2, final_norm.reshape(1, d))


def _route(topi, n_slots):
    n = topi.shape[0]
    e_flat = topi.reshape(n * TOP_K)
    oh = (e_flat[:, None] == jnp.arange(N_EXPERTS, dtype=jnp.int32)[None, :]).astype(jnp.int32)
    csum = jnp.cumsum(oh, axis=0)
    rank = jnp.sum(csum * oh, axis=1) - 1
    cnt = csum[-1]
    cnt_pad = (cnt + TM_E - 1) // TM_E * TM_E
    ends = jnp.cumsum(cnt_pad)
    offs = ends - cnt_pad
    slot = jnp.sum(oh * offs[None, :], axis=1) + rank
    n_active = (ends[-1] // TM_E).astype(jnp.int32)
    nt = n_slots // TM_E
    tile_start = jnp.arange(nt, dtype=jnp.int32) * TM_E
    te = jnp.sum((tile_start[:, None] >= ends[None, :]).astype(jnp.int32), axis=1)
    te = jnp.minimum(te, N_EXPERTS - 1)
    te = jnp.where(jnp.arange(nt) < n_active, te, te[jnp.maximum(n_active - 1, 0)])
    n_routed = n * TOP_K
    owner = jnp.full((n_slots,), -1, jnp.int32).at[slot].set(
        jnp.arange(n_routed, dtype=jnp.int32), unique_indices=True, mode="promise_in_bounds")
    is_pad = owner < 0
    spare = n_routed + TM_E + jnp.cumsum(is_pad.astype(jnp.int32)) - 1
    row_token = jnp.where(is_pad, 0, owner // TOP_K)
    routed = (owner % TOP_K) * n + owner // TOP_K
    row_dst = jnp.concatenate([n_routed + jnp.arange(TM_E, dtype=jnp.int32), jnp.where(is_pad, spare, routed)])
    n_y_rows = n_slots + TM_E
    return te.astype(jnp.int32), n_active.reshape(1), row_token, row_dst, n_y_rows


def kernel(x, c, ctx, c_ctx, ev_ada_w, ev_ada_b, ev_norm1, ev_norm2, ev_w_in, ev_q_gain, ev_k_gain, ev_dw_w, ev_dw_b, ev_ln_g, ev_ln_b, ev_w_o, ev_ff_w1, ev_ff_w3, ev_ff_w2, od_ada_w, od_ada_b, od_norm1, od_norm2, od_w_in, od_conv_w, od_conv_b, od_a_log_f, od_a_log_b, od_dt_bias_f, od_dt_bias_b, od_d_skip, od_gnorm, od_w_out, od_router, od_ex_w1, od_ex_w3, od_ex_w2, final_norm):
    bsz, seq_len, d = x.shape
    ctx_len = ctx.shape[1]
    assert ev_ada_w.shape[0] == 1 and od_ada_w.shape[0] == 1, "one even and one odd layer"
    assert ctx_len % TM == 0 and seq_len % TM == 0 and seq_len % GRID_W == 0
    n_ctx_tiles = ctx_len // TM
    mod = _mod_table(c, c_ctx, ev_ada_w[0], ev_ada_b[0])
    cc = ev_dw_w.shape[2]
    splits = [(0, ATTN_W), (ATTN_W, ATTN_W + KV_W), (ATTN_W + KV_W, ATTN_W + 2 * KV_W),
              (ATTN_W + 2 * KV_W, ATTN_W + 2 * KV_W + 2 * cc)]
    q, k, v, u = _inproj((ctx, x), mod, ev_norm1[0], ev_w_in[0].astype(BF16), splits, (F32, F32, BF16, F32),
                         n_ctx_tiles)
    cos2, sin2 = _rope_tables(ctx_len, seq_len)
    attn = _attention(q, k, v, cos2, sin2, ev_q_gain[0], ev_k_gain[0], ctx_len)
    conv = _conformer_conv(u, ev_dw_w[0], ev_dw_b[0], ev_ln_g[0], ev_ln_b[0], ctx_len)
    xa, (ex_w1, ex_w3, ex_w2) = _even_ffn(
        (ctx, x), attn, conv, mod, ev_norm2[0], ev_w_o[0].astype(BF16), ev_ff_w1[0].astype(BF16),
        ev_ff_w3[0].astype(BF16), ev_ff_w2[0].astype(BF16), n_ctx_tiles, (od_ex_w1[0], od_ex_w3[0], od_ex_w2[0]))

    mod = _mod_table(c, c_ctx, od_ada_w[0], od_ada_b[0])
    nh = od_a_log_f.shape[1]
    d_inner = nh * SSM_HEADDIM
    conv_dim = od_conv_w.shape[2]
    splits = [(0, d_inner), (d_inner, d_inner + conv_dim), (d_inner + conv_dim, d_inner + conv_dim + 2 * nh)]
    z, xbc, dt = _inproj((xa,), mod, od_norm1[0], od_w_in[0].astype(BF16), splits, (BF16, F32, F32), n_ctx_tiles)
    dtt = dt.transpose(0, 2, 1)
    xbc = _ssm_conv(xbc, od_conv_w[0], od_conv_b[0], ctx_len)
    y_b = _ssd_scan(xbc, dt, dtt, od_a_log_b[0], od_dt_bias_b[0], ctx_len, rev=True)
    y = _ssd_scan(xbc, dt, dtt, od_a_log_f[0], od_dt_bias_f[0], ctx_len, rev=False, y_other=y_b,
                  d_skip=od_d_skip[0])
    x1, h, topi, topw = _odd_out(xa, y, z, mod, od_gnorm[0], od_norm2[0], od_w_out[0].astype(BF16),
                                 od_router[0], ctx_len)

    n = bsz * seq_len
    n_slots = n * TOP_K + N_EXPERTS * TM_E
    tile_expert, n_active, row_token, row_dst, n_y_rows = _route(topi.reshape(n, TOP_K), n_slots)
    y_moe = _expert_ffn(h.reshape(n, d), tile_expert, n_active, row_token, row_dst, n_y_rows,
                        ex_w1, ex_w3, ex_w2)
    g2 = mod[:, 1, 5:6, :]
    out = _moe_combine(x1.reshape(n, d), topw.reshape(n, TOP_K), g2, final_norm, y_moe, seq_len // TM_C)
    return out.reshape(bsz, seq_len, d)
```

```python
import functools

import jax
import jax.numpy as jnp
import numpy as np
from jax import lax
from jax.experimental import pallas as pl
from jax.experimental.pallas import tpu as pltpu

F32 = jnp.float32
BF16 = jnp.bfloat16
EPS = 1e-6
LOG2_E = 1.4426950408889634

GRID_W = 64
HEAD_DIM = 64
ATTN_HEADS = 8
KV_HEADS = 2
ATTN_W = ATTN_HEADS * HEAD_DIM
KV_W = KV_HEADS * HEAD_DIM
ROPE_THETA = 10000.0
SSM_HEADDIM = 64
SSM_GROUPS = 4
D_STATE = 128
CHUNK = 128
N_EXPERTS = 8
TOP_K = 2

LANES = 128
SUBLANES = 8
VMEM_LIMIT = 56 * 1024 * 1024

TM = 256
TM_E = 1024
TF_E = 896
TM_C = 256
SSD_CPS = 2


def _params(*sem):
    return pltpu.CompilerParams(dimension_semantics=sem, vmem_limit_bytes=VMEM_LIMIT)


def _resident(shape):
    nd = len(shape)
    return pl.BlockSpec(shape, lambda *_: (0,) * nd, pipeline_mode=pl.Buffered(1))


def _silu(x):
    return x * jax.nn.sigmoid(x)


def _rms(x, g):
    return x * lax.rsqrt(jnp.mean(x * x, axis=-1, keepdims=True) + EPS) * g


def _dot(a, b):
    return jnp.dot(a, b, preferred_element_type=F32)


def _split3(a):
    a1 = a.astype(BF16)
    r1 = a - a1.astype(F32)
    a2 = r1.astype(BF16)
    a3 = (r1 - a2.astype(F32)).astype(BF16)
    return a1, a2, a3


def _ada_body(c_ref, w_ref, b_ref, o_ref):
    s = _silu(c_ref[...]).astype(BF16)
    o_ref[...] = _dot(s, w_ref[...].astype(BF16)) + b_ref[...]


def _adaln(cond, w, b):
    r, d = cond.shape
    n = w.shape[1]
    tn = n // 4
    return pl.pallas_call(
        _ada_body,
        grid=(n // tn,),
        in_specs=[pl.BlockSpec((r, d), lambda j: (0, 0)),
                  pl.BlockSpec((d, tn), lambda j: (0, j)),
                  pl.BlockSpec((1, tn), lambda j: (0, j))],
        out_specs=pl.BlockSpec((r, tn), lambda j: (0, j)),
        out_shape=jax.ShapeDtypeStruct((r, n), F32),
        compiler_params=_params("arbitrary"),
        name="adaln",
    )(cond, w, b.reshape(1, n))


def _mod_table(c, c_ctx, w, b):
    bsz, d = c.shape
    rows = -(-(bsz + 1) // SUBLANES) * SUBLANES
    cond = jnp.zeros((rows, d), F32).at[:bsz].set(c).at[bsz].set(c_ctx)
    m = _adaln(cond, w, b)
    lat = m[:bsz]
    ctx = jnp.broadcast_to(m[bsz][None], lat.shape)
    return jnp.stack([ctx, lat], axis=1).reshape(bsz, 2, 6, d)


def _token_specs(parts, n_ctx_tiles):
    d = parts[0].shape[2]
    if len(parts) == 1:
        return [pl.BlockSpec((1, TM, d), lambda b, i: (b, i, 0))]
    return [pl.BlockSpec((1, TM, d), lambda b, i: (b, jnp.minimum(i, n_ctx_tiles - 1), 0)),
            pl.BlockSpec((1, TM, d), lambda b, i: (b, jnp.maximum(i - n_ctx_tiles, 0), 0))]


def _token_tile(refs, n_ctx_tiles):
    if len(refs) == 1:
        return refs[0][0]
    return jnp.where(pl.program_id(1) < n_ctx_tiles, refs[0][0], refs[1][0])


def _inproj_body(*refs, splits, n_parts, n_ctx_tiles):
    x = _token_tile(refs[:n_parts], n_ctx_tiles)
    mod_ref, n_ref, w_ref = refs[n_parts:n_parts + 3]
    out_refs = refs[n_parts + 3:]
    h = _rms(x, n_ref[...]) * (1.0 + mod_ref[0, 0, 1:2, :]) + mod_ref[0, 0, 0:1, :]
    r = _dot(h.astype(BF16), w_ref[...])
    for o_ref, (lo, hi) in zip(out_refs, splits):
        o_ref[0] = r[:, lo:hi].astype(o_ref.dtype)


def _inproj(parts, mod, norm, w_bf16, splits, dtypes, n_ctx_tiles):
    bsz, _, d = parts[0].shape
    t = sum(p.shape[1] for p in parts)
    n = w_bf16.shape[1]
    widths = [hi - lo for lo, hi in splits]
    return pl.pallas_call(
        functools.partial(_inproj_body, splits=tuple(splits), n_parts=len(parts), n_ctx_tiles=n_ctx_tiles),
        grid=(bsz, t // TM),
        in_specs=_token_specs(parts, n_ctx_tiles) + [
            pl.BlockSpec((1, 1, 6, d), lambda b, i: (b, (i >= n_ctx_tiles).astype(jnp.int32), 0, 0)),
            _resident((1, d)),
            _resident((d, n))],
        out_specs=[pl.BlockSpec((1, TM, wd), lambda b, i: (b, i, 0)) for wd in widths],
        out_shape=[jax.ShapeDtypeStruct((bsz, t, wd), dt) for wd, dt in zip(widths, dtypes)],
        compiler_params=_params("arbitrary", "arbitrary"),
        name="inproj",
    )(*parts, mod, norm.reshape(1, d), w_bf16)


def _attn_body(q_ref, k_ref, v_ref, cq_ref, sq_ref, ck_ref, sk_ref, qg_ref, kg_ref, o_ref, kt_s, v2_s,
               *, n_ctx_tiles, ctx_len):
    i = pl.program_id(1)
    t = k_ref.shape[1]
    lane = lax.broadcasted_iota(jnp.int32, (1, LANES), 1)
    low = lane < HEAD_DIM
    even = (lane % 2) == 0

    def norm_rope(x, gain, cos, sin):
        x2 = x * x
        s_lo = jnp.sum(jnp.where(low, x2, 0.0), axis=-1, keepdims=True)
        s_hi = jnp.sum(jnp.where(low, 0.0, x2), axis=-1, keepdims=True)
        ms = jnp.where(low, s_lo, s_hi) * (1.0 / HEAD_DIM)
        xn = x * lax.rsqrt(ms + EPS) * gain
        swapped = jnp.where(even, pltpu.roll(xn, LANES - 1, 1), pltpu.roll(xn, 1, 1))
        return xn * cos + swapped * sin

    @pl.when(i == 0)
    def _():
        k = norm_rope(k_ref[0], kg_ref[...], ck_ref[...], sk_ref[...])
        kr = pltpu.roll(k, HEAD_DIM, 1)
        kt_s[0] = jnp.where(low, k, kr).T.astype(BF16)
        kt_s[1] = jnp.where(low, kr, k).T.astype(BF16)
        v = v_ref[0].astype(F32)
        vr = pltpu.roll(v, HEAD_DIM, 1)
        v2_s[0] = jnp.where(low, v, vr).astype(BF16)
        v2_s[1] = jnp.where(low, vr, v).astype(BF16)

    def run(tk):
        for j in range(ATTN_HEADS // 2):
            g = (2 * j) // (ATTN_HEADS // KV_HEADS)
            qp = norm_rope(q_ref[0, :, j * LANES:(j + 1) * LANES], qg_ref[...], cq_ref[...], sq_ref[...])
            qp = qp * (HEAD_DIM ** -0.5 * LOG2_E)
            outs = []
            for hh in range(2):
                qm = jnp.where(low if hh == 0 else jnp.logical_not(low), qp, 0.0).astype(BF16)
                s = _dot(qm, kt_s[g, :, 0:tk])
                m = jnp.max(s, axis=-1, keepdims=True)
                p = jnp.exp2(s - m)
                l = jnp.sum(p, axis=-1, keepdims=True)
                o = _dot(p.astype(BF16), v2_s[g, 0:tk, :])
                outs.append(o / l)
            o_ref[0, :, j * LANES:(j + 1) * LANES] = jnp.where(low, outs[0], outs[1]).astype(o_ref.dtype)

    @pl.when(i < n_ctx_tiles)
    def _():
        run(ctx_len)

    @pl.when(i >= n_ctx_tiles)
    def _():
        run(t)


def _attention(q, k, v, cos2, sin2, q_gain, k_gain, ctx_len):
    bsz, t, _ = q.shape
    n_ctx_tiles = ctx_len // TM
    qg = jnp.tile(q_gain.reshape(1, HEAD_DIM), (1, 2))
    kg = jnp.tile(k_gain.reshape(1, HEAD_DIM), (1, 2))
    return pl.pallas_call(
        functools.partial(_attn_body, n_ctx_tiles=n_ctx_tiles, ctx_len=ctx_len),
        grid=(bsz, t // TM),
        in_specs=[pl.BlockSpec((1, TM, ATTN_W), lambda b, i: (b, i, 0)),
                  pl.BlockSpec((1, t, KV_W), lambda b, i: (b, 0, 0)),
                  pl.BlockSpec((1, t, KV_W), lambda b, i: (b, 0, 0)),
                  pl.BlockSpec((TM, LANES), lambda b, i: (i, 0)),
                  pl.BlockSpec((TM, LANES), lambda b, i: (i, 0)),
                  _resident((t, LANES)),
                  _resident((t, LANES)),
                  _resident((1, LANES)),
                  _resident((1, LANES))],
        out_specs=pl.BlockSpec((1, TM, ATTN_W), lambda b, i: (b, i, 0)),
        out_shape=jax.ShapeDtypeStruct((bsz, t, ATTN_W), BF16),
        scratch_shapes=[pltpu.VMEM((KV_HEADS, LANES, t), BF16), pltpu.VMEM((KV_HEADS, t, LANES), BF16)],
        compiler_params=_params("arbitrary", "arbitrary"),
        name="attention",
    )(q, k, v, cos2, sin2, cos2, sin2, qg, kg)


def _rope_tables(ctx_len, seq_len):
    f32 = np.float32
    rows = seq_len // GRID_W
    t_row = np.repeat(np.arange(rows, dtype=f32), GRID_W)
    t_col = np.tile(np.arange(GRID_W, dtype=f32), rows)
    axis_dim = HEAD_DIM // 2
    inv_freq = (f32(ROPE_THETA) ** (-np.arange(0, axis_dim, 2, dtype=f32) / f32(axis_dim))).astype(f32)
    ang = np.concatenate([t_row[:, None] * inv_freq, t_col[:, None] * inv_freq], axis=-1)
    cos = np.repeat(np.cos(ang), 2, axis=-1)
    sin = np.repeat(np.sin(ang), 2, axis=-1) * np.tile(np.array([-1.0, 1.0], f32), axis_dim)
    cos = np.concatenate([np.ones((ctx_len, HEAD_DIM), f32), cos], axis=0)
    sin = np.concatenate([np.zeros((ctx_len, HEAD_DIM), f32), sin], axis=0)
    return jnp.asarray(np.tile(cos, (1, 2)), F32), jnp.asarray(np.tile(sin, (1, 2)), F32)


def _fill_padded(src_ref, pad_s, width, ctx_len, seq_len, pad, rows, fn):
    t = ctx_len + seq_len
    z = jnp.zeros((pad + SUBLANES, width), F32)
    pad_s[0:pad, :] = z[0:pad]
    pad_s[pad + ctx_len:2 * pad + ctx_len, :] = z[0:pad]
    pad_s[2 * pad + t:3 * pad + t + SUBLANES, :] = z

    def seg(tok0, off, ntiles):
        def body(n, carry):
            r = pl.multiple_of(tok0 + n * rows, rows)
            pad_s[pl.ds(r + off, rows), :] = fn(src_ref, r, rows)
            return carry
        lax.fori_loop(0, ntiles, body, 0)

    seg(0, pad, ctx_len // rows)
    seg(ctx_len, 2 * pad, seq_len // rows)


def _conv_segments(pad_s, w_ref, taps, ctx_len, seq_len, pad, rows, emit):
    half = taps // 2
    span = rows + SUBLANES

    def seg(tok0, off, ntiles):
        def body(n, carry):
            r = pl.multiple_of(tok0 + n * rows, rows)
            base = r + off - pad
            acc = None
            for rho in range(SUBLANES):
                part = None
                for j in range(taps):
                    dj = pad - half + j
                    if dj % SUBLANES != rho:
                        continue
                    term = w_ref[j:j + 1, :] * pad_s[pl.ds(pl.multiple_of(base + (dj - rho), SUBLANES), span), :]
                    part = term if part is None else part + term
                if part is None:
                    continue
                if rho:
                    part = pltpu.roll(part, span - rho, 0)
                acc = part[0:rows] if acc is None else acc + part[0:rows]
            emit(r, acc)
            return carry
        lax.fori_loop(0, ntiles, body, 0)

    seg(0, pad, ctx_len // rows)
    seg(ctx_len, 2 * pad, seq_len // rows)


CC_PAD = 16
CC_ROWS = 32


def _cconv_body(u_ref, w_ref, b_ref, g_ref, bb_ref, o_ref, pad_s, *, ctx_len, seq_len):
    c = o_ref.shape[2]

    def glu(src_ref, r, rows):
        return src_ref[0, pl.ds(r, rows), 0:c] * jax.nn.sigmoid(src_ref[0, pl.ds(r, rows), c:2 * c])

    _fill_padded(u_ref, pad_s, c, ctx_len, seq_len, CC_PAD, TM, glu)

    def emit(r, acc):
        h = acc + b_ref[...]
        mu = jnp.mean(h, axis=-1, keepdims=True)
        hc = h - mu
        var = jnp.mean(hc * hc, axis=-1, keepdims=True)
        y = hc * lax.rsqrt(var + EPS) * g_ref[...] + bb_ref[...]
        o_ref[0, pl.ds(r, CC_ROWS), :] = _silu(y).astype(o_ref.dtype)

    _conv_segments(pad_s, w_ref, w_ref.shape[0], ctx_len, seq_len, CC_PAD, CC_ROWS, emit)


def _conformer_conv(u, dw_w, dw_b, ln_g, ln_b, ctx_len):
    bsz, t, c2 = u.shape
    c = c2 // 2
    taps = dw_w.shape[0]
    return pl.pallas_call(
        functools.partial(_cconv_body, ctx_len=ctx_len, seq_len=t - ctx_len),
        grid=(bsz,),
        in_specs=[pl.BlockSpec((1, t, c2), lambda b: (b, 0, 0)),
                  _resident((taps, c)), _resident((1, c)), _resident((1, c)), _resident((1, c))],
        out_specs=pl.BlockSpec((1, t, c), lambda b: (b, 0, 0)),
        out_shape=jax.ShapeDtypeStruct((bsz, t, c), BF16),
        scratch_shapes=[pltpu.VMEM((t + 3 * CC_PAD + SUBLANES, c), F32)],
        compiler_params=_params("arbitrary"),
        name="conformer_conv",
    )(u, dw_w, dw_b.reshape(1, c), ln_g.reshape(1, c), ln_b.reshape(1, c))


SC_PAD = 8
SC_ROWS = 64
SC_COLS = 512


def _sconv_body(x_ref, w_ref, b_ref, o_ref, pad_s, *, ctx_len, seq_len):
    def ident(src_ref, r, rows):
        return src_ref[0, pl.ds(r, rows), :]

    _fill_padded(x_ref, pad_s, SC_COLS, ctx_len, seq_len, SC_PAD, TM, ident)

    def emit(r, acc):
        o_ref[0, pl.ds(r, SC_ROWS), :] = _silu(acc + b_ref[...]).astype(o_ref.dtype)

    _conv_segments(pad_s, w_ref, w_ref.shape[0], ctx_len, seq_len, SC_PAD, SC_ROWS, emit)


def _ssm_conv(xbc, conv_w, conv_b, ctx_len):
    bsz, t, c = xbc.shape
    taps = conv_w.shape[0]
    return pl.pallas_call(
        functools.partial(_sconv_body, ctx_len=ctx_len, seq_len=t - ctx_len),
        grid=(bsz, c // SC_COLS),
        in_specs=[pl.BlockSpec((1, t, SC_COLS), lambda b, j: (b, 0, j)),
                  pl.BlockSpec((taps, SC_COLS), lambda b, j: (0, j)),
                  pl.BlockSpec((1, SC_COLS), lambda b, j: (0, j))],
        out_specs=pl.BlockSpec((1, t, SC_COLS), lambda b, j: (b, 0, j)),
        out_shape=jax.ShapeDtypeStruct((bsz, t, c), BF16),
        scratch_shapes=[pltpu.VMEM((t + 3 * SC_PAD + SUBLANES, SC_COLS), F32)],
        compiler_params=_params("arbitrary", "arbitrary"),
        name="ssm_conv",
    )(xbc, conv_w, conv_b.reshape(1, c))


def _even_ffn_body(*refs, n_parts, n_ctx_tiles, n_cast):
    x = _token_tile(refs[:n_parts], n_ctx_tiles)
    a_ref, c_ref, mod_ref, n2_ref, wo_ref, w1_ref, w3_ref, w2_ref = refs[n_parts:n_parts + 8]
    cast_in = refs[n_parts + 8:n_parts + 8 + n_cast]
    o_ref = refs[n_parts + 8 + n_cast]
    cast_out = refs[n_parts + 9 + n_cast:]
    for src, dst in zip(cast_in, cast_out):
        dst[...] = src[...].astype(dst.dtype)
    ca = a_ref.shape[2]
    mix = _dot(a_ref[0], wo_ref[0:ca, :]) + _dot(c_ref[0], wo_ref[ca:, :])
    x1 = x + mod_ref[0, 0, 2:3, :] * mix
    h = _rms(x1, n2_ref[...]) * (1.0 + mod_ref[0, 0, 4:5, :]) + mod_ref[0, 0, 3:4, :]
    hb = h.astype(BF16)
    u = _silu(_dot(hb, w1_ref[...])) * _dot(hb, w3_ref[...])
    o_ref[0] = x1 + mod_ref[0, 0, 5:6, :] * _dot(u.astype(BF16), w2_ref[...])


CAST_ROW_BLOCKS = 8


def _even_ffn(parts, attn, conv, mod, norm2, wo, w1, w3, w2, n_ctx_tiles, to_cast):
    bsz, t, ca = attn.shape
    d = parts[0].shape[2]
    cc = conv.shape[2]
    nt = t // TM
    row_blocks = max([r for r in (CAST_ROW_BLOCKS, 4, 2, 1) if to_cast[0].shape[0] * r <= bsz * nt], default=0)
    if row_blocks == 0:
        extra, to_cast = tuple(w.astype(BF16) for w in to_cast), ()
    else:
        extra = ()
    n_blocks = (to_cast[0].shape[0] if to_cast else 0) * row_blocks

    def cast_block(b, i):
        blk = jnp.minimum(b * nt + i, n_blocks - 1)
        return (blk // row_blocks, blk % row_blocks, 0)

    cast_specs = [pl.BlockSpec((1, w.shape[1] // row_blocks, w.shape[2]), cast_block) for w in to_cast]
    res = pl.pallas_call(
        functools.partial(_even_ffn_body, n_parts=len(parts), n_ctx_tiles=n_ctx_tiles, n_cast=len(to_cast)),
        grid=(bsz, nt),
        in_specs=_token_specs(parts, n_ctx_tiles) + [
                  pl.BlockSpec((1, TM, ca), lambda b, i: (b, i, 0)),
                  pl.BlockSpec((1, TM, cc), lambda b, i: (b, i, 0)),
                  pl.BlockSpec((1, 1, 6, d), lambda b, i: (b, (i >= n_ctx_tiles).astype(jnp.int32), 0, 0)),
                  _resident((1, d)), _resident(wo.shape), _resident(w1.shape), _resident(w3.shape),
                  _resident(w2.shape)] + cast_specs,
        out_specs=[pl.BlockSpec((1, TM, d), lambda b, i: (b, i, 0))] + cast_specs,
        out_shape=[jax.ShapeDtypeStruct((bsz, t, d), F32)] + [jax.ShapeDtypeStruct(w.shape, BF16) for w in to_cast],
        compiler_params=_params("arbitrary", "arbitrary"),
        name="even_ffn",
    )(*parts, attn, conv, mod, norm2.reshape(1, d), wo, w1, w3, w2, *to_cast)
    return res[0], tuple(res[1:]) + extra


def _softplus(x):
    return jnp.maximum(x, 0.0) + jnp.log(1.0 + jnp.exp(-jnp.abs(x)))


def _ssd_body(xs_ref, bm_ref, cm_ref, dt_ref, dtt_ref, alr_ref, alc_ref, bir_ref, bic_ref, exp_ref, sel_ref,
              *rest, rev, n_ctx_steps, combine):
    if combine:
        yb_ref, dsk_ref, y_ref, st_s = rest
    else:
        y_ref, st_s = rest
    c = pl.program_id(1)
    nh = alr_ref.shape[1]
    hpg = nh // SSM_GROUPS
    gw = hpg * SSM_HEADDIM
    end = 0 if rev else CHUNK - 1

    @pl.when(c == 0)
    def _():
        st_s[...] = jnp.zeros_like(st_s)

    off = nh if rev else 0
    row = lax.broadcasted_iota(jnp.int32, (CHUNK, CHUNK), 0)
    col = lax.broadcasted_iota(jnp.int32, (CHUNK, CHUNK), 1)
    tri = (col >= row) if rev else (col <= row)
    trit = (row >= col) if rev else (row <= col)
    tri_b = tri.astype(BF16)
    trit_b = trit.astype(BF16)
    lane = lax.broadcasted_iota(jnp.int32, (1, LANES), 1)
    low = lane < SSM_HEADDIM

    def cat3(v):
        return jnp.concatenate(_split3(v), axis=1)

    def step(with_y, k):
        rows = slice(k * CHUNK, (k + 1) * CHUNK)
        dt = _softplus(dt_ref[0, rows, off:off + nh] + bir_ref[...])
        dtt = _softplus(dtt_ref[0, off:off + nh, rows] + bic_ref[...])
        a = dt * (-jnp.exp(alr_ref[...]))
        at = dtt * (-jnp.exp(alc_ref[...]))
        acs = sum(_dot(tri_b, p) for p in _split3(a))
        acst = sum(_dot(p, trit_b) for p in _split3(at))
        tot = acs[end:end + 1, :]
        w_end = dt * jnp.exp(tot - acs)
        eacs = jnp.exp(acs)
        cdec = jnp.exp(tot)
        pieces = [w_end, jnp.broadcast_to(cdec, (SUBLANES, nh))]
        if with_y:
            pieces = [dt, eacs] + pieces
        spread = _dot(cat3(jnp.concatenate(pieces, axis=0)), exp_ref[...])
        r0 = 2 * CHUNK if with_y else 0
        wex = spread[r0:r0 + CHUNK]
        dec = spread[r0 + CHUNK:r0 + CHUNK + 1]
        xs = xs_ref[0, rows, :].astype(F32)
        xcd = (xs * wex).astype(BF16)
        if with_y:
            xc = (xs * spread[0:CHUNK]).astype(BF16)
            eax = spread[CHUNK:2 * CHUNK]
            acs_l = _dot(cat3(acs), sel_ref[...])
        for g in range(SSM_GROUPS):
            gs = slice(g * gw, (g + 1) * gw)
            bm = bm_ref[0, rows, g * D_STATE:(g + 1) * D_STATE]
            bmt = bm.astype(F32).T.astype(BF16)
            sg = st_s[g]
            st_s[g] = sg * dec[:, gs] + _dot(bmt, xcd[:, gs])

            if not with_y:
                continue
            cmb = cm_ref[0, rows, g * D_STATE:(g + 1) * D_STATE].astype(BF16)
            cb = _dot(cmb, bmt)
            yoff = _dot(cmb, sg.astype(BF16))
            for kp in range(hpg // 2):
                e0 = g * hpg + 2 * kp
                c0 = e0 * SSM_HEADDIM
                xcb = xc[:, c0:c0 + LANES]
                res = []
                for e in (e0, e0 + 1):
                    diff = acs_l[:, e * LANES:(e + 1) * LANES] - acst[e:e + 1, :]
                    m = (jnp.exp(jnp.where(tri, diff, -jnp.inf)) * cb).astype(BF16)
                    res.append(_dot(m, xcb))
                y = jnp.where(low, res[0], res[1]) + yoff[:, kp * LANES:(kp + 1) * LANES] * eax[:, c0:c0 + LANES]
                if combine:
                    y = (y + yb_ref[0, rows, c0:c0 + LANES].astype(F32)
                         + dsk_ref[:, c0:c0 + LANES] * xs[:, c0:c0 + LANES])
                y_ref[0, rows, c0:c0 + LANES] = y.astype(y_ref.dtype)

    order = range(SSD_CPS - 1, -1, -1) if rev else range(SSD_CPS)

    @pl.when(c >= n_ctx_steps)
    def _():
        for k in order:
            step(True, k)

    @pl.when(c < n_ctx_steps)
    def _():
        for k in order:
            step(False, k)


def _ssd_scan(xbc, dt, dtt, a_log, dt_bias, ctx_len, rev, y_other=None, d_skip=None):
    bsz, t, _ = xbc.shape
    nh = a_log.shape[0]
    d_inner = nh * SSM_HEADDIM
    gn = SSM_GROUPS * D_STATE
    rows = SSD_CPS * CHUNK
    assert t % rows == 0 and ctx_len % rows == 0
    nc = t // rows
    ncc = ctx_len // rows
    seq_len = t - ctx_len
    combine = y_other is not None

    if rev:
        def chunk(i):
            return jnp.where(i < ncc, ncc - 1 - i, nc - 1 + ncc - i)
    else:
        def chunk(i):
            return i

    def ychunk(i):
        return jnp.maximum(chunk(i), ncc) - ncc if not rev else jnp.where(i < ncc, nc - 1 - ncc, chunk(i) - ncc)

    bcol = d_inner // gn
    in_specs = [pl.BlockSpec((1, rows, d_inner), lambda b, i: (b, chunk(i), 0)),
                pl.BlockSpec((1, rows, gn), lambda b, i: (b, chunk(i), bcol)),
                pl.BlockSpec((1, rows, gn), lambda b, i: (b, chunk(i), bcol + 1)),
                pl.BlockSpec((1, rows, 2 * nh), lambda b, i: (b, chunk(i), 0)),
                pl.BlockSpec((1, 2 * nh, rows), lambda b, i: (b, 0, chunk(i))),
                _resident((1, nh)), _resident((nh, 1)), _resident((1, nh)), _resident((nh, 1)),
                _resident((3 * nh, d_inner)), _resident((3 * nh, nh * LANES))]
    heads = np.arange(nh)[:, None]
    spread_p = jnp.asarray(np.tile(np.arange(d_inner)[None, :] // SSM_HEADDIM == heads, (3, 1)), BF16)
    spread_l = jnp.asarray(np.tile(np.arange(nh * LANES)[None, :] // LANES == heads, (3, 1)), BF16)
    args = [xbc, xbc, xbc, dt, dtt, a_log.reshape(1, nh), a_log.reshape(nh, 1),
            dt_bias.reshape(1, nh), dt_bias.reshape(nh, 1), spread_p, spread_l]
    if combine:
        in_specs += [pl.BlockSpec((1, rows, d_inner), lambda b, i: (b, ychunk(i), 0)), _resident((1, d_inner))]
        args += [y_other, jnp.repeat(d_skip, SSM_HEADDIM).reshape(1, d_inner)]
    return pl.pallas_call(
        functools.partial(_ssd_body, rev=rev, n_ctx_steps=ncc, combine=combine),
        grid=(bsz, nc),
        in_specs=in_specs,
        out_specs=pl.BlockSpec((1, rows, d_inner), lambda b, i: (b, ychunk(i), 0)),
        out_shape=jax.ShapeDtypeStruct((bsz, seq_len, d_inner), BF16),
        scratch_shapes=[pltpu.VMEM((SSM_GROUPS, D_STATE, d_inner // SSM_GROUPS), F32)],
        compiler_params=_params("arbitrary", "arbitrary"),
        name="ssd_bwd" if rev else "ssd_fwd",
    )(*args)


def _odd_out_body(x_ref, y_ref, z_ref, mod_ref, gn_ref, n2_ref, wout_ref, wr_ref, x1_ref, h_ref, ti_ref, tw_ref):
    z = z_ref[0].astype(F32)
    yn = _rms(y_ref[0].astype(F32) * _silu(z), gn_ref[...])
    x1 = x_ref[0] + mod_ref[0, 0, 2:3, :] * _dot(yn.astype(BF16), wout_ref[...])
    x1_ref[0] = x1
    h = _rms(x1, n2_ref[...]) * (1.0 + mod_ref[0, 0, 4:5, :]) + mod_ref[0, 0, 3:4, :]
    h_ref[0] = h
    ne = wr_ref.shape[1]
    h1, h2, _ = _split3(h)
    w1, w2, _ = _split3(wr_ref[...])
    hw = _dot(h1, jnp.concatenate([w1, w2], axis=1))
    lg = hw[:, 0:ne] + hw[:, ne:2 * ne] + _dot(h2, w1)
    idx = lax.broadcasted_iota(jnp.int32, lg.shape, 1)
    m1 = jnp.max(lg, axis=-1, keepdims=True)
    i1 = jnp.min(jnp.where(lg == m1, idx, ne), axis=-1, keepdims=True)
    lg2 = jnp.where(idx == i1, -jnp.inf, lg)
    m2 = jnp.max(lg2, axis=-1, keepdims=True)
    i2 = jnp.min(jnp.where(lg2 == m2, idx, ne), axis=-1, keepdims=True)
    e2 = jnp.exp(m2 - m1)
    den = 1.0 + e2
    ti_ref[0] = jnp.concatenate([i1, i2], axis=1)
    tw_ref[0] = jnp.concatenate([1.0 / den, e2 / den], axis=1)


def _odd_out(x, y, z, mod, gnorm, norm2, w_out, w_router, ctx_len):
    bsz, t, d = x.shape
    seq_len = t - ctx_len
    di = y.shape[2]
    ne = w_router.shape[1]
    nct = ctx_len // TM
    lat = lambda b, i: (b, i + nct, 0)
    own = lambda b, i: (b, i, 0)
    return pl.pallas_call(
        _odd_out_body,
        grid=(bsz, seq_len // TM),
        in_specs=[pl.BlockSpec((1, TM, d), lat),
                  pl.BlockSpec((1, TM, di), own),
                  pl.BlockSpec((1, TM, di), lat),
                  pl.BlockSpec((1, 1, 6, d), lambda b, i: (b, 1, 0, 0)),
                  _resident((1, di)), _resident((1, d)), _resident(w_out.shape), _resident((d, ne))],
        out_specs=[pl.BlockSpec((1, TM, d), own), pl.BlockSpec((1, TM, d), own),
                   pl.BlockSpec((1, TM, TOP_K), own), pl.BlockSpec((1, TM, TOP_K), own)],
        out_shape=[jax.ShapeDtypeStruct((bsz, seq_len, d), F32), jax.ShapeDtypeStruct((bsz, seq_len, d), F32),
                   jax.ShapeDtypeStruct((bsz, seq_len, TOP_K), jnp.int32),
                   jax.ShapeDtypeStruct((bsz, seq_len, TOP_K), F32)],
        compiler_params=_params("arbitrary", "arbitrary"),
        name="odd_out",
    )(x, y, z, mod, gnorm.reshape(1, di), norm2.reshape(1, d), w_out, w_router)


def _expert_body(te_ref, na_ref, rows_ref, next_rows_ref, dst_ref, h_hbm, w1_ref, w3_ref, w2_ref, y_hbm,
                 xf_s, xb_s, acc_s, out_s, gsem, ssem, *, nf, nt):
    ti = pl.program_id(0)
    f = pl.program_id(1)
    real = ti < nt
    active = ti < na_ref[0]
    part = TM_E // nf
    base = f * part
    last = f == nf - 1

    def gather_row(idx_ref, q, j, priority):
        pltpu.make_async_copy(h_hbm.at[pl.ds(idx_ref[0, 0, q * part + j], 1), :],
                              xf_s.at[q, pl.ds(j, 1), :], gsem).start(priority=priority)

    @pl.when(jnp.logical_and(ti == 0, f == 0))
    def _():
        out_s[...] = jnp.zeros_like(out_s)
        for q in range(nf):
            def body(j, carry):
                gather_row(rows_ref, q, 2 * j, 0)
                gather_row(rows_ref, q, 2 * j + 1, 1)
                return carry
            lax.fori_loop(0, part // 2, body, 0)

    @pl.when(f == 0)
    def _():
        for q in range(nf):
            pltpu.make_async_copy(h_hbm.at[pl.ds(0, part), :], xf_s.at[q], gsem).wait()

    @pl.when(jnp.logical_and(active, f == 0))
    def _():
        for q in range(nf):
            xb_s[q * part:(q + 1) * part, :] = xf_s[q].astype(BF16)
        acc_s[...] = jnp.zeros_like(acc_s)

    def scatter_part():
        for j in range(part):
            pltpu.make_async_copy(out_s.at[f, pl.ds(j, 1), :],
                                  y_hbm.at[pl.ds(dst_ref[0, 0, base + j], 1), :], ssem).start(priority=1)

    def gather_part():
        for j in range(part):
            gather_row(next_rows_ref, f, j, 1)

    @pl.when(active)
    def _():
        scatter_part()
        gather_part()
        xb = xb_s[...]
        u = _silu(_dot(xb, w1_ref[0])) * _dot(xb, w3_ref[0])
        acc_s[...] += _dot(u.astype(BF16), w2_ref[0])

    @pl.when(jnp.logical_and(real, jnp.logical_not(active)))
    def _():
        scatter_part()
        gather_part()

    @pl.when(jnp.logical_not(real))
    def _():
        scatter_part()

    @pl.when(last)
    def _():
        for q in range(nf):
            pltpu.make_async_copy(out_s.at[q], y_hbm.at[pl.ds(0, part), :], ssem).wait()

    @pl.when(jnp.logical_and(active, last))
    def _():
        for q in range(nf):
            out_s[q] = acc_s[q * part:(q + 1) * part, :]


def _expert_ffn(h, tile_expert, n_active, row_token, row_dst, n_y_rows, w1, w3, w2):
    n, d = h.shape
    ns = row_token.shape[0]
    nt = ns // TM_E
    dff = w1.shape[2]
    nf = dff // TF_E

    def wcol(ti, f, te, na):
        return (te[jnp.minimum(ti, nt - 1)], 0, jnp.where(ti < na[0], f, nf - 1))

    def wrow(ti, f, te, na):
        return (te[jnp.minimum(ti, nt - 1)], jnp.where(ti < na[0], f, nf - 1), 0)

    def idx_block(shift):
        return pl.BlockSpec((1, 1, TM_E), lambda ti, f, te, na: (jnp.minimum(ti + shift, nt), 0, 0),
                            memory_space=pltpu.SMEM)

    grid_spec = pltpu.PrefetchScalarGridSpec(
        num_scalar_prefetch=2,
        grid=(nt + 1, nf),
        in_specs=[idx_block(0), idx_block(1), idx_block(0),
                  pl.BlockSpec(memory_space=pl.ANY),
                  pl.BlockSpec((1, d, TF_E), wcol),
                  pl.BlockSpec((1, d, TF_E), wcol),
                  pl.BlockSpec((1, TF_E, d), wrow)],
        out_specs=pl.BlockSpec(memory_space=pl.ANY),
        scratch_shapes=[pltpu.VMEM((nf, TM_E // nf, d), F32), pltpu.VMEM((TM_E, d), BF16),
                        pltpu.VMEM((TM_E, d), F32), pltpu.VMEM((nf, TM_E // nf, d), F32),
                        pltpu.SemaphoreType.DMA(()), pltpu.SemaphoreType.DMA(())],
    )
    rows = jnp.concatenate([row_token, jnp.zeros((TM_E,), jnp.int32)]).reshape(nt + 1, 1, TM_E)
    dst = row_dst.reshape(nt + 1, 1, TM_E)
    return pl.pallas_call(
        functools.partial(_expert_body, nf=nf, nt=nt),
        grid_spec=grid_spec,
        out_shape=jax.ShapeDtypeStruct((n_y_rows, d), F32),
        compiler_params=_params("arbitrary", "arbitrary"),
        name="expert_ffn",
    )(tile_expert, n_active, rows, rows, dst, h, w1, w3, w2)


def _combine_body(x_ref, y0_ref, y1_ref, tw_ref, g2_ref, fn_ref, o_ref):
    tw = tw_ref[...]
    moe = tw[:, 0:1] * y0_ref[...] + tw[:, 1:2] * y1_ref[...]
    o_ref[...] = _rms(x_ref[...] + g2_ref[0] * moe, fn_ref[...])


def _moe_combine(x1, topw, g2, final_norm, y, tiles_per_batch):
    n, d = x1.shape
    nt = n // TM_C
    return pl.pallas_call(
        _combine_body,
        grid=(nt,),
        in_specs=[pl.BlockSpec((TM_C, d), lambda i: (i, 0)),
                  pl.BlockSpec((TM_C, d), lambda i: (i, 0)),
                  pl.BlockSpec((TM_C, d), lambda i: (i + nt, 0)),
                  pl.BlockSpec((TM_C, TOP_K), lambda i: (i, 0)),
                  pl.BlockSpec((1, 1, d), lambda i: (i // tiles_per_batch, 0, 0)),
                  _resident((1, d))],
        out_specs=pl.BlockSpec((TM_C, d), lambda i: (i, 0)),
        out_shape=jax.ShapeDtypeStruct((n, d), F32),
        compiler_params=_params("arbitrary"),
        name="moe_combine",
    )(x1, y, y, topw, g2, final_norm.reshape(1, d))


def _route(topi, n_slots):
    n = topi.shape[0]
    e_flat = topi.reshape(n * TOP_K)
    oh = (e_flat[:, None] == jnp.arange(N_EXPERTS, dtype=jnp.int32)[None, :]).astype(jnp.int32)
    csum = jnp.cumsum(oh, axis=0)
    rank = jnp.sum(csum * oh, axis=1) - 1
    cnt = csum[-1]
    cnt_pad = (cnt + TM_E - 1) // TM_E * TM_E
    ends = jnp.cumsum(cnt_pad)
    offs = ends - cnt_pad
    slot = jnp.sum(oh * offs[None, :], axis=1) + rank
    n_active = (ends[-1] // TM_E).astype(jnp.int32)
    nt = n_slots // TM_E
    tile_start = jnp.arange(nt, dtype=jnp.int32) * TM_E
    te = jnp.sum((tile_start[:, None] >= ends[None, :]).astype(jnp.int32), axis=1)
    te = jnp.minimum(te, N_EXPERTS - 1)
    te = jnp.where(jnp.arange(nt) < n_active, te, te[jnp.maximum(n_active - 1, 0)])
    n_routed = n * TOP_K
    owner = jnp.full((n_slots,), -1, jnp.int32).at[slot].set(
        jnp.arange(n_routed, dtype=jnp.int32), unique_indices=True, mode="promise_in_bounds")
    is_pad = owner < 0
    spare = n_routed + TM_E + jnp.cumsum(is_pad.astype(jnp.int32)) - 1
    row_token = jnp.where(is_pad, 0, owner // TOP_K)
    routed = (owner % TOP_K) * n + owner // TOP_K
    row_dst = jnp.concatenate([n_routed + jnp.arange(TM_E, dtype=jnp.int32), jnp.where(is_pad, spare, routed)])
    n_y_rows = n_slots + TM_E
    return te.astype(jnp.int32), n_active.reshape(1), row_token, row_dst, n_y_rows


def kernel(x, c, ctx, c_ctx, ev_ada_w, ev_ada_b, ev_norm1, ev_norm2, ev_w_in, ev_q_gain, ev_k_gain, ev_dw_w, ev_dw_b, ev_ln_g, ev_ln_b, ev_w_o, ev_ff_w1, ev_ff_w3, ev_ff_w2, od_ada_w, od_ada_b, od_norm1, od_norm2, od_w_in, od_conv_w, od_conv_b, od_a_log_f, od_a_log_b, od_dt_bias_f, od_dt_bias_b, od_d_skip, od_gnorm, od_w_out, od_router, od_ex_w1, od_ex_w3, od_ex_w2, final_norm):
    bsz, seq_len, d = x.shape
    ctx_len = ctx.shape[1]
    assert ev_ada_w.shape[0] == 1 and od_ada_w.shape[0] == 1, "one even and one odd layer"
    assert ctx_len % TM == 0 and seq_len % TM == 0 and seq_len % GRID_W == 0
    n_ctx_tiles = ctx_len // TM
    mod = _mod_table(c, c_ctx, ev_ada_w[0], ev_ada_b[0])
    cc = ev_dw_w.shape[2]
    splits = [(0, ATTN_W), (ATTN_W, ATTN_W + KV_W), (ATTN_W + KV_W, ATTN_W + 2 * KV_W),
              (ATTN_W + 2 * KV_W, ATTN_W + 2 * KV_W + 2 * cc)]
    q, k, v, u = _inproj((ctx, x), mod, ev_norm1[0], ev_w_in[0].astype(BF16), splits, (F32, F32, BF16, F32),
                         n_ctx_tiles)
    cos2, sin2 = _rope_tables(ctx_len, seq_len)
    attn = _attention(q, k, v, cos2, sin2, ev_q_gain[0], ev_k_gain[0], ctx_len)
    conv = _conformer_conv(u, ev_dw_w[0], ev_dw_b[0], ev_ln_g[0], ev_ln_b[0], ctx_len)
    xa, (ex_w1, ex_w3, ex_w2) = _even_ffn(
        (ctx, x), attn, conv, mod, ev_norm2[0], ev_w_o[0].astype(BF16), ev_ff_w1[0].astype(BF16),
        ev_ff_w3[0].astype(BF16), ev_ff_w2[0].astype(BF16), n_ctx_tiles, (od_ex_w1[0], od_ex_w3[0], od_ex_w2[0]))

    mod = _mod_table(c, c_ctx, od_ada_w[0], od_ada_b[0])
    nh = od_a_log_f.shape[1]
    d_inner = nh * SSM_HEADDIM
    conv_dim = od_conv_w.shape[2]
    splits = [(0, d_inner), (d_inner, d_inner + conv_dim), (d_inner + conv_dim, d_inner + conv_dim + 2 * nh)]
    z, xbc, dt = _inproj((xa,), mod, od_norm1[0], od_w_in[0].astype(BF16), splits, (BF16, F32, F32), n_ctx_tiles)
    dtt = dt.transpose(0, 2, 1)
    xbc = _ssm_conv(xbc, od_conv_w[0], od_conv_b[0], ctx_len)
    y_b = _ssd_scan(xbc, dt, dtt, od_a_log_b[0], od_dt_bias_b[0], ctx_len, rev=True)
    y = _ssd_scan(xbc, dt, dtt, od_a_log_f[0], od_dt_bias_f[0], ctx_len, rev=False, y_other=y_b,
                  d_skip=od_d_skip[0])
    x1, h, topi, topw = _odd_out(xa, y, z, mod, od_gnorm[0], od_norm2[0], od_w_out[0].astype(BF16),
                                 od_router[0], ctx_len)

    n = bsz * seq_len
    n_slots = n * TOP_K + N_EXPERTS * TM_E
    tile_expert, n_active, row_token, row_dst, n_y_rows = _route(topi.reshape(n, TOP_K), n_slots)
    y_moe = _expert_ffn(h.reshape(n, d), tile_expert, n_active, row_token, row_dst, n_y_rows,
                        ex_w1, ex_w3, ex_w2)
    g2 = mod[:, 1, 5:6, :]
    out = _moe_combine(x1.reshape(n, d), topw.reshape(n, TOP_K), g2, final_norm, y_moe, seq_len // TM_C)
    return out.reshape(bsz, seq_len, d)
```

```python
import functools

import jax
import jax.numpy as jnp
import numpy as np
from jax import lax
from jax.experimental import pallas as pl
from jax.experimental.pallas import tpu as pltpu

F32 = jnp.float32
BF16 = jnp.bfloat16
EPS = 1e-6
LOG2_E = 1.4426950408889634

GRID_W = 64
HEAD_DIM = 64
ATTN_HEADS = 8
KV_HEADS = 2
ATTN_W = ATTN_HEADS * HEAD_DIM
KV_W = KV_HEADS * HEAD_DIM
ROPE_THETA = 10000.0
SSM_HEADDIM = 64
SSM_GROUPS = 4
D_STATE = 128
CHUNK = 128
N_EXPERTS = 8
TOP_K = 2

LANES = 128
SUBLANES = 8
VMEM_LIMIT = 56 * 1024 * 1024

TM = 256
TM_E = 512
TF_E = 1792
TM_C = 256
SSD_CPS = 2


def _params(*sem):
    return pltpu.CompilerParams(dimension_semantics=sem, vmem_limit_bytes=VMEM_LIMIT)


def _resident(shape):
    nd = len(shape)
    return pl.BlockSpec(shape, lambda *_: (0,) * nd, pipeline_mode=pl.Buffered(1))


def _silu(x):
    return x * jax.nn.sigmoid(x)


def _rms(x, g):
    return x * lax.rsqrt(jnp.mean(x * x, axis=-1, keepdims=True) + EPS) * g


def _dot(a, b):
    return jnp.dot(a, b, preferred_element_type=F32)


def _split3(a):
    a1 = a.astype(BF16)
    r1 = a - a1.astype(F32)
    a2 = r1.astype(BF16)
    a3 = (r1 - a2.astype(F32)).astype(BF16)
    return a1, a2, a3


def _ada_body(c_ref, w_ref, b_ref, o_ref):
    s = _silu(c_ref[...]).astype(BF16)
    o_ref[...] = _dot(s, w_ref[...].astype(BF16)) + b_ref[...]


def _adaln(cond, w, b):
    r, d = cond.shape
    n = w.shape[1]
    tn = n // 4
    return pl.pallas_call(
        _ada_body,
        grid=(n // tn,),
        in_specs=[pl.BlockSpec((r, d), lambda j: (0, 0)),
                  pl.BlockSpec((d, tn), lambda j: (0, j)),
                  pl.BlockSpec((1, tn), lambda j: (0, j))],
        out_specs=pl.BlockSpec((r, tn), lambda j: (0, j)),
        out_shape=jax.ShapeDtypeStruct((r, n), F32),
        compiler_params=_params("arbitrary"),
        name="adaln",
    )(cond, w, b.reshape(1, n))


def _mod_table(c, c_ctx, w, b):
    bsz, d = c.shape
    rows = -(-(bsz + 1) // SUBLANES) * SUBLANES
    cond = jnp.zeros((rows, d), F32).at[:bsz].set(c).at[bsz].set(c_ctx)
    m = _adaln(cond, w, b)
    lat = m[:bsz]
    ctx = jnp.broadcast_to(m[bsz][None], lat.shape)
    return jnp.stack([ctx, lat], axis=1).reshape(bsz, 2, 6, d)


def _token_specs(parts, n_ctx_tiles):
    d = parts[0].shape[2]
    if len(parts) == 1:
        return [pl.BlockSpec((1, TM, d), lambda b, i: (b, i, 0))]
    return [pl.BlockSpec((1, TM, d), lambda b, i: (b, jnp.minimum(i, n_ctx_tiles - 1), 0)),
            pl.BlockSpec((1, TM, d), lambda b, i: (b, jnp.maximum(i - n_ctx_tiles, 0), 0))]


def _token_tile(refs, n_ctx_tiles):
    if len(refs) == 1:
        return refs[0][0]
    return jnp.where(pl.program_id(1) < n_ctx_tiles, refs[0][0], refs[1][0])


def _inproj_body(*refs, splits, n_parts, n_ctx_tiles):
    x = _token_tile(refs[:n_parts], n_ctx_tiles)
    mod_ref, n_ref, w_ref = refs[n_parts:n_parts + 3]
    out_refs = refs[n_parts + 3:]
    h = _rms(x, n_ref[...]) * (1.0 + mod_ref[0, 0, 1:2, :]) + mod_ref[0, 0, 0:1, :]
    r = _dot(h.astype(BF16), w_ref[...])
    for o_ref, (lo, hi) in zip(out_refs, splits):
        o_ref[0] = r[:, lo:hi].astype(o_ref.dtype)


def _inproj(parts, mod, norm, w_bf16, splits, dtypes, n_ctx_tiles):
    bsz, _, d = parts[0].shape
    t = sum(p.shape[1] for p in parts)
    n = w_bf16.shape[1]
    widths = [hi - lo for lo, hi in splits]
    return pl.pallas_call(
        functools.partial(_inproj_body, splits=tuple(splits), n_parts=len(parts), n_ctx_tiles=n_ctx_tiles),
        grid=(bsz, t // TM),
        in_specs=_token_specs(parts, n_ctx_tiles) + [
            pl.BlockSpec((1, 1, 6, d), lambda b, i: (b, (i >= n_ctx_tiles).astype(jnp.int32), 0, 0)),
            _resident((1, d)),
            _resident((d, n))],
        out_specs=[pl.BlockSpec((1, TM, wd), lambda b, i: (b, i, 0)) for wd in widths],
        out_shape=[jax.ShapeDtypeStruct((bsz, t, wd), dt) for wd, dt in zip(widths, dtypes)],
        compiler_params=_params("arbitrary", "arbitrary"),
        name="inproj",
    )(*parts, mod, norm.reshape(1, d), w_bf16)


def _attn_body(q_ref, k_ref, v_ref, cq_ref, sq_ref, ck_ref, sk_ref, qg_ref, kg_ref, o_ref, kt_s, v2_s,
               *, n_ctx_tiles, ctx_len):
    i = pl.program_id(1)
    t = k_ref.shape[1]
    lane = lax.broadcasted_iota(jnp.int32, (1, LANES), 1)
    low = lane < HEAD_DIM
    even = (lane % 2) == 0

    def norm_rope(x, gain, cos, sin):
        x2 = x * x
        s_lo = jnp.sum(jnp.where(low, x2, 0.0), axis=-1, keepdims=True)
        s_hi = jnp.sum(jnp.where(low, 0.0, x2), axis=-1, keepdims=True)
        ms = jnp.where(low, s_lo, s_hi) * (1.0 / HEAD_DIM)
        xn = x * lax.rsqrt(ms + EPS) * gain
        swapped = jnp.where(even, pltpu.roll(xn, LANES - 1, 1), pltpu.roll(xn, 1, 1))
        return xn * cos + swapped * sin

    @pl.when(i == 0)
    def _():
        k = norm_rope(k_ref[0], kg_ref[...], ck_ref[...], sk_ref[...])
        kr = pltpu.roll(k, HEAD_DIM, 1)
        kt_s[0] = jnp.where(low, k, kr).T.astype(BF16)
        kt_s[1] = jnp.where(low, kr, k).T.astype(BF16)
        v = v_ref[0].astype(F32)
        vr = pltpu.roll(v, HEAD_DIM, 1)
        v2_s[0] = jnp.where(low, v, vr).astype(BF16)
        v2_s[1] = jnp.where(low, vr, v).astype(BF16)

    def run(tk):
        for j in range(ATTN_HEADS // 2):
            g = (2 * j) // (ATTN_HEADS // KV_HEADS)
            qp = norm_rope(q_ref[0, :, j * LANES:(j + 1) * LANES], qg_ref[...], cq_ref[...], sq_ref[...])
            qp = qp * (HEAD_DIM ** -0.5 * LOG2_E)
            outs = []
            for hh in range(2):
                qm = jnp.where(low if hh == 0 else jnp.logical_not(low), qp, 0.0).astype(BF16)
                s = _dot(qm, kt_s[g, :, 0:tk])
                m = jnp.max(s, axis=-1, keepdims=True)
                p = jnp.exp2(s - m)
                l = jnp.sum(p, axis=-1, keepdims=True)
                o = _dot(p.astype(BF16), v2_s[g, 0:tk, :])
                outs.append(o / l)
            o_ref[0, :, j * LANES:(j + 1) * LANES] = jnp.where(low, outs[0], outs[1]).astype(o_ref.dtype)

    @pl.when(i < n_ctx_tiles)
    def _():
        run(ctx_len)

    @pl.when(i >= n_ctx_tiles)
    def _():
        run(t)


def _attention(q, k, v, cos2, sin2, q_gain, k_gain, ctx_len):
    bsz, t, _ = q.shape
    n_ctx_tiles = ctx_len // TM
    qg = jnp.tile(q_gain.reshape(1, HEAD_DIM), (1, 2))
    kg = jnp.tile(k_gain.reshape(1, HEAD_DIM), (1, 2))
    return pl.pallas_call(
        functools.partial(_attn_body, n_ctx_tiles=n_ctx_tiles, ctx_len=ctx_len),
        grid=(bsz, t // TM),
        in_specs=[pl.BlockSpec((1, TM, ATTN_W), lambda b, i: (b, i, 0)),
                  pl.BlockSpec((1, t, KV_W), lambda b, i: (b, 0, 0)),
                  pl.BlockSpec((1, t, KV_W), lambda b, i: (b, 0, 0)),
                  pl.BlockSpec((TM, LANES), lambda b, i: (i, 0)),
                  pl.BlockSpec((TM, LANES), lambda b, i: (i, 0)),
                  _resident((t, LANES)),
                  _resident((t, LANES)),
                  _resident((1, LANES)),
                  _resident((1, LANES))],
        out_specs=pl.BlockSpec((1, TM, ATTN_W), lambda b, i: (b, i, 0)),
        out_shape=jax.ShapeDtypeStruct((bsz, t, ATTN_W), BF16),
        scratch_shapes=[pltpu.VMEM((KV_HEADS, LANES, t), BF16), pltpu.VMEM((KV_HEADS, t, LANES), BF16)],
        compiler_params=_params("arbitrary", "arbitrary"),
        name="attention",
    )(q, k, v, cos2, sin2, cos2, sin2, qg, kg)


def _rope_tables(ctx_len, seq_len):
    f32 = np.float32
    rows = seq_len // GRID_W
    t_row = np.repeat(np.arange(rows, dtype=f32), GRID_W)
    t_col = np.tile(np.arange(GRID_W, dtype=f32), rows)
    axis_dim = HEAD_DIM // 2
    inv_freq = (f32(ROPE_THETA) ** (-np.arange(0, axis_dim, 2, dtype=f32) / f32(axis_dim))).astype(f32)
    ang = np.concatenate([t_row[:, None] * inv_freq, t_col[:, None] * inv_freq], axis=-1)
    cos = np.repeat(np.cos(ang), 2, axis=-1)
    sin = np.repeat(np.sin(ang), 2, axis=-1) * np.tile(np.array([-1.0, 1.0], f32), axis_dim)
    cos = np.concatenate([np.ones((ctx_len, HEAD_DIM), f32), cos], axis=0)
    sin = np.concatenate([np.zeros((ctx_len, HEAD_DIM), f32), sin], axis=0)
    return jnp.asarray(np.tile(cos, (1, 2)), F32), jnp.asarray(np.tile(sin, (1, 2)), F32)


def _fill_padded(src_ref, pad_s, width, ctx_len, seq_len, pad, rows, fn):
    t = ctx_len + seq_len
    z = jnp.zeros((pad + SUBLANES, width), F32)
    pad_s[0:pad, :] = z[0:pad]
    pad_s[pad + ctx_len:2 * pad + ctx_len, :] = z[0:pad]
    pad_s[2 * pad + t:3 * pad + t + SUBLANES, :] = z

    def seg(tok0, off, ntiles):
        def body(n, carry):
            r = pl.multiple_of(tok0 + n * rows, rows)
            pad_s[pl.ds(r + off, rows), :] = fn(src_ref, r, rows)
            return carry
        lax.fori_loop(0, ntiles, body, 0)

    seg(0, pad, ctx_len // rows)
    seg(ctx_len, 2 * pad, seq_len // rows)


def _conv_segments(pad_s, w_ref, taps, ctx_len, seq_len, pad, rows, emit):
    half = taps // 2
    span = rows + SUBLANES

    def seg(tok0, off, ntiles):
        def body(n, carry):
            r = pl.multiple_of(tok0 + n * rows, rows)
            base = r + off - pad
            acc = None
            for rho in range(SUBLANES):
                part = None
                for j in range(taps):
                    dj = pad - half + j
                    if dj % SUBLANES != rho:
                        continue
                    term = w_ref[j:j + 1, :] * pad_s[pl.ds(pl.multiple_of(base + (dj - rho), SUBLANES), span), :]
                    part = term if part is None else part + term
                if part is None:
                    continue
                if rho:
                    part = pltpu.roll(part, span - rho, 0)
                acc = part[0:rows] if acc is None else acc + part[0:rows]
            emit(r, acc)
            return carry
        lax.fori_loop(0, ntiles, body, 0)

    seg(0, pad, ctx_len // rows)
    seg(ctx_len, 2 * pad, seq_len // rows)


CC_PAD = 16
CC_ROWS = 32


def _cconv_body(u_ref, w_ref, b_ref, g_ref, bb_ref, o_ref, pad_s, *, ctx_len, seq_len):
    c = o_ref.shape[2]

    def glu(src_ref, r, rows):
        return src_ref[0, pl.ds(r, rows), 0:c] * jax.nn.sigmoid(src_ref[0, pl.ds(r, rows), c:2 * c])

    _fill_padded(u_ref, pad_s, c, ctx_len, seq_len, CC_PAD, TM, glu)

    def emit(r, acc):
        h = acc + b_ref[...]
        mu = jnp.mean(h, axis=-1, keepdims=True)
        hc = h - mu
        var = jnp.mean(hc * hc, axis=-1, keepdims=True)
        y = hc * lax.rsqrt(var + EPS) * g_ref[...] + bb_ref[...]
        o_ref[0, pl.ds(r, CC_ROWS), :] = _silu(y).astype(o_ref.dtype)

    _conv_segments(pad_s, w_ref, w_ref.shape[0], ctx_len, seq_len, CC_PAD, CC_ROWS, emit)


def _conformer_conv(u, dw_w, dw_b, ln_g, ln_b, ctx_len):
    bsz, t, c2 = u.shape
    c = c2 // 2
    taps = dw_w.shape[0]
    return pl.pallas_call(
        functools.partial(_cconv_body, ctx_len=ctx_len, seq_len=t - ctx_len),
        grid=(bsz,),
        in_specs=[pl.BlockSpec((1, t, c2), lambda b: (b, 0, 0)),
                  _resident((taps, c)), _resident((1, c)), _resident((1, c)), _resident((1, c))],
        out_specs=pl.BlockSpec((1, t, c), lambda b: (b, 0, 0)),
        out_shape=jax.ShapeDtypeStruct((bsz, t, c), BF16),
        scratch_shapes=[pltpu.VMEM((t + 3 * CC_PAD + SUBLANES, c), F32)],
        compiler_params=_params("arbitrary"),
        name="conformer_conv",
    )(u, dw_w, dw_b.reshape(1, c), ln_g.reshape(1, c), ln_b.reshape(1, c))


SC_PAD = 8
SC_ROWS = 64
SC_COLS = 512


def _sconv_body(x_ref, w_ref, b_ref, o_ref, pad_s, *, ctx_len, seq_len):
    def ident(src_ref, r, rows):
        return src_ref[0, pl.ds(r, rows), :]

    _fill_padded(x_ref, pad_s, SC_COLS, ctx_len, seq_len, SC_PAD, TM, ident)

    def emit(r, acc):
        o_ref[0, pl.ds(r, SC_ROWS), :] = _silu(acc + b_ref[...]).astype(o_ref.dtype)

    _conv_segments(pad_s, w_ref, w_ref.shape[0], ctx_len, seq_len, SC_PAD, SC_ROWS, emit)


def _ssm_conv(xbc, conv_w, conv_b, ctx_len):
    bsz, t, c = xbc.shape
    taps = conv_w.shape[0]
    return pl.pallas_call(
        functools.partial(_sconv_body, ctx_len=ctx_len, seq_len=t - ctx_len),
        grid=(bsz, c // SC_COLS),
        in_specs=[pl.BlockSpec((1, t, SC_COLS), lambda b, j: (b, 0, j)),
                  pl.BlockSpec((taps, SC_COLS), lambda b, j: (0, j)),
                  pl.BlockSpec((1, SC_COLS), lambda b, j: (0, j))],
        out_specs=pl.BlockSpec((1, t, SC_COLS), lambda b, j: (b, 0, j)),
        out_shape=jax.ShapeDtypeStruct((bsz, t, c), BF16),
        scratch_shapes=[pltpu.VMEM((t + 3 * SC_PAD + SUBLANES, SC_COLS), F32)],
        compiler_params=_params("arbitrary", "arbitrary"),
        name="ssm_conv",
    )(xbc, conv_w, conv_b.reshape(1, c))


def _even_ffn_body(*refs, n_parts, n_ctx_tiles, n_cast):
    x = _token_tile(refs[:n_parts], n_ctx_tiles)
    a_ref, c_ref, mod_ref, n2_ref, wo_ref, w1_ref, w3_ref, w2_ref = refs[n_parts:n_parts + 8]
    cast_in = refs[n_parts + 8:n_parts + 8 + n_cast]
    o_ref = refs[n_parts + 8 + n_cast]
    cast_out = refs[n_parts + 9 + n_cast:]
    for src, dst in zip(cast_in, cast_out):
        dst[...] = src[...].astype(dst.dtype)
    ca = a_ref.shape[2]
    mix = _dot(a_ref[0], wo_ref[0:ca, :]) + _dot(c_ref[0], wo_ref[ca:, :])
    x1 = x + mod_ref[0, 0, 2:3, :] * mix
    h = _rms(x1, n2_ref[...]) * (1.0 + mod_ref[0, 0, 4:5, :]) + mod_ref[0, 0, 3:4, :]
    hb = h.astype(BF16)
    u = _silu(_dot(hb, w1_ref[...])) * _dot(hb, w3_ref[...])
    o_ref[0] = x1 + mod_ref[0, 0, 5:6, :] * _dot(u.astype(BF16), w2_ref[...])


CAST_ROW_BLOCKS = 8


def _even_ffn(parts, attn, conv, mod, norm2, wo, w1, w3, w2, n_ctx_tiles, to_cast):
    bsz, t, ca = attn.shape
    d = parts[0].shape[2]
    cc = conv.shape[2]
    nt = t // TM
    row_blocks = max([r for r in (CAST_ROW_BLOCKS, 4, 2, 1) if to_cast[0].shape[0] * r <= bsz * nt], default=0)
    if row_blocks == 0:
        extra, to_cast = tuple(w.astype(BF16) for w in to_cast), ()
    else:
        extra = ()
    n_blocks = (to_cast[0].shape[0] if to_cast else 0) * row_blocks

    def cast_block(b, i):
        blk = jnp.minimum(b * nt + i, n_blocks - 1)
        return (blk // row_blocks, blk % row_blocks, 0)

    cast_specs = [pl.BlockSpec((1, w.shape[1] // row_blocks, w.shape[2]), cast_block) for w in to_cast]
    res = pl.pallas_call(
        functools.partial(_even_ffn_body, n_parts=len(parts), n_ctx_tiles=n_ctx_tiles, n_cast=len(to_cast)),
        grid=(bsz, nt),
        in_specs=_token_specs(parts, n_ctx_tiles) + [
                  pl.BlockSpec((1, TM, ca), lambda b, i: (b, i, 0)),
                  pl.BlockSpec((1, TM, cc), lambda b, i: (b, i, 0)),
                  pl.BlockSpec((1, 1, 6, d), lambda b, i: (b, (i >= n_ctx_tiles).astype(jnp.int32), 0, 0)),
                  _resident((1, d)), _resident(wo.shape), _resident(w1.shape), _resident(w3.shape),
                  _resident(w2.shape)] + cast_specs,
        out_specs=[pl.BlockSpec((1, TM, d), lambda b, i: (b, i, 0))] + cast_specs,
        out_shape=[jax.ShapeDtypeStruct((bsz, t, d), F32)] + [jax.ShapeDtypeStruct(w.shape, BF16) for w in to_cast],
        compiler_params=_params("arbitrary", "arbitrary"),
        name="even_ffn",
    )(*parts, attn, conv, mod, norm2.reshape(1, d), wo, w1, w3, w2, *to_cast)
    return res[0], tuple(res[1:]) + extra


def _softplus(x):
    return jnp.maximum(x, 0.0) + jnp.log(1.0 + jnp.exp(-jnp.abs(x)))


def _ssd_body(xs_ref, bm_ref, cm_ref, dt_ref, dtt_ref, alr_ref, alc_ref, bir_ref, bic_ref, exp_ref, sel_ref,
              *rest, rev, n_ctx_steps, combine):
    if combine:
        yb_ref, dsk_ref, y_ref, st_s = rest
    else:
        y_ref, st_s = rest
    c = pl.program_id(1)
    nh = alr_ref.shape[1]
    hpg = nh // SSM_GROUPS
    gw = hpg * SSM_HEADDIM
    end = 0 if rev else CHUNK - 1

    @pl.when(c == 0)
    def _():
        st_s[...] = jnp.zeros_like(st_s)

    off = nh if rev else 0
    row = lax.broadcasted_iota(jnp.int32, (CHUNK, CHUNK), 0)
    col = lax.broadcasted_iota(jnp.int32, (CHUNK, CHUNK), 1)
    tri = (col >= row) if rev else (col <= row)
    trit = (row >= col) if rev else (row <= col)
    tri_b = tri.astype(BF16)
    trit_b = trit.astype(BF16)
    lane = lax.broadcasted_iota(jnp.int32, (1, LANES), 1)
    low = lane < SSM_HEADDIM

    def cat3(v):
        return jnp.concatenate(_split3(v), axis=1)

    def step(with_y, k):
        rows = slice(k * CHUNK, (k + 1) * CHUNK)
        dt = _softplus(dt_ref[0, rows, off:off + nh] + bir_ref[...])
        dtt = _softplus(dtt_ref[0, off:off + nh, rows] + bic_ref[...])
        a = dt * (-jnp.exp(alr_ref[...]))
        at = dtt * (-jnp.exp(alc_ref[...]))
        acs = sum(_dot(tri_b, p) for p in _split3(a))
        acst = sum(_dot(p, trit_b) for p in _split3(at))
        tot = acs[end:end + 1, :]
        w_end = dt * jnp.exp(tot - acs)
        eacs = jnp.exp(acs)
        cdec = jnp.exp(tot)
        pieces = [w_end, jnp.broadcast_to(cdec, (SUBLANES, nh))]
        if with_y:
            pieces = [dt, eacs] + pieces
        spread = _dot(cat3(jnp.concatenate(pieces, axis=0)), exp_ref[...])
        r0 = 2 * CHUNK if with_y else 0
        wex = spread[r0:r0 + CHUNK]
        dec = spread[r0 + CHUNK:r0 + CHUNK + 1]
        xs = xs_ref[0, rows, :].astype(F32)
        xcd = (xs * wex).astype(BF16)
        if with_y:
            xc = (xs * spread[0:CHUNK]).astype(BF16)
            eax = spread[CHUNK:2 * CHUNK]
            acs_l = _dot(cat3(acs), sel_ref[...])
        for g in range(SSM_GROUPS):
            gs = slice(g * gw, (g + 1) * gw)
            bm = bm_ref[0, rows, g * D_STATE:(g + 1) * D_STATE]
            bmt = bm.astype(F32).T.astype(BF16)
            sg = st_s[g]
            st_s[g] = sg * dec[:, gs] + _dot(bmt, xcd[:, gs])

            if not with_y:
                continue
            cmb = cm_ref[0, rows, g * D_STATE:(g + 1) * D_STATE].astype(BF16)
            cb = _dot(cmb, bmt)
            yoff = _dot(cmb, sg.astype(BF16))
            for kp in range(hpg // 2):
                e0 = g * hpg + 2 * kp
                c0 = e0 * SSM_HEADDIM
                xcb = xc[:, c0:c0 + LANES]
                res = []
                for e in (e0, e0 + 1):
                    diff = acs_l[:, e * LANES:(e + 1) * LANES] - acst[e:e + 1, :]
                    m = (jnp.exp(jnp.where(tri, diff, -jnp.inf)) * cb).astype(BF16)
                    res.append(_dot(m, xcb))
                y = jnp.where(low, res[0], res[1]) + yoff[:, kp * LANES:(kp + 1) * LANES] * eax[:, c0:c0 + LANES]
                if combine:
                    y = (y + yb_ref[0, rows, c0:c0 + LANES].astype(F32)
                         + dsk_ref[:, c0:c0 + LANES] * xs[:, c0:c0 + LANES])
                y_ref[0, rows, c0:c0 + LANES] = y.astype(y_ref.dtype)

    order = range(SSD_CPS - 1, -1, -1) if rev else range(SSD_CPS)

    @pl.when(c >= n_ctx_steps)
    def _():
        for k in order:
            step(True, k)

    @pl.when(c < n_ctx_steps)
    def _():
        for k in order:
            step(False, k)


def _ssd_scan(xbc, dt, dtt, a_log, dt_bias, ctx_len, rev, y_other=None, d_skip=None):
    bsz, t, _ = xbc.shape
    nh = a_log.shape[0]
    d_inner = nh * SSM_HEADDIM
    gn = SSM_GROUPS * D_STATE
    rows = SSD_CPS * CHUNK
    assert t % rows == 0 and ctx_len % rows == 0
    nc = t // rows
    ncc = ctx_len // rows
    seq_len = t - ctx_len
    combine = y_other is not None

    if rev:
        def chunk(i):
            return jnp.where(i < ncc, ncc - 1 - i, nc - 1 + ncc - i)
    else:
        def chunk(i):
            return i

    def ychunk(i):
        return jnp.maximum(chunk(i), ncc) - ncc if not rev else jnp.where(i < ncc, nc - 1 - ncc, chunk(i) - ncc)

    bcol = d_inner // gn
    in_specs = [pl.BlockSpec((1, rows, d_inner), lambda b, i: (b, chunk(i), 0)),
                pl.BlockSpec((1, rows, gn), lambda b, i: (b, chunk(i), bcol)),
                pl.BlockSpec((1, rows, gn), lambda b, i: (b, chunk(i), bcol + 1)),
                pl.BlockSpec((1, rows, 2 * nh), lambda b, i: (b, chunk(i), 0)),
                pl.BlockSpec((1, 2 * nh, rows), lambda b, i: (b, 0, chunk(i))),
                _resident((1, nh)), _resident((nh, 1)), _resident((1, nh)), _resident((nh, 1)),
                _resident((3 * nh, d_inner)), _resident((3 * nh, nh * LANES))]
    heads = np.arange(nh)[:, None]
    spread_p = jnp.asarray(np.tile(np.arange(d_inner)[None, :] // SSM_HEADDIM == heads, (3, 1)), BF16)
    spread_l = jnp.asarray(np.tile(np.arange(nh * LANES)[None, :] // LANES == heads, (3, 1)), BF16)
    args = [xbc, xbc, xbc, dt, dtt, a_log.reshape(1, nh), a_log.reshape(nh, 1),
            dt_bias.reshape(1, nh), dt_bias.reshape(nh, 1), spread_p, spread_l]
    if combine:
        in_specs += [pl.BlockSpec((1, rows, d_inner), lambda b, i: (b, ychunk(i), 0)), _resident((1, d_inner))]
        args += [y_other, jnp.repeat(d_skip, SSM_HEADDIM).reshape(1, d_inner)]
    return pl.pallas_call(
        functools.partial(_ssd_body, rev=rev, n_ctx_steps=ncc, combine=combine),
        grid=(bsz, nc),
        in_specs=in_specs,
        out_specs=pl.BlockSpec((1, rows, d_inner), lambda b, i: (b, ychunk(i), 0)),
        out_shape=jax.ShapeDtypeStruct((bsz, seq_len, d_inner), BF16),
        scratch_shapes=[pltpu.VMEM((SSM_GROUPS, D_STATE, d_inner // SSM_GROUPS), F32)],
        compiler_params=_params("arbitrary", "arbitrary"),
        name="ssd_bwd" if rev else "ssd_fwd",
    )(*args)


def _odd_out_body(x_ref, y_ref, z_ref, mod_ref, gn_ref, n2_ref, wout_ref, wr_ref, x1_ref, h_ref, ti_ref, tw_ref):
    z = z_ref[0].astype(F32)
    yn = _rms(y_ref[0].astype(F32) * _silu(z), gn_ref[...])
    x1 = x_ref[0] + mod_ref[0, 0, 2:3, :] * _dot(yn.astype(BF16), wout_ref[...])
    x1_ref[0] = x1
    h = _rms(x1, n2_ref[...]) * (1.0 + mod_ref[0, 0, 4:5, :]) + mod_ref[0, 0, 3:4, :]
    h_ref[0] = h
    ne = wr_ref.shape[1]
    h1, h2, _ = _split3(h)
    w1, w2, _ = _split3(wr_ref[...])
    hw = _dot(h1, jnp.concatenate([w1, w2], axis=1))
    lg = hw[:, 0:ne] + hw[:, ne:2 * ne] + _dot(h2, w1)
    idx = lax.broadcasted_iota(jnp.int32, lg.shape, 1)
    m1 = jnp.max(lg, axis=-1, keepdims=True)
    i1 = jnp.min(jnp.where(lg == m1, idx, ne), axis=-1, keepdims=True)
    lg2 = jnp.where(idx == i1, -jnp.inf, lg)
    m2 = jnp.max(lg2, axis=-1, keepdims=True)
    i2 = jnp.min(jnp.where(lg2 == m2, idx, ne), axis=-1, keepdims=True)
    e2 = jnp.exp(m2 - m1)
    den = 1.0 + e2
    ti_ref[0] = jnp.concatenate([i1, i2], axis=1)
    tw_ref[0] = jnp.concatenate([1.0 / den, e2 / den], axis=1)


def _odd_out(x, y, z, mod, gnorm, norm2, w_out, w_router, ctx_len):
    bsz, t, d = x.shape
    seq_len = t - ctx_len
    di = y.shape[2]
    ne = w_router.shape[1]
    nct = ctx_len // TM
    lat = lambda b, i: (b, i + nct, 0)
    own = lambda b, i: (b, i, 0)
    return pl.pallas_call(
        _odd_out_body,
        grid=(bsz, seq_len // TM),
        in_specs=[pl.BlockSpec((1, TM, d), lat),
                  pl.BlockSpec((1, TM, di), own),
                  pl.BlockSpec((1, TM, di), lat),
                  pl.BlockSpec((1, 1, 6, d), lambda b, i: (b, 1, 0, 0)),
                  _resident((1, di)), _resident((1, d)), _resident(w_out.shape), _resident((d, ne))],
        out_specs=[pl.BlockSpec((1, TM, d), own), pl.BlockSpec((1, TM, d), own),
                   pl.BlockSpec((1, TM, TOP_K), own), pl.BlockSpec((1, TM, TOP_K), own)],
        out_shape=[jax.ShapeDtypeStruct((bsz, seq_len, d), F32), jax.ShapeDtypeStruct((bsz, seq_len, d), F32),
                   jax.ShapeDtypeStruct((bsz, seq_len, TOP_K), jnp.int32),
                   jax.ShapeDtypeStruct((bsz, seq_len, TOP_K), F32)],
        compiler_params=_params("arbitrary", "arbitrary"),
        name="odd_out",
    )(x, y, z, mod, gnorm.reshape(1, di), norm2.reshape(1, d), w_out, w_router)


def _expert_body(te_ref, na_ref, rows_ref, next_rows_ref, dst_ref, h_hbm, w1_ref, w3_ref, w2_ref, y_hbm,
                 xf_s, xb_s, acc_s, out_s, gsem, ssem, *, nf, nt):
    ti = pl.program_id(0)
    f = pl.program_id(1)
    real = ti < nt
    active = ti < na_ref[0]
    part = TM_E // nf
    base = f * part
    last = f == nf - 1

    def gather_row(idx_ref, q, j, priority):
        pltpu.make_async_copy(h_hbm.at[pl.ds(idx_ref[0, 0, q * part + j], 1), :],
                              xf_s.at[q, pl.ds(j, 1), :], gsem).start(priority=priority)

    @pl.when(jnp.logical_and(ti == 0, f == 0))
    def _():
        out_s[...] = jnp.zeros_like(out_s)
        for q in range(nf):
            def body(j, carry):
                gather_row(rows_ref, q, 2 * j, 0)
                gather_row(rows_ref, q, 2 * j + 1, 1)
                return carry
            lax.fori_loop(0, part // 2, body, 0)

    @pl.when(f == 0)
    def _():
        for q in range(nf):
            pltpu.make_async_copy(h_hbm.at[pl.ds(0, part), :], xf_s.at[q], gsem).wait()

    @pl.when(jnp.logical_and(active, f == 0))
    def _():
        for q in range(nf):
            xb_s[q * part:(q + 1) * part, :] = xf_s[q].astype(BF16)

    def scatter_part():
        for j in range(part):
            pltpu.make_async_copy(out_s.at[f, pl.ds(j, 1), :],
                                  y_hbm.at[pl.ds(dst_ref[0, 0, base + j], 1), :], ssem).start(priority=1)

    def gather_part():
        for j in range(part):
            gather_row(next_rows_ref, f, j, 1)

    def drain_scatter():
        for q in range(nf):
            pltpu.make_async_copy(out_s.at[q], y_hbm.at[pl.ds(0, part), :], ssem).wait()

    def partial():
        scatter_part()
        gather_part()
        xb = xb_s[...]
        u = _silu(_dot(xb, w1_ref[0])) * _dot(xb, w3_ref[0])
        return _dot(u.astype(BF16), w2_ref[0])

    @pl.when(jnp.logical_and(active, f == 0))
    def _():
        acc_s[...] = partial()

    if nf > 2:
        @pl.when(jnp.logical_and(active, jnp.logical_and(f > 0, jnp.logical_not(last))))
        def _():
            acc_s[...] += partial()

    @pl.when(jnp.logical_and(active, last))
    def _():
        p = partial()
        drain_scatter()
        for q in range(nf):
            out_s[q] = acc_s[q * part:(q + 1) * part, :] + p[q * part:(q + 1) * part, :]

    @pl.when(jnp.logical_and(real, jnp.logical_not(active)))
    def _():
        scatter_part()
        gather_part()

    @pl.when(jnp.logical_not(real))
    def _():
        scatter_part()

    @pl.when(jnp.logical_and(last, jnp.logical_not(active)))
    def _():
        drain_scatter()


def _expert_ffn(h, tile_expert, n_active, row_token, row_dst, n_y_rows, w1, w3, w2):
    n, d = h.shape
    ns = row_token.shape[0]
    nt = ns // TM_E
    dff = w1.shape[2]
    nf = dff // TF_E

    def wcol(ti, f, te, na):
        return (te[jnp.minimum(ti, nt - 1)], 0, jnp.where(ti < na[0], f, nf - 1))

    def wrow(ti, f, te, na):
        return (te[jnp.minimum(ti, nt - 1)], jnp.where(ti < na[0], f, nf - 1), 0)

    def idx_block(shift):
        return pl.BlockSpec((1, 1, TM_E), lambda ti, f, te, na: (jnp.minimum(ti + shift, nt), 0, 0),
                            memory_space=pltpu.SMEM)

    grid_spec = pltpu.PrefetchScalarGridSpec(
        num_scalar_prefetch=2,
        grid=(nt + 1, nf),
        in_specs=[idx_block(0), idx_block(1), idx_block(0),
                  pl.BlockSpec(memory_space=pl.ANY),
                  pl.BlockSpec((1, d, TF_E), wcol),
                  pl.BlockSpec((1, d, TF_E), wcol),
                  pl.BlockSpec((1, TF_E, d), wrow)],
        out_specs=pl.BlockSpec(memory_space=pl.ANY),
        scratch_shapes=[pltpu.VMEM((nf, TM_E // nf, d), F32), pltpu.VMEM((TM_E, d), BF16),
                        pltpu.VMEM((TM_E, d), F32), pltpu.VMEM((nf, TM_E // nf, d), F32),
                        pltpu.SemaphoreType.DMA(()), pltpu.SemaphoreType.DMA(())],
    )
    rows = jnp.concatenate([row_token, jnp.zeros((TM_E,), jnp.int32)]).reshape(nt + 1, 1, TM_E)
    dst = row_dst.reshape(nt + 1, 1, TM_E)
    return pl.pallas_call(
        functools.partial(_expert_body, nf=nf, nt=nt),
        grid_spec=grid_spec,
        out_shape=jax.ShapeDtypeStruct((n_y_rows, d), F32),
        compiler_params=_params("arbitrary", "arbitrary"),
        name="expert_ffn",
    )(tile_expert, n_active, rows, rows, dst, h, w1, w3, w2)


def _combine_body(x_ref, y0_ref, y1_ref, tw_ref, g2_ref, fn_ref, o_ref):
    tw = tw_ref[...]
    moe = tw[:, 0:1] * y0_ref[...] + tw[:, 1:2] * y1_ref[...]
    o_ref[...] = _rms(x_ref[...] + g2_ref[0] * moe, fn_ref[...])


def _moe_combine(x1, topw, g2, final_norm, y, tiles_per_batch):
    n, d = x1.shape
    nt = n // TM_C
    return pl.pallas_call(
        _combine_body,
        grid=(nt,),
        in_specs=[pl.BlockSpec((TM_C, d), lambda i: (i, 0)),
                  pl.BlockSpec((TM_C, d), lambda i: (i, 0)),
                  pl.BlockSpec((TM_C, d), lambda i: (i + nt, 0)),
                  pl.BlockSpec((TM_C, TOP_K), lambda i: (i, 0)),
                  pl.BlockSpec((1, 1, d), lambda i: (i // tiles_per_batch, 0, 0)),
                  _resident((1, d))],
        out_specs=pl.BlockSpec((TM_C, d), lambda i: (i, 0)),
        out_shape=jax.ShapeDtypeStruct((n, d), F32),
        compiler_params=_params("arbitrary"),
        name="moe_combine",
    )(x1, y, y, topw, g2, final_norm.reshape(1, d))


def _route(topi, n_slots):
    n = topi.shape[0]
    e_flat = topi.reshape(n * TOP_K)
    oh = (e_flat[:, None] == jnp.arange(N_EXPERTS, dtype=jnp.int32)[None, :]).astype(jnp.int32)
    csum = jnp.cumsum(oh, axis=0)
    rank = jnp.sum(csum * oh, axis=1) - 1
    cnt = csum[-1]
    cnt_pad = (cnt + TM_E - 1) // TM_E * TM_E
    ends = jnp.cumsum(cnt_pad)
    offs = ends - cnt_pad
    slot = jnp.sum(oh * offs[None, :], axis=1) + rank
    n_active = (ends[-1] // TM_E).astype(jnp.int32)
    nt = n_slots // TM_E
    tile_start = jnp.arange(nt, dtype=jnp.int32) * TM_E
    te = jnp.sum((tile_start[:, None] >= ends[None, :]).astype(jnp.int32), axis=1)
    te = jnp.minimum(te, N_EXPERTS - 1)
    te = jnp.where(jnp.arange(nt) < n_active, te, te[jnp.maximum(n_active - 1, 0)])
    n_routed = n * TOP_K
    owner = jnp.full((n_slots,), -1, jnp.int32).at[slot].set(
        jnp.arange(n_routed, dtype=jnp.int32), unique_indices=True, mode="promise_in_bounds")
    is_pad = owner < 0
    spare = n_routed + TM_E + jnp.cumsum(is_pad.astype(jnp.int32)) - 1
    row_token = jnp.where(is_pad, 0, owner // TOP_K)
    routed = (owner % TOP_K) * n + owner // TOP_K
    row_dst = jnp.concatenate([n_routed + jnp.arange(TM_E, dtype=jnp.int32), jnp.where(is_pad, spare, routed)])
    n_y_rows = n_slots + TM_E
    return te.astype(jnp.int32), n_active.reshape(1), row_token, row_dst, n_y_rows


def kernel(x, c, ctx, c_ctx, ev_ada_w, ev_ada_b, ev_norm1, ev_norm2, ev_w_in, ev_q_gain, ev_k_gain, ev_dw_w, ev_dw_b, ev_ln_g, ev_ln_b, ev_w_o, ev_ff_w1, ev_ff_w3, ev_ff_w2, od_ada_w, od_ada_b, od_norm1, od_norm2, od_w_in, od_conv_w, od_conv_b, od_a_log_f, od_a_log_b, od_dt_bias_f, od_dt_bias_b, od_d_skip, od_gnorm, od_w_out, od_router, od_ex_w1, od_ex_w3, od_ex_w2, final_norm):
    bsz, seq_len, d = x.shape
    ctx_len = ctx.shape[1]
    assert ev_ada_w.shape[0] == 1 and od_ada_w.shape[0] == 1, "one even and one odd layer"
    assert ctx_len % TM == 0 and seq_len % TM == 0 and seq_len % GRID_W == 0
    n_ctx_tiles = ctx_len // TM
    mod = _mod_table(c, c_ctx, ev_ada_w[0], ev_ada_b[0])
    cc = ev_dw_w.shape[2]
    splits = [(0, ATTN_W), (ATTN_W, ATTN_W + KV_W), (ATTN_W + KV_W, ATTN_W + 2 * KV_W),
              (ATTN_W + 2 * KV_W, ATTN_W + 2 * KV_W + 2 * cc)]
    q, k, v, u = _inproj((ctx, x), mod, ev_norm1[0], ev_w_in[0].astype(BF16), splits, (F32, F32, BF16, F32),
                         n_ctx_tiles)
    cos2, sin2 = _rope_tables(ctx_len, seq_len)
    attn = _attention(q, k, v, cos2, sin2, ev_q_gain[0], ev_k_gain[0], ctx_len)
    conv = _conformer_conv(u, ev_dw_w[0], ev_dw_b[0], ev_ln_g[0], ev_ln_b[0], ctx_len)
    xa, (ex_w1, ex_w3, ex_w2) = _even_ffn(
        (ctx, x), attn, conv, mod, ev_norm2[0], ev_w_o[0].astype(BF16), ev_ff_w1[0].astype(BF16),
        ev_ff_w3[0].astype(BF16), ev_ff_w2[0].astype(BF16), n_ctx_tiles, (od_ex_w1[0], od_ex_w3[0], od_ex_w2[0]))

    mod = _mod_table(c, c_ctx, od_ada_w[0], od_ada_b[0])
    nh = od_a_log_f.shape[1]
    d_inner = nh * SSM_HEADDIM
    conv_dim = od_conv_w.shape[2]
    splits = [(0, d_inner), (d_inner, d_inner + conv_dim), (d_inner + conv_dim, d_inner + conv_dim + 2 * nh)]
    z, xbc, dt = _inproj((xa,), mod, od_norm1[0], od_w_in[0].astype(BF16), splits, (BF16, F32, F32), n_ctx_tiles)
    dtt = dt.transpose(0, 2, 1)
    xbc = _ssm_conv(xbc, od_conv_w[0], od_conv_b[0], ctx_len)
    y_b = _ssd_scan(xbc, dt, dtt, od_a_log_b[0], od_dt_bias_b[0], ctx_len, rev=True)
    y = _ssd_scan(xbc, dt, dtt, od_a_log_f[0], od_dt_bias_f[0], ctx_len, rev=False, y_other=y_b,
                  d_skip=od_d_skip[0])
    x1, h, topi, topw = _odd_out(xa, y, z, mod, od_gnorm[0], od_norm2[0], od_w_out[0].astype(BF16),
                                 od_router[0], ctx_len)

    n = bsz * seq_len
    n_slots = n * TOP_K + N_EXPERTS * TM_E
    tile_expert, n_active, row_token, row_dst, n_y_rows = _route(topi.reshape(n, TOP_K), n_slots)
    y_moe = _expert_ffn(h.reshape(n, d), tile_expert, n_active, row_token, row_dst, n_y_rows,
                        ex_w1, ex_w3, ex_w2)
    g2 = mod[:, 1, 5:6, :]
    out = _moe_combine(x1.reshape(n, d), topw.reshape(n, TOP_K), g2, final_norm, y_moe, seq_len // TM_C)
    return out.reshape(bsz, seq_len, d)
```
